```python
import math
import jax
import jax.numpy as jnp
from jax import lax
import numpy as np

D_MODEL = 1024
BATCH = 32
SEQ = 2048
DEPTH = 2

GRID_W = 64
CTX_LEN = 256
N_MOD = 9
D_FF = 2816
FFN_HALF = 0.5
ALPHA = (2 * DEPTH) ** 0.25
BETA = (8 * DEPTH) ** -0.25
N_BRANCH = 4
BRANCH_W = 256
NA_HEADS = 4
NA_DH = 64
NA_KH_MAX = 8
NA_KW = 16
ML_HEADS = 4
ML_DQK = 64
ML_DV = 64
ML_CHUNK = 64
ML_F_BIAS = 3.0
GLA_HEADS = 4
GLA_DK = 32
GLA_DV = 64
GLA_RANK = 16
GLA_TAU = 16.0
GLA_CHUNK = 16
DA_HEADS = 4
DA_DQK = 32
DA_DV = 64
DA_QBLOCK = 128
ROPE_BASE = 10000.0

COLS = (
    ("na_q", NA_HEADS * NA_DH), ("na_k", NA_HEADS * NA_DH), ("na_v", NA_HEADS * NA_DH),
    ("ml_q", ML_HEADS * ML_DQK), ("ml_k", ML_HEADS * ML_DQK), ("ml_v", ML_HEADS * ML_DV),
    ("ml_o", ML_HEADS * ML_DV), ("ml_if", 4 * ML_HEADS),
    ("gla_q", GLA_HEADS * GLA_DK), ("gla_k", GLA_HEADS * GLA_DK), ("gla_v", GLA_HEADS * GLA_DV),
    ("gla_g", GLA_HEADS * GLA_DV), ("gla_a", 2 * GLA_RANK),
    ("da_q", DA_HEADS * 2 * DA_DQK), ("da_k", DA_HEADS * 2 * DA_DQK), ("da_v", DA_HEADS * DA_DV),
    ("gates", N_BRANCH * D_MODEL),
)
N_COLS = sum(w for _, w in COLS)

kernel_name = "hybrid_dit_na_mlstm_gla_diffattn"


def _layer_norm(x, g, b, eps=1e-5):
    xf = x.astype(jnp.float32)
    mu = jnp.mean(xf, axis=-1, keepdims=True)
    var = jnp.mean(jnp.square(xf - mu), axis=-1, keepdims=True)
    return ((xf - mu) * lax.rsqrt(var + eps) * g + b).astype(x.dtype)


def _modulate(x, shift, scale):
    return x * (1 + scale) + shift


def _swiglu(h, w_in, w_out):
    a, b = jnp.split(h @ w_in, 2, axis=-1)
    return (jax.nn.silu(a) * b) @ w_out


def _heads(t, n):
    B, S, W = t.shape
    return t.reshape(B, S, n, W // n).transpose(0, 2, 1, 3)


def _merge(t):
    B, H, S, d = t.shape
    return t.transpose(0, 2, 1, 3).reshape(B, S, H * d)


def _to_chunks(t, size):
    B, H, S = t.shape[:3]
    return jnp.moveaxis(t.reshape(B, H, S // size, size, *t.shape[3:]), 2, 0)


def _from_chunks(t):
    nc, B, H, size = t.shape[:4]
    return jnp.moveaxis(t, 0, 2).reshape(B, H, nc * size, *t.shape[4:])


def _head_norm(y, g, center, eps=1e-6):
    yf = y.astype(jnp.float32)
    if center:
        yf = yf - jnp.mean(yf, axis=-1, keepdims=True)
    yf = yf * lax.rsqrt(jnp.mean(jnp.square(yf), axis=-1, keepdims=True) + eps)
    return _merge(yf) * g


def _split_cols(p):
    out, off = {}, 0
    for name, w in COLS:
        out[name] = p[..., off:off + w]
        off += w
    return out


def _dense_attention(q, k, v):
    s = jnp.einsum("bhqd,bhkd->bhqk", q, k).astype(jnp.float32) * q.shape[-1] ** -0.5
    return jnp.einsum("bhqk,bhkd->bhqd", jax.nn.softmax(s, axis=-1).astype(v.dtype), v)


def _neighbourhood_attention(q, k, v, qc, kc, vc, rpb, rows, ctx_out):
    B, H, S, d = q.shape
    kh = min(NA_KH_MAX, rows)
    scale = d ** -0.5
    qg = q.reshape(B, H, rows, GRID_W, d)
    kg = k.reshape(B, H, rows, GRID_W, d)
    vg = v.reshape(B, H, rows, GRID_W, d)
    col = jnp.arange(GRID_W)
    c0 = jnp.clip(col - NA_KW // 2, 0, GRID_W - NA_KW)
    col_ok = (col[None, :] >= c0[:, None]) & (col[None, :] < c0[:, None] + NA_KW)
    dc_idx = jnp.clip(col[None, :] - col[:, None] + NA_KW - 1, 0, 2 * NA_KW - 2)
    n_loc = kh * GRID_W

    def row_block(r):
        r0 = jnp.clip(r - kh // 2, 0, rows - kh)
        k_blk = lax.dynamic_slice_in_dim(kg, r0, kh, axis=2)
        v_blk = lax.dynamic_slice_in_dim(vg, r0, kh, axis=2)
        q_row = lax.dynamic_index_in_dim(qg, r, axis=2, keepdims=False)
        dr_idx = r0 + jnp.arange(kh) - r + NA_KH_MAX - 1
        bias = rpb[:, dr_idx[None, :, None], dc_idx[:, None, :]]
        s_loc = jnp.einsum("bhqd,bhrkd->bhqrk", q_row, k_blk).astype(jnp.float32) * scale + bias
        s_loc = jnp.where(col_ok[:, None, :], s_loc, -jnp.inf).reshape(B, H, GRID_W, n_loc)
        s_ctx = jnp.einsum("bhqd,bhtd->bhqt", q_row, kc).astype(jnp.float32) * scale
        p = jax.nn.softmax(jnp.concatenate([s_loc, s_ctx], axis=-1), axis=-1).astype(v.dtype)
        p_loc = p[..., :n_loc].reshape(B, H, GRID_W, kh, GRID_W)
        return (jnp.einsum("bhqrk,bhrkd->bhqd", p_loc, v_blk)
                + jnp.einsum("bhqt,bhtd->bhqd", p[..., n_loc:], vc))

    out = _from_chunks(lax.map(row_block, jnp.arange(rows)))
    outc = _dense_attention(qc, kc, vc) if ctx_out else None
    return out, outc


def _mlstm_scan(q, k, v, log_i, log_f, state):
    L = ML_CHUNK
    mask = jnp.tril(jnp.ones((L, L), dtype=bool))

    def step(carry, inp):
        c_mat, n_vec, m_prev = carry
        qb, kb, vb, ib, fb = inp
        b = jnp.cumsum(fb, axis=-1)
        d_log = jnp.where(mask, b[..., :, None] - b[..., None, :] + ib[..., None, :], -jnp.inf)
        inter = b + m_prev[..., None]
        m_t = jnp.maximum(inter, jnp.max(d_log, axis=-1))
        w = jnp.exp(d_log - m_t[..., None])
        w_inter = jnp.exp(inter - m_t)
        qk = jnp.einsum("bhtd,bhsd->bhts", qb, kb) * w
        num = (jnp.einsum("bhts,bhsv->bhtv", qk, vb)
               + w_inter[..., None] * jnp.einsum("bhtd,bhdv->bhtv", qb, c_mat))
        den = jnp.sum(qk, axis=-1) + w_inter * jnp.einsum("bhtd,bhd->bht", qb, n_vec)
        h = num / jnp.maximum(jnp.abs(den), jnp.exp(-m_t))[..., None]
        b_end = b[..., -1]
        k_log = b_end[..., None] - b + ib
        m_new = jnp.maximum(b_end + m_prev, jnp.max(k_log, axis=-1))
        wk = jnp.exp(k_log - m_new[..., None])
        decay = jnp.exp(b_end + m_prev - m_new)
        c_mat = decay[..., None, None] * c_mat + jnp.einsum("bhsd,bhsv->bhdv", kb * wk[..., None], vb)
        n_vec = decay[..., None] * n_vec + jnp.einsum("bhs,bhsd->bhd", wk, kb)
        return (c_mat, n_vec, m_new), h

    xs = tuple(_to_chunks(t, L) for t in (q, k, v, log_i, log_f))
    state, hs = lax.scan(step, state, xs)
    return _from_chunks(hs), state


def _gla_scan(q, k, v, log_a, state):
    L = GLA_CHUNK
    mask = jnp.tril(jnp.ones((L, L), dtype=bool))

    def step(st, inp):
        qb, kb, vb, ab = inp
        b = jnp.cumsum(ab, axis=2)
        rel = jnp.where(mask[:, :, None], b[:, :, :, None, :] - b[:, :, None, :, :], -jnp.inf)
        att = jnp.einsum("bhtd,bhsd,bhtsd->bhts", qb, kb, jnp.exp(rel))
        o = jnp.einsum("bhts,bhsv->bhtv", att, vb) + jnp.einsum("bhtd,bhdv->bhtv", qb * jnp.exp(b), st)
        b_end = b[:, :, -1:, :]
        st = (jnp.exp(b_end[:, :, 0, :])[..., None] * st
              + jnp.einsum("bhsd,bhsv->bhdv", kb * jnp.exp(b_end - b), vb))
        return st, o

    xs = tuple(_to_chunks(t, L) for t in (q, k, v, log_a))
    state, os_ = lax.scan(step, state, xs)
    return _from_chunks(os_), state


def _flip(ts):
    return tuple(jnp.flip(t, axis=2) for t in ts)


def _bidirectional(scan, lat_f, lat_b, ctx_f, ctx_b, init):
    hc_f, st_f = scan(*ctx_f, init)
    h_f, _ = scan(*lat_f, st_f)
    hc_b, st_b = scan(*_flip(ctx_b), init)
    h_b, _ = scan(*_flip(lat_b), st_b)
    return h_f + jnp.flip(h_b, axis=2), hc_f + jnp.flip(hc_b, axis=2)


def _axial_rope_tables(S, dim, dtype):
    t = jnp.arange(S)
    n_f = dim // 4
    inv = ROPE_BASE ** (-jnp.arange(n_f, dtype=jnp.float32) / n_f)

    def cs(pos):
        ang = pos.astype(jnp.float32)[:, None] * inv
        return jnp.cos(ang).astype(dtype), jnp.sin(ang).astype(dtype)

    return cs(t // GRID_W), cs(t % GRID_W)


def _rotate(y, cos, sin):
    y1, y2 = jnp.split(y, 2, axis=-1)
    return jnp.concatenate([y1 * cos - y2 * sin, y1 * sin + y2 * cos], axis=-1)


def _rope_2d(x, row_cs, col_cs):
    xr, xc = jnp.split(x, 2, axis=-1)
    return jnp.concatenate([_rotate(xr, *row_cs), _rotate(xc, *col_cs)], axis=-1)


def _diff_map(q1, q2, k1, k2, v, lam):
    scale = DA_DQK ** -0.5
    a1 = jax.nn.softmax(jnp.einsum("bhqd,bhkd->bhqk", q1, k1).astype(jnp.float32) * scale, axis=-1)
    a2 = jax.nn.softmax(jnp.einsum("bhqd,bhkd->bhqk", q2, k2).astype(jnp.float32) * scale, axis=-1)
    return jnp.einsum("bhqk,bhkd->bhqd", (a1 - lam * a2).astype(v.dtype), v)


def _diff_attention(q1, q2, k1, k2, v, q1c, q2c, k1c, k2c, vc, lam, ctx_out):
    S = q1.shape[2]
    k1a = jnp.concatenate([k1, k1c], axis=2)
    k2a = jnp.concatenate([k2, k2c], axis=2)
    va = jnp.concatenate([v, vc], axis=2)

    def block(i):
        s0 = i * DA_QBLOCK
        return _diff_map(lax.dynamic_slice_in_dim(q1, s0, DA_QBLOCK, axis=2),
                         lax.dynamic_slice_in_dim(q2, s0, DA_QBLOCK, axis=2), k1a, k2a, va, lam)

    out = _from_chunks(lax.map(block, jnp.arange(S // DA_QBLOCK)))
    outc = _diff_map(q1c, q2c, k1c, k2c, vc, lam) if ctx_out else None
    return out, outc


def _prep(p, ml_gate_b, gla_w_a2, gla_b_a):
    f32 = jnp.float32
    P = _split_cols(p)
    g = (P["ml_if"] + ml_gate_b).astype(f32)
    i_f, f_f, i_b, f_b = [t.transpose(0, 2, 1) for t in jnp.split(g, 4, axis=-1)]
    f_f, f_b = jax.nn.log_sigmoid(f_f), jax.nn.log_sigmoid(f_b)
    mq = _heads(P["ml_q"], ML_HEADS).astype(f32) * ML_DQK ** -0.5
    mk = _heads(P["ml_k"], ML_HEADS).astype(f32)
    mv = _heads(P["ml_v"], ML_HEADS).astype(f32)
    gq = _heads(P["gla_q"], GLA_HEADS).astype(f32) * GLA_DK ** -0.5
    gk = _heads(P["gla_k"], GLA_HEADS).astype(f32)
    gv = _heads(P["gla_v"], GLA_HEADS).astype(f32)
    a_f, a_b = jnp.split(P["gla_a"], 2, axis=-1)

    def decay(a, j):
        return _heads(jax.nn.log_sigmoid((a @ gla_w_a2[j] + gla_b_a[j]).astype(f32)) / GLA_TAU, GLA_HEADS)

    dq1, dq2 = jnp.split(_heads(P["da_q"], DA_HEADS), 2, axis=-1)
    dk1, dk2 = jnp.split(_heads(P["da_k"], DA_HEADS), 2, axis=-1)
    return {
        "na": (_heads(P["na_q"], NA_HEADS), _heads(P["na_k"], NA_HEADS), _heads(P["na_v"], NA_HEADS)),
        "ml_f": (mq, mk, mv, i_f, f_f),
        "ml_b": (mq, mk, mv, i_b, f_b),
        "ml_o": P["ml_o"],
        "gla_f": (gq, gk, gv, decay(a_f, 0)),
        "gla_b": (gq, gk, gv, decay(a_b, 1)),
        "gla_g": P["gla_g"],
        "da": (dq1, dq2, dk1, dk2, _heads(P["da_v"], DA_HEADS)),
        "gates": P["gates"],
    }


def _merge_branches(ys, gates, w_branch, w_out):
    g = jnp.split(gates, N_BRANCH, axis=-1)
    acc = jax.nn.sigmoid(g[0]) * (ys[0] @ w_branch[0])
    for j in range(1, N_BRANCH):
        acc = acc + jax.nn.sigmoid(g[j]) * (ys[j] @ w_branch[j])
    return acc @ w_out


def _token_mixer(h, hc, ctx_out, w_mix_in, na_rpb, ml_gate_b, ml_norm_g, gla_w_a2, gla_b_a,
                 gla_norm_g, da_lambda, da_norm_g, lambda_init, w_branch, w_out):
    B, S, _ = h.shape
    rows = S // GRID_W
    dt = h.dtype
    lt = _prep(h @ w_mix_in, ml_gate_b, gla_w_a2, gla_b_a)
    cx = _prep(hc @ w_mix_in, ml_gate_b, gla_w_a2, gla_b_a)

    o_na, oc_na = _neighbourhood_attention(*lt["na"], *cx["na"], na_rpb, rows, ctx_out)

    ml_init = (jnp.zeros((B, ML_HEADS, ML_DQK, ML_DV), jnp.float32),
               jnp.zeros((B, ML_HEADS, ML_DQK), jnp.float32),
               jnp.zeros((B, ML_HEADS), jnp.float32))
    h_ml, hc_ml = _bidirectional(_mlstm_scan, lt["ml_f"], lt["ml_b"], cx["ml_f"], cx["ml_b"], ml_init)

    gla_init = jnp.zeros((B, GLA_HEADS, GLA_DK, GLA_DV), jnp.float32)
    h_gla, hc_gla = _bidirectional(_gla_scan, lt["gla_f"], lt["gla_b"], cx["gla_f"], cx["gla_b"], gla_init)

    row_cs, col_cs = _axial_rope_tables(S, DA_DQK, dt)
    q1, q2, k1, k2, v = lt["da"]
    q1, q2, k1, k2 = [_rope_2d(t, row_cs, col_cs) for t in (q1, q2, k1, k2)]
    lq1, lk1, lq2, lk2 = da_lambda.astype(jnp.float32)
    lam = jnp.exp(jnp.sum(lq1 * lk1)) - jnp.exp(jnp.sum(lq2 * lk2)) + lambda_init
    o_da, oc_da = _diff_attention(q1, q2, k1, k2, v, *cx["da"], lam, ctx_out)

    def finish(t, o_na_, h_ml_, h_gla_, o_da_):
        y_na = _merge(o_na_)
        y_ml = jax.nn.sigmoid(t["ml_o"]) * _head_norm(h_ml_, ml_norm_g, True).astype(dt)
        y_gla = jax.nn.silu(t["gla_g"]) * _head_norm(h_gla_, gla_norm_g, False).astype(dt)
        y_da = ((1.0 - lambda_init) * _head_norm(o_da_, da_norm_g, False)).astype(dt)
        return _merge_branches((y_na, y_ml, y_gla, y_da), t["gates"], w_branch, w_out)

    y = finish(lt, o_na, h_ml, h_gla, o_da)
    yc = finish(cx, oc_na, hc_ml, hc_gla, oc_da) if ctx_out else None
    return y, yc


def setup_inputs(seed: int = 0) -> dict:
    key = jax.random.key(seed)
    ks = jax.random.split(key, 21)

    def nrm(k, shape, s):
        return jax.random.normal(k, shape, jnp.float32) * s

    gate_base = jnp.concatenate([jnp.zeros((ML_HEADS,), jnp.float32), jnp.full((ML_HEADS,), ML_F_BIAS, jnp.float32),
                                 jnp.zeros((ML_HEADS,), jnp.float32), jnp.full((ML_HEADS,), ML_F_BIAS, jnp.float32)])
    return {
        "x": nrm(ks[0], (BATCH, SEQ, D_MODEL), 1.0),
        "c": nrm(ks[1], (BATCH, D_MODEL), 1.0),
        "ctx": nrm(ks[2], (BATCH, CTX_LEN, D_MODEL), 1.0),
        "c_ctx": nrm(ks[3], (D_MODEL,), 1.0),
        "w_ada": nrm(ks[4], (DEPTH, D_MODEL, N_MOD * D_MODEL), D_MODEL ** -0.5),
        "b_ada": nrm(ks[5], (DEPTH, N_MOD * D_MODEL), 0.02),
        "ln_g": 1.0 + nrm(ks[6], (DEPTH, 3, D_MODEL), 0.02),
        "ln_b": nrm(ks[7], (DEPTH, 3, D_MODEL), 0.02),
        "ffn_w_in": nrm(ks[8], (DEPTH, 2, D_MODEL, 2 * D_FF), D_MODEL ** -0.5),
        "ffn_w_out": nrm(ks[9], (DEPTH, 2, D_FF, D_MODEL), BETA * D_FF ** -0.5),
        "w_mix_in": nrm(ks[10], (DEPTH, D_MODEL, N_COLS), D_MODEL ** -0.5),
        "na_rpb": nrm(ks[11], (DEPTH, NA_HEADS, 2 * NA_KH_MAX - 1, 2 * NA_KW - 1), 0.1),
        "ml_gate_b": gate_base + nrm(ks[12], (DEPTH, 4 * ML_HEADS), 0.1),
        "ml_norm_g": 1.0 + nrm(ks[13], (DEPTH, ML_HEADS * ML_DV), 0.02),
        "gla_w_a2": nrm(ks[14], (DEPTH, 2, GLA_RANK, GLA_HEADS * GLA_DK), GLA_RANK ** -0.5),
        "gla_b_a": nrm(ks[15], (DEPTH, 2, GLA_HEADS * GLA_DK), 0.1),
        "gla_norm_g": 1.0 + nrm(ks[16], (DEPTH, GLA_HEADS * GLA_DV), 0.02),
        "da_lambda": nrm(ks[17], (DEPTH, 4, DA_DQK), 0.1),
        "da_norm_g": 1.0 + nrm(ks[18], (DEPTH, DA_HEADS * DA_DV), 0.02),
        "w_branch": nrm(ks[19], (DEPTH, N_BRANCH, BRANCH_W, D_MODEL), BRANCH_W ** -0.5),
        "w_out": nrm(ks[20], (DEPTH, D_MODEL, D_MODEL), BETA * D_MODEL ** -0.5),
    }


def reference(x, c, ctx, c_ctx, w_ada, b_ada, ln_g, ln_b, ffn_w_in, ffn_w_out, w_mix_in, na_rpb,
              ml_gate_b, ml_norm_g, gla_w_a2, gla_b_a, gla_norm_g, da_lambda, da_norm_g, w_branch, w_out):
    lat, cx = x, ctx
    for l in range(DEPTH):
        ctx_out = l < DEPTH - 1
        lambda_init = 0.8 - 0.6 * math.exp(-0.3 * l)
        mod = jnp.split((jax.nn.silu(c) @ w_ada[l] + b_ada[l])[:, None, :], N_MOD, axis=-1)
        modc = jnp.split(jax.nn.silu(c_ctx) @ w_ada[l] + b_ada[l], N_MOD, axis=-1)

        lat = _layer_norm(ALPHA * lat + FFN_HALF * mod[2] * _swiglu(_modulate(lat, mod[0], mod[1]),
                          ffn_w_in[l, 0], ffn_w_out[l, 0]), ln_g[l, 0], ln_b[l, 0])
        cx = _layer_norm(ALPHA * cx + FFN_HALF * modc[2] * _swiglu(_modulate(cx, modc[0], modc[1]),
                         ffn_w_in[l, 0], ffn_w_out[l, 0]), ln_g[l, 0], ln_b[l, 0])

        y, yc = _token_mixer(_modulate(lat, mod[3], mod[4]), _modulate(cx, modc[3], modc[4]), ctx_out,
                             w_mix_in[l], na_rpb[l], ml_gate_b[l], ml_norm_g[l], gla_w_a2[l], gla_b_a[l],
                             gla_norm_g[l], da_lambda[l], da_norm_g[l], lambda_init, w_branch[l], w_out[l])
        lat = _layer_norm(ALPHA * lat + mod[5] * y, ln_g[l, 1], ln_b[l, 1])

        lat = _layer_norm(ALPHA * lat + FFN_HALF * mod[8] * _swiglu(_modulate(lat, mod[6], mod[7]),
                          ffn_w_in[l, 1], ffn_w_out[l, 1]), ln_g[l, 2], ln_b[l, 2])
        if ctx_out:
            cx = _layer_norm(ALPHA * cx + modc[5] * yc, ln_g[l, 1], ln_b[l, 1])
            cx = _layer_norm(ALPHA * cx + FFN_HALF * modc[8] * _swiglu(_modulate(cx, modc[6], modc[7]),
                             ffn_w_in[l, 1], ffn_w_out[l, 1]), ln_g[l, 2], ln_b[l, 2])
    return lat
```

```python
import functools
import math

import jax
import jax.numpy as jnp
import numpy as np
from jax import lax
from jax.experimental import pallas as pl
from jax.experimental.pallas import tpu as pltpu

F32 = jnp.float32
BF16 = jnp.bfloat16

D_MODEL = 1024
SEQ = 2048
CTX_LEN = 256
NTOK = SEQ + CTX_LEN
GRID_W = 64
GRID_ROWS = SEQ // GRID_W
N_MOD = 9
D_FF = 2816
FFN_HALF = 0.5
N_BRANCH = 4
BRANCH_W = 256
N_HEADS = 4
NA_KH = 8
NA_KW = 16
ML_F_OFF = 4
GLA_DK = 32
GLA_RANK = 16
GLA_TAU = 16.0
DA_DQK = 32
ROPE_BASE = 10000.0
LN_EPS = 1e-5
HEAD_EPS = 1e-6
NEG = -1e30

V7X_VMEM_LIMIT = 56 * 1024 * 1024
FF_CHUNK = 256
BLK = 256
N_BLK = NTOK // BLK
N_LAT_BLK = SEQ // BLK
NA_BAND_ROWS = 12
NA_BAND = NA_BAND_ROWS * GRID_W
NA_QROWS = BLK // GRID_W

C_NA = 0
C_ML = 768
C_GLA = 1792
C_SMALL = 2560
C_DA = 2688
N_PROJ = 3968
SMALL_A_OFF = 16


def _dot(a, b):
    return jnp.dot(a, b, preferred_element_type=F32)


def _dot_nt(a, b):
    return lax.dot_general(a, b, (((1,), (1,)), ((), ())), preferred_element_type=F32)


def _dot_tn(a, b):
    return lax.dot_general(a, b, (((0,), (0,)), ((), ())), preferred_element_type=F32)


def _split(x, n):
    parts = []
    r = x
    for i in range(n):
        p = r.astype(BF16)
        parts.append(p)
        if i + 1 < n:
            r = r - p.astype(F32)
    return parts


def _dot_split(a, b, na, nb, dot=_dot):
    pa = _split(a, na) if na > 1 else [a.astype(BF16)]
    pb = _split(b, nb) if nb > 1 else [b.astype(BF16)]
    acc = None
    for i, x in enumerate(pa):
        for j, y in enumerate(pb):
            if i + j < max(na, nb):
                t = dot(x, y)
                acc = t if acc is None else acc + t
    return acc


def _layer_norm(y, g, b):
    mu = jnp.mean(y, axis=-1, keepdims=True)
    yc = y - mu
    var = jnp.mean(yc * yc, axis=-1, keepdims=True)
    return yc * lax.rsqrt(var + LN_EPS) * g + b


def _mod_rows(ref, b, n_batch, is_ctx):
    lat = ref[0, pl.ds(b, 1), :]
    ctx = ref[0, n_batch:n_batch + 1, :]
    return jnp.where(is_ctx, ctx, lat)


def _is_ctx_rows(t, tm):
    return (t * tm + lax.broadcasted_iota(jnp.int32, (tm, 1), 0)) >= SEQ


def _head_mask(width, head, n=1):
    lane = lax.broadcasted_iota(jnp.int32, (n, width), 1)
    per = width // N_HEADS
    return (lane >= head * per) & (lane < (head + 1) * per)


def _stack_heads(x):
    w = x.shape[-1]
    return jnp.concatenate([jnp.where(_head_mask(w, h), x, jnp.zeros_like(x)) for h in range(N_HEADS)], axis=0)


def _unstack_heads(x, m):
    w = x.shape[-1]
    out = jnp.zeros((m, w), x.dtype)
    for h in range(N_HEADS):
        out = jnp.where(_head_mask(w, h), x[h * m:(h + 1) * m], out)
    return out


def _ada_kernel(c_ref, w_ref, b_ref, o_ref):
    c = c_ref[...]
    s = c * jax.nn.sigmoid(c)
    o_ref[0] = _dot_split(s, w_ref[0], 2, 2) + b_ref[0]


def _ada(cc, w_ada, b_ada):
    depth = w_ada.shape[0]
    rows = cc.shape[0]
    return pl.pallas_call(
        _ada_kernel,
        grid=(depth, N_MOD),
        in_specs=[
            pl.BlockSpec((rows, D_MODEL), lambda l, k: (0, 0)),
            pl.BlockSpec((1, D_MODEL, D_MODEL), lambda l, k: (l, 0, k)),
            pl.BlockSpec((1, 1, D_MODEL), lambda l, k: (l, 0, k)),
        ],
        out_specs=pl.BlockSpec((1, rows, D_MODEL), lambda l, k: (l, 0, k)),
        out_shape=jax.ShapeDtypeStruct((depth, rows, N_MOD * D_MODEL), F32),
        compiler_params=pltpu.CompilerParams(
            dimension_semantics=("arbitrary", "arbitrary"), vmem_limit_bytes=V7X_VMEM_LIMIT),
        name="ada_mod",
    )(cc, w_ada, b_ada.reshape(depth, 1, N_MOD * D_MODEL))


def _ffn_kernel(x_ref, sh_ref, sc_ref, gt_ref, w_in_ref, w_out_ref, lng_ref, lnb_ref, o_ref, g_scr,
                *, n_batch, tm, alpha):
    b = pl.program_id(0)
    t = pl.program_id(1)
    is_ctx = _is_ctx_rows(t, tm)
    x = x_ref[0]
    h = (x * (1.0 + _mod_rows(sc_ref, b, n_batch, is_ctx)) + _mod_rows(sh_ref, b, n_batch, is_ctx)).astype(BF16)
    for j in range(D_FF // FF_CHUNK):
        lo = j * FF_CHUNK
        a = _dot(h, w_in_ref[:, lo:lo + FF_CHUNK])
        v = _dot(h, w_in_ref[:, D_FF + lo:D_FF + lo + FF_CHUNK])
        g_scr[:, lo:lo + FF_CHUNK] = (a * jax.nn.sigmoid(a) * v).astype(BF16)
    y = _dot(g_scr[...], w_out_ref[...])
    y = alpha * x + FFN_HALF * _mod_rows(gt_ref, b, n_batch, is_ctx) * y
    o_ref[0] = _layer_norm(y, lng_ref[0], lnb_ref[0])


def _const_spec(shape, index_map):
    return pl.BlockSpec(shape, index_map, pipeline_mode=pl.Buffered(1))


def _mod_spec(rows, layer, k):
    return pl.BlockSpec((1, rows, D_MODEL), lambda b, t: (layer, 0, k))


def _ffn(xs, mod, w_in, w_out, ln_g, ln_b, *, layer, sub, mod0, n_rows, tm, alpha):
    n_batch = xs.shape[0]
    rows = mod.shape[1]
    ln_idx = layer * 3 + (0 if sub == 0 else 2)
    kern = functools.partial(_ffn_kernel, n_batch=n_batch, tm=tm, alpha=alpha)
    return pl.pallas_call(
        kern,
        grid=(n_batch, n_rows // tm),
        in_specs=[
            pl.BlockSpec((1, tm, D_MODEL), lambda b, t: (b, t, 0)),
            _mod_spec(rows, layer, mod0), _mod_spec(rows, layer, mod0 + 1), _mod_spec(rows, layer, mod0 + 2),
            _const_spec((None, None, D_MODEL, 2 * D_FF), lambda b, t: (layer, sub, 0, 0)),
            _const_spec((None, None, D_FF, D_MODEL), lambda b, t: (layer, sub, 0, 0)),
            pl.BlockSpec((1, 1, D_MODEL), lambda b, t: (ln_idx, 0, 0)),
            pl.BlockSpec((1, 1, D_MODEL), lambda b, t: (ln_idx, 0, 0)),
        ],
        out_specs=pl.BlockSpec((1, tm, D_MODEL), lambda b, t: (b, t, 0)),
        out_shape=jax.ShapeDtypeStruct(xs.shape, F32),
        scratch_shapes=[pltpu.VMEM((tm, D_FF), BF16)],
        compiler_params=pltpu.CompilerParams(
            dimension_semantics=("arbitrary", "arbitrary"), vmem_limit_bytes=V7X_VMEM_LIMIT),
        name=f"ffn_l{layer}_s{sub}",
    )(xs, mod, mod, mod, w_in, w_out, ln_g, ln_b)


def _inproj_kernel(x_ref, sh_ref, sc_ref, w_ref, cs_ref, cos_ref, sin_ref,
                   na_ref, ml_ref, gla_ref, sm_ref, da_ref, *, n_batch, tm):
    b = pl.program_id(0)
    t = pl.program_id(1)
    is_ctx = _is_ctx_rows(t, tm)
    x = x_ref[0]
    h = (x * (1.0 + _mod_rows(sc_ref, b, n_batch, is_ctx)) + _mod_rows(sh_ref, b, n_batch, is_ctx)).astype(BF16)
    na_ref[0] = (_dot(h, w_ref[:, C_NA:C_ML]) * cs_ref[:, C_NA:C_ML]).astype(BF16)
    ml_ref[0] = _dot(h, w_ref[:, C_ML:C_GLA]) * cs_ref[:, C_ML:C_GLA]
    gla_ref[0] = _dot(h, w_ref[:, C_GLA:C_SMALL]) * cs_ref[:, C_GLA:C_SMALL]
    sm_ref[0] = _dot(h, w_ref[:, C_SMALL:C_DA])
    pd = _dot(h, w_ref[:, C_DA:N_PROJ])
    cos = cos_ref[...]
    sin = sin_ref[...]
    w = BRANCH_W
    da_ref[0, :, 0:w] = ((pd[:, 0:w] * cos + pd[:, w:2 * w] * sin) * (DA_DQK ** -0.5)).astype(BF16)
    da_ref[0, :, w:2 * w] = (pd[:, 2 * w:3 * w] * cos + pd[:, 3 * w:4 * w] * sin).astype(BF16)
    da_ref[0, :, 2 * w:3 * w] = pd[:, 4 * w:5 * w].astype(BF16)


def _inproj(xs, mod, w_all, colscale, cos_tab, sin_tab, *, layer, tm):
    n_batch = xs.shape[0]
    rows = mod.shape[1]
    kern = functools.partial(_inproj_kernel, n_batch=n_batch, tm=tm)

    def out(width, dtype):
        return (pl.BlockSpec((1, tm, width), lambda b, t: (b, t, 0)),
                jax.ShapeDtypeStruct((n_batch, NTOK, width), dtype))

    outs = [out(768, BF16), out(1024, F32), out(768, F32), out(128, F32), out(768, BF16)]
    return pl.pallas_call(
        kern,
        grid=(n_batch, NTOK // tm),
        in_specs=[
            pl.BlockSpec((1, tm, D_MODEL), lambda b, t: (b, t, 0)),
            _mod_spec(rows, layer, 3), _mod_spec(rows, layer, 4),
            _const_spec((D_MODEL, N_PROJ), lambda b, t: (0, 0)),
            pl.BlockSpec((1, C_SMALL), lambda b, t: (0, 0)),
            pl.BlockSpec((tm, BRANCH_W), lambda b, t: (t, 0)),
            pl.BlockSpec((tm, BRANCH_W), lambda b, t: (t, 0)),
        ],
        out_specs=[o[0] for o in outs],
        out_shape=[o[1] for o in outs],
        compiler_params=pltpu.CompilerParams(
            dimension_semantics=("arbitrary", "arbitrary"), vmem_limit_bytes=V7X_VMEM_LIMIT),
        name=f"inproj_l{layer}",
    )(xs, mod, mod, w_all, colscale, cos_tab, sin_tab)


def _softmax_pv(parts, vals):
    m = None
    for s in parts:
        mi = jnp.max(s, axis=-1, keepdims=True)
        m = mi if m is None else jnp.maximum(m, mi)
    l = None
    o = None
    for s, v in zip(parts, vals):
        p = jnp.exp(s - m)
        li = jnp.sum(p, axis=-1, keepdims=True)
        oi = _dot(p.astype(BF16), v)
        l = li if l is None else l + li
        o = oi if o is None else o + oi
    return o / l


def _na_kernel(q_ref, k_ref, v_ref, bias_ref, o_ref, *, with_ctx):
    i = pl.program_id(0)
    qs = _stack_heads(q_ref[0])
    kc = k_ref[0, SEQ:NTOK, :]
    vc = v_ref[0, SEQ:NTOK, :]

    @pl.when(i < N_LAT_BLK)
    def _():
        start = pl.multiple_of(jnp.clip(i - 1, 0, N_LAT_BLK - NA_BAND // BLK) * BLK, BLK)
        kb = k_ref[0, pl.ds(start, NA_BAND), :]
        vb = v_ref[0, pl.ds(start, NA_BAND), :]
        s_loc = _dot_nt(qs, kb) + bias_ref[...].reshape(N_HEADS * BLK, NA_BAND)
        s_ctx = _dot_nt(qs, kc)
        o_ref[0] = _unstack_heads(_softmax_pv([s_loc, s_ctx], [vb, vc]), BLK).astype(BF16)

    if with_ctx:
        @pl.when(i == N_LAT_BLK)
        def _():
            o_ref[0] = _unstack_heads(_softmax_pv([_dot_nt(qs, kc)], [vc]), BLK).astype(BF16)


def _na_pattern(i):
    return jnp.minimum(i, 1) + (i >= N_LAT_BLK - 2).astype(jnp.int32) + (i >= N_LAT_BLK - 1).astype(jnp.int32)


def _neighbourhood(na, bias, *, layer, with_ctx):
    n_batch = na.shape[0]
    n_blk = N_BLK if with_ctx else N_LAT_BLK
    return pl.pallas_call(
        functools.partial(_na_kernel, with_ctx=with_ctx),
        grid=(n_blk, n_batch),
        in_specs=[
            pl.BlockSpec((1, BLK, BRANCH_W), lambda i, b: (b, i, 0)),
            pl.BlockSpec((1, NTOK, BRANCH_W), lambda i, b: (b, 0, 1)),
            pl.BlockSpec((1, NTOK, BRANCH_W), lambda i, b: (b, 0, 2)),
            pl.BlockSpec((N_HEADS, None, BLK, NA_BAND),
                         lambda i, b: (0, _na_pattern(jnp.minimum(i, N_LAT_BLK - 1)), 0, 0)),
        ],
        out_specs=pl.BlockSpec((1, BLK, BRANCH_W), lambda i, b: (b, i, 0)),
        out_shape=jax.ShapeDtypeStruct((n_batch, n_blk * BLK, BRANCH_W), BF16),
        compiler_params=pltpu.CompilerParams(
            dimension_semantics=("arbitrary", "arbitrary"), vmem_limit_bytes=V7X_VMEM_LIMIT),
        name=f"na_l{layer}",
    )(na, na, na, bias)


def _na_bias_tables(rpb):
    blocks = np.array([0, 1, N_LAT_BLK - 2, N_LAT_BLK - 1])
    qr = blocks[:, None] * NA_QROWS + np.arange(NA_QROWS)[None, :]
    r0 = np.clip(qr - NA_KH // 2, 0, GRID_ROWS - NA_KH)
    band0 = np.clip(blocks - 1, 0, N_LAT_BLK - NA_BAND // BLK) * NA_QROWS
    kr = band0[:, None] + np.arange(NA_BAND_ROWS)[None, :]
    row_ok = (kr[:, None, :] >= r0[:, :, None]) & (kr[:, None, :] < r0[:, :, None] + NA_KH)
    dr = np.clip(kr[:, None, :] - qr[:, :, None] + NA_KH - 1, 0, 2 * NA_KH - 2)
    col = np.arange(GRID_W)
    c0 = np.clip(col - NA_KW // 2, 0, GRID_W - NA_KW)
    col_ok = (col[None, :] >= c0[:, None]) & (col[None, :] < c0[:, None] + NA_KW)
    dc = np.clip(col[None, :] - col[:, None] + NA_KW - 1, 0, 2 * NA_KW - 2)
    ok = row_ok[:, :, None, :, None] & col_ok[None, None, :, None, :]
    dr_b = np.broadcast_to(dr[:, :, None, :, None], ok.shape)
    dc_b = np.broadcast_to(dc[None, None, :, None, :], ok.shape)
    vals = rpb[:, dr_b, dc_b]
    tab = jnp.where(ok[None], vals, NEG)
    return tab.reshape(N_HEADS, 4, BLK, NA_BAND)


def _da_kernel(q_ref, k_ref, v_ref, lam_ref, o_ref, *, lambda_init, with_ctx):
    i = pl.program_id(1)
    lp = lam_ref[...]
    lam = (jnp.exp(jnp.sum(lp[0:1] * lp[1:2], axis=-1, keepdims=True))
           - jnp.exp(jnp.sum(lp[2:3] * lp[3:4], axis=-1, keepdims=True)) + lambda_init)
    q = q_ref[0]
    lane = lax.broadcasted_iota(jnp.int32, (1, BRANCH_W), 1)

    def attend(k, v):
        out = jnp.zeros((BLK, BRANCH_W), F32)
        for h in range(N_HEADS):
            lo = h * 2 * DA_DQK
            q1 = jnp.where((lane >= lo) & (lane < lo + DA_DQK), q, jnp.zeros_like(q))
            q2 = jnp.where((lane >= lo + DA_DQK) & (lane < lo + 2 * DA_DQK), q, jnp.zeros_like(q))
            s = _dot_nt(jnp.concatenate([q1, q2], axis=0), k)
            p = jnp.exp(s - jnp.max(s, axis=-1, keepdims=True))
            r = 1.0 / jnp.sum(p, axis=-1, keepdims=True)
            a = p[:BLK] * r[:BLK] - p[BLK:] * (lam * r[BLK:])
            out = jnp.where(_head_mask(BRANCH_W, h), _dot(a.astype(BF16), v), out)
        return out

    @pl.when(i < N_LAT_BLK)
    def _():
        o_ref[0] = attend(k_ref[0], v_ref[0])

    if with_ctx:
        @pl.when(i == N_LAT_BLK)
        def _():
            o_ref[0] = attend(k_ref[0, SEQ:NTOK, :], v_ref[0, SEQ:NTOK, :])


def _diff_attention(da, lam_p, *, layer, lambda_init, with_ctx):
    n_batch = da.shape[0]
    n_blk = N_BLK if with_ctx else N_LAT_BLK
    return pl.pallas_call(
        functools.partial(_da_kernel, lambda_init=lambda_init, with_ctx=with_ctx),
        grid=(n_batch, n_blk),
        in_specs=[
            pl.BlockSpec((1, BLK, BRANCH_W), lambda b, i: (b, i, 0)),
            pl.BlockSpec((1, NTOK, BRANCH_W), lambda b, i: (b, 0, 1)),
            pl.BlockSpec((1, NTOK, BRANCH_W), lambda b, i: (b, 0, 2)),
            pl.BlockSpec((None, 4, DA_DQK), lambda b, i: (layer, 0, 0)),
        ],
        out_specs=pl.BlockSpec((1, BLK, BRANCH_W), lambda b, i: (b, i, 0)),
        out_shape=jax.ShapeDtypeStruct((n_batch, n_blk * BLK, BRANCH_W), F32),
        compiler_params=pltpu.CompilerParams(
            dimension_semantics=("arbitrary", "arbitrary"), vmem_limit_bytes=V7X_VMEM_LIMIT),
        name=f"da_l{layer}",
    )(da, da, da, lam_p)


def _fwd_chunk(g):
    return jnp.where(g == 0, N_LAT_BLK, g - 1)


def _bwd_chunk(g):
    return jnp.where(g == 0, N_LAT_BLK, N_LAT_BLK - g)


def _tri_mask(reverse):
    r = lax.broadcasted_iota(jnp.int32, (BLK, BLK), 0)
    c = lax.broadcasted_iota(jnp.int32, (BLK, BLK), 1)
    return (c >= r) if reverse else (c <= r)


def _cumsum_rows(x, mask):
    tri = jnp.where(mask, 1.0, 0.0).astype(BF16)
    return _dot_split(tri, x, 1, 3)


def _mlstm_chunk(d, q_ref, k_ref, v_ref, s_ref, gb_ref, o_ref, c_scr, n_scr, m_scr):
    reverse = d == 1
    mask = _tri_mask(reverse)
    end = 0 if reverse else BLK - 1
    q = q_ref[0]
    k = k_ref[0]
    v = v_ref[0]
    qb = q.astype(BF16)
    kb = k.astype(BF16)
    vb = v.astype(BF16)
    gs = s_ref[0] + gb_ref[...]
    lane = lax.broadcasted_iota(jnp.int32, (1, 128), 1)
    f_lo = 2 * N_HEADS * d + ML_F_OFF
    is_f = (lane >= f_lo) & (lane < f_lo + N_HEADS)
    gl = jnp.where(is_f, jax.nn.log_sigmoid(gs), gs)
    bc = _cumsum_rows(gl, mask)
    gl_t = gl.T
    bc_t = bc.T
    m_row = m_scr[d, 0:1, :]
    s_all = _dot_nt(_stack_heads(qb), kb)
    q_c = _dot(qb, c_scr[d].astype(BF16))
    q_n = _dot(qb, n_scr[d].astype(BF16))
    qk_parts, w_inters, m_ts, dens = [], [], [], []
    wk_l = jnp.zeros((BLK, BRANCH_W), F32)
    decay_l = jnp.zeros((1, BRANCH_W), F32)
    decay_n = jnp.zeros((1, 128), F32)
    m_new_row = m_row
    for h in range(N_HEADS):
        ci = 2 * N_HEADS * d + h
        cf = f_lo + h
        b_col = bc[:, cf:cf + 1]
        b_row = bc_t[cf:cf + 1, :]
        i_row = gl_t[ci:ci + 1, :]
        i_col = gl[:, ci:ci + 1]
        m_prev = m_row[:, h:h + 1]
        d_log = jnp.where(mask, b_col - b_row + i_row, NEG)
        inter = b_col + m_prev
        m_t = jnp.maximum(inter, jnp.max(d_log, axis=-1, keepdims=True))
        w = jnp.exp(d_log - m_t)
        w_inter = jnp.exp(inter - m_t)
        qk = s_all[h * BLK:(h + 1) * BLK] * w
        dens.append(jnp.sum(qk, axis=-1, keepdims=True) + w_inter * q_n[:, h:h + 1])
        qk_parts.append(qk.astype(BF16))
        w_inters.append(w_inter)
        m_ts.append(m_t)
        b_end = bc[end:end + 1, cf:cf + 1]
        k_log = b_end - b_col + i_col
        m_new = jnp.maximum(b_end + m_prev, jnp.max(k_log, axis=0, keepdims=True))
        wk = jnp.exp(k_log - m_new)
        decay = jnp.exp(b_end + m_prev - m_new)
        hm = _head_mask(BRANCH_W, h)
        wk_l = jnp.where(hm, wk, wk_l)
        decay_l = jnp.where(hm, decay, decay_l)
        decay_n = jnp.where(lane == h, decay, decay_n)
        m_new_row = jnp.where(lane == h, m_new, m_new_row)
    num_all = _dot(jnp.concatenate(qk_parts, axis=0), vb)
    out = jnp.zeros((BLK, BRANCH_W), F32)
    for h in range(N_HEADS):
        num = num_all[h * BLK:(h + 1) * BLK] + w_inters[h] * q_c
        hv = num / jnp.maximum(jnp.abs(dens[h]), jnp.exp(-m_ts[h]))
        out = jnp.where(_head_mask(BRANCH_W, h), hv, out)
    o_ref[0] = out
    kw = (k * wk_l).astype(BF16)
    r_head = lax.broadcasted_iota(jnp.int32, (BRANCH_W, BRANCH_W), 0) // (BRANCH_W // N_HEADS)
    c_head = lax.broadcasted_iota(jnp.int32, (BRANCH_W, BRANCH_W), 1) // (BRANCH_W // N_HEADS)
    c_scr[d] = decay_l * c_scr[d] + jnp.where(r_head == c_head, _dot_tn(kw, vb), 0.0)
    n_sum = _dot_tn(kw, jnp.ones((BLK, 128), BF16))
    n_sel = (lax.broadcasted_iota(jnp.int32, (BRANCH_W, 128), 0) // (BRANCH_W // N_HEADS)
             == lax.broadcasted_iota(jnp.int32, (BRANCH_W, 128), 1))
    n_scr[d] = decay_n * n_scr[d] + jnp.where(n_sel, n_sum, 0.0)
    m_scr[d, 0:1, :] = m_new_row


def _mlstm_kernel(qf, kf, vf, sf, qb, kb, vb, sb, gb_ref, o_ref_f, o_ref_b, c_scr, n_scr, m_scr):
    @pl.when(pl.program_id(1) == 0)
    def _():
        c_scr[...] = jnp.zeros_like(c_scr)
        n_scr[...] = jnp.zeros_like(n_scr)
        m_scr[...] = jnp.zeros_like(m_scr)

    _mlstm_chunk(0, qf, kf, vf, sf, gb_ref, o_ref_f, c_scr, n_scr, m_scr)
    _mlstm_chunk(1, qb, kb, vb, sb, gb_ref, o_ref_b, c_scr, n_scr, m_scr)


def _scan_specs(width_blocks, chunk_fn):
    return [pl.BlockSpec((1, BLK, BRANCH_W), lambda b, g, j=j: (b, chunk_fn(g), j)) for j in width_blocks]


def _mlstm(ml, small, gate_b, *, layer):
    n_batch = ml.shape[0]
    small_spec = lambda fn: pl.BlockSpec((1, BLK, 128), lambda b, g: (b, fn(g), 0))
    out_spec = lambda fn: pl.BlockSpec((1, BLK, BRANCH_W), lambda b, g: (b, fn(g), 0))
    return pl.pallas_call(
        _mlstm_kernel,
        grid=(n_batch, N_BLK),
        in_specs=(_scan_specs((0, 1, 2), _fwd_chunk) + [small_spec(_fwd_chunk)]
                  + _scan_specs((0, 1, 2), _bwd_chunk) + [small_spec(_bwd_chunk)]
                  + [pl.BlockSpec((1, 128), lambda b, g: (0, 0))]),
        out_specs=[out_spec(_fwd_chunk), out_spec(_bwd_chunk)],
        out_shape=[jax.ShapeDtypeStruct((n_batch, NTOK, BRANCH_W), F32)] * 2,
        scratch_shapes=[pltpu.VMEM((2, BRANCH_W, BRANCH_W), F32), pltpu.VMEM((2, BRANCH_W, 128), F32),
                        pltpu.VMEM((2, 8, 128), F32)],
        compiler_params=pltpu.CompilerParams(
            dimension_semantics=("arbitrary", "arbitrary"), vmem_limit_bytes=V7X_VMEM_LIMIT),
        name=f"mlstm_l{layer}",
    )(ml, ml, ml, small, ml, ml, ml, small, gate_b)


def _gla_chunk(d, qk_ref, v_ref, s_ref, w2_ref, ba_ref, o_ref, st_scr):
    reverse = d == 1
    mask = _tri_mask(reverse)
    end = 0 if reverse else BLK - 1
    half = BLK // 2
    q = qk_ref[0, :, 0:128]
    k = qk_ref[0, :, 128:256]
    vb = v_ref[0].astype(BF16)
    x = _dot_split(s_ref[0], w2_ref[d], 3, 3) + ba_ref[d]
    la = jax.nn.log_sigmoid(x) * (1.0 / GLA_TAU)
    bc = _cumsum_rows(la, mask)
    ref_row = bc[half:half + 1, :]
    qe = (q * jnp.exp(bc - ref_row)).astype(BF16)
    ke = (k * jnp.exp(ref_row - bc)).astype(BF16)
    att = _dot_nt(_stack_heads(qe), ke)
    mask4 = jnp.concatenate([mask] * N_HEADS, axis=0)
    o_all = _dot(jnp.where(mask4, att, 0.0).astype(BF16), vb)
    st = st_scr[d]
    inter = _dot_nt((q * jnp.exp(bc)).astype(BF16), st.astype(BF16))
    o_ref[0] = _unstack_heads(o_all, BLK) + inter
    b_end = bc[end:end + 1, :]
    kend = (k * jnp.exp(b_end - bc)).astype(BF16)
    r_head = lax.broadcasted_iota(jnp.int32, (BRANCH_W, 128), 0) // (BRANCH_W // N_HEADS)
    c_head = lax.broadcasted_iota(jnp.int32, (BRANCH_W, 128), 1) // GLA_DK
    st_scr[d] = jnp.exp(b_end) * st + jnp.where(r_head == c_head, _dot_tn(vb, kend), 0.0)


def _gla_kernel(qkf, vf, sf, qkb, vb, sb, w2_ref, ba_ref, o_ref_f, o_ref_b, st_scr):
    @pl.when(pl.program_id(1) == 0)
    def _():
        st_scr[...] = jnp.zeros_like(st_scr)

    _gla_chunk(0, qkf, vf, sf, w2_ref, ba_ref, o_ref_f, st_scr)
    _gla_chunk(1, qkb, vb, sb, w2_ref, ba_ref, o_ref_b, st_scr)


def _gla(gla, small, w2pad, ba, *, layer):
    n_batch = gla.shape[0]
    small_spec = lambda fn: pl.BlockSpec((1, BLK, 128), lambda b, g: (b, fn(g), 0))
    out_spec = lambda fn: pl.BlockSpec((1, BLK, BRANCH_W), lambda b, g: (b, fn(g), 0))
    return pl.pallas_call(
        _gla_kernel,
        grid=(n_batch, N_BLK),
        in_specs=(_scan_specs((0, 1), _fwd_chunk) + [small_spec(_fwd_chunk)]
                  + _scan_specs((0, 1), _bwd_chunk) + [small_spec(_bwd_chunk)]
                  + [pl.BlockSpec((2, 128, 128), lambda b, g: (0, 0, 0)),
                     pl.BlockSpec((2, 1, 128), lambda b, g: (0, 0, 0))]),
        out_specs=[out_spec(_fwd_chunk), out_spec(_bwd_chunk)],
        out_shape=[jax.ShapeDtypeStruct((n_batch, NTOK, BRANCH_W), F32)] * 2,
        scratch_shapes=[pltpu.VMEM((2, BRANCH_W, 128), F32)],
        compiler_params=pltpu.CompilerParams(
            dimension_semantics=("arbitrary", "arbitrary"), vmem_limit_bytes=V7X_VMEM_LIMIT),
        name=f"gla_l{layer}",
    )(gla, gla, small, gla, gla, small, w2pad, ba)


def _group_mean(x, gmat):
    return _dot_split(x, gmat, 2, 1)


def _merge_kernel(x_ref, sh_ref, sc_ref, gt_ref, na_ref, mlf_ref, mlb_ref, mlo_ref, glf_ref, glb_ref, glg_ref,
                  da_ref, ng_ref, gm_ref, wg_ref, wb_ref, wo_ref, lng_ref, lnb_ref, o_ref,
                  *, n_batch, tm, alpha, lambda_init):
    b = pl.program_id(0)
    t = pl.program_id(1)
    is_ctx = _is_ctx_rows(t, tm)
    x = x_ref[0]
    h = (x * (1.0 + _mod_rows(sc_ref, b, n_batch, is_ctx)) + _mod_rows(sh_ref, b, n_batch, is_ctx)).astype(BF16)
    gmat = gm_ref[...]

    def rms_norm(y, gain):
        return y * lax.rsqrt(_group_mean(y * y, gmat) + HEAD_EPS) * gain

    h_ml = mlf_ref[0] + mlb_ref[0]
    y_ml = jax.nn.sigmoid(mlo_ref[0]) * rms_norm(h_ml - _group_mean(h_ml, gmat), ng_ref[0:1, :])
    g_gl = glg_ref[0]
    y_gla = g_gl * jax.nn.sigmoid(g_gl) * rms_norm(glf_ref[0] + glb_ref[0], ng_ref[1:2, :])
    y_da = (1.0 - lambda_init) * rms_norm(da_ref[0], ng_ref[2:3, :])
    ys = [na_ref[0], y_ml.astype(BF16), y_gla.astype(BF16), y_da.astype(BF16)]
    acc = None
    for j in range(N_BRANCH):
        gate = jax.nn.sigmoid(_dot(h, wg_ref[:, j * D_MODEL:(j + 1) * D_MODEL]))
        term = gate * _dot(ys[j], wb_ref[j])
        acc = term if acc is None else acc + term
    y = _dot(acc.astype(BF16), wo_ref[...])
    y = alpha * x + _mod_rows(gt_ref, b, n_batch, is_ctx) * y
    o_ref[0] = _layer_norm(y, lng_ref[0], lnb_ref[0])


def _merge(xs, mod, na_o, ml_o, ml, gla_o, gla, da_o, norm_g, gmat, w_gates, w_branch, w_out, ln_g, ln_b,
           *, layer, n_rows, tm, alpha, lambda_init):
    n_batch = xs.shape[0]
    rows = mod.shape[1]
    ln_idx = layer * 3 + 1
    kern = functools.partial(_merge_kernel, n_batch=n_batch, tm=tm, alpha=alpha, lambda_init=lambda_init)
    tok = lambda j: pl.BlockSpec((1, tm, BRANCH_W), lambda b, t: (b, t, j))
    return pl.pallas_call(
        kern,
        grid=(n_batch, n_rows // tm),
        in_specs=[
            pl.BlockSpec((1, tm, D_MODEL), lambda b, t: (b, t, 0)),
            _mod_spec(rows, layer, 3), _mod_spec(rows, layer, 4), _mod_spec(rows, layer, 5),
            tok(0), tok(0), tok(0), tok(3), tok(0), tok(0), tok(2), tok(0),
            pl.BlockSpec((8, BRANCH_W), lambda b, t: (0, 0)),
            pl.BlockSpec((BRANCH_W, BRANCH_W), lambda b, t: (0, 0)),
            _const_spec((D_MODEL, N_BRANCH * D_MODEL), lambda b, t: (0, 0)),
            _const_spec((None, N_BRANCH, BRANCH_W, D_MODEL), lambda b, t: (layer, 0, 0, 0)),
            _const_spec((None, D_MODEL, D_MODEL), lambda b, t: (layer, 0, 0)),
            pl.BlockSpec((1, 1, D_MODEL), lambda b, t: (ln_idx, 0, 0)),
            pl.BlockSpec((1, 1, D_MODEL), lambda b, t: (ln_idx, 0, 0)),
        ],
        out_specs=pl.BlockSpec((1, tm, D_MODEL), lambda b, t: (b, t, 0)),
        out_shape=jax.ShapeDtypeStruct((n_batch, n_rows, D_MODEL), F32),
        compiler_params=pltpu.CompilerParams(
            dimension_semantics=("arbitrary", "arbitrary"), vmem_limit_bytes=V7X_VMEM_LIMIT),
        name=f"merge_l{layer}",
    )(xs, mod, mod, mod, na_o, ml_o[0], ml_o[1], ml, gla_o[0], gla_o[1], gla, da_o, norm_g, gmat,
      w_gates, w_branch, w_out, ln_g, ln_b)


def _rope_tables():
    t = jnp.arange(SEQ)
    n_f = DA_DQK // 4
    inv = ROPE_BASE ** (-jnp.arange(n_f, dtype=F32) / n_f)

    def cs(pos):
        ang = pos.astype(F32)[:, None] * inv
        return jnp.cos(ang), jnp.sin(ang)

    (cr, sr), (cc, sc) = cs(t // GRID_W), cs(t % GRID_W)
    cos32 = jnp.concatenate([cr, cr, cc, cc], axis=-1)
    sin32 = jnp.concatenate([-sr, sr, -sc, sc], axis=-1)
    reps = BRANCH_W // DA_DQK
    cos = jnp.concatenate([jnp.tile(cos32, (1, reps)), jnp.ones((CTX_LEN, BRANCH_W), F32)], axis=0)
    sin = jnp.concatenate([jnp.tile(sin32, (1, reps)), jnp.zeros((CTX_LEN, BRANCH_W), F32)], axis=0)
    return cos, sin


def _swap_perm():
    n_f = DA_DQK // 4
    base = np.concatenate([np.arange(n_f, 2 * n_f), np.arange(0, n_f),
                           np.arange(3 * n_f, 4 * n_f), np.arange(2 * n_f, 3 * n_f)])
    return np.concatenate([g * DA_DQK + base for g in range(BRANCH_W // DA_DQK)])


def _prep_mix_weights(w):
    na, ml, ml_if = w[:, 0:768], w[:, 768:1792], w[:, 1792:1808]
    gla, gla_a = w[:, 1808:2576], w[:, 2576:2608]
    dq, dk, dv = w[:, 2608:2864], w[:, 2864:3120], w[:, 3120:3376]
    gates = w[:, 3376:]
    perm = _swap_perm()
    pad = jnp.zeros((D_MODEL, 128 - ml_if.shape[1] - gla_a.shape[1]), w.dtype)
    w_all = jnp.concatenate([na, ml, gla, ml_if, gla_a, pad, dq, dq[:, perm], dk, dk[:, perm], dv], axis=1)
    return w_all.astype(BF16), gates.astype(BF16)


def _col_scale():
    cs = np.ones((1, C_SMALL), np.float32)
    cs[:, C_NA:C_NA + 256] = 64 ** -0.5
    cs[:, C_ML:C_ML + 256] = 64 ** -0.5
    cs[:, C_GLA:C_GLA + 128] = GLA_DK ** -0.5
    return jnp.asarray(cs)


def _group_mean_matrix():
    head = np.arange(BRANCH_W) // (BRANCH_W // N_HEADS)
    return jnp.asarray((head[:, None] == head[None, :]) / (BRANCH_W // N_HEADS), dtype=BF16)


def kernel(x, c, ctx, c_ctx, w_ada, b_ada, ln_g, ln_b, ffn_w_in, ffn_w_out, w_mix_in, na_rpb, ml_gate_b,
           ml_norm_g, gla_w_a2, gla_b_a, gla_norm_g, da_lambda, da_norm_g, w_branch, w_out):
    n_batch = x.shape[0]
    depth = w_ada.shape[0]
    alpha = (2 * depth) ** 0.25
    mod_rows = -(-(n_batch + 1) // 8) * 8
    cc = jnp.concatenate([c, c_ctx[None, :], jnp.zeros((mod_rows - n_batch - 1, D_MODEL), F32)], axis=0)
    mod = _ada(cc, w_ada, b_ada)

    xs = jnp.concatenate([x, ctx], axis=1)
    w_in_bf = ffn_w_in.astype(BF16)
    w_out_bf = ffn_w_out.astype(BF16)
    w_branch_bf = w_branch.astype(BF16)
    w_o_bf = w_out.astype(BF16)
    ln_g3 = ln_g.reshape(depth * 3, 1, D_MODEL)
    ln_b3 = ln_b.reshape(depth * 3, 1, D_MODEL)
    cos_tab, sin_tab = _rope_tables()
    colscale = _col_scale()
    gmat = _group_mean_matrix()

    for l in range(depth):
        ctx_out = l < depth - 1
        lambda_init = 0.8 - 0.6 * math.exp(-0.3 * l)
        w_all, w_gates = _prep_mix_weights(w_mix_in[l])
        bias = _na_bias_tables(na_rpb[l])
        gate_b = jnp.concatenate([ml_gate_b[l], jnp.zeros((128 - ml_gate_b.shape[1],), F32)])[None, :]
        w2pad = jnp.zeros((2, 128, 128), F32)
        for d in range(2):
            lo = SMALL_A_OFF + d * GLA_RANK
            w2pad = w2pad.at[d, lo:lo + GLA_RANK, :].set(gla_w_a2[l, d])
        ba = gla_b_a[l][:, None, :]
        norm_g = jnp.concatenate([ml_norm_g[l][None], gla_norm_g[l][None], da_norm_g[l][None],
                                  jnp.zeros((5, BRANCH_W), F32)], axis=0)

        xs = _ffn(xs, mod, w_in_bf, w_out_bf, ln_g3, ln_b3, layer=l, sub=0, mod0=0,
                  n_rows=NTOK, tm=768, alpha=alpha)
        na, ml, gla, small, da = _inproj(xs, mod, w_all, colscale, cos_tab, sin_tab, layer=l, tm=768)
        na_o = _neighbourhood(na, bias, layer=l, with_ctx=ctx_out)
        da_o = _diff_attention(da, da_lambda, layer=l, lambda_init=lambda_init, with_ctx=ctx_out)
        ml_o = _mlstm(ml, small, gate_b, layer=l)
        gla_o = _gla(gla, small, w2pad, ba, layer=l)
        n_rows, tm = (NTOK, 768) if ctx_out else (SEQ, 512)
        xs = _merge(xs, mod, na_o, ml_o, ml, gla_o, gla, da_o, norm_g, gmat, w_gates, w_branch_bf, w_o_bf,
                    ln_g3, ln_b3, layer=l, n_rows=n_rows, tm=tm, alpha=alpha, lambda_init=lambda_init)
        xs = _ffn(xs, mod, w_in_bf, w_out_bf, ln_g3, ln_b3, layer=l, sub=1, mod0=6,
                  n_rows=n_rows, tm=tm, alpha=alpha)
    return xs
```

```python
import functools
import math

import jax
import jax.numpy as jnp
import numpy as np
from jax import lax
from jax.experimental import pallas as pl
from jax.experimental.pallas import tpu as pltpu

F32 = jnp.float32
BF16 = jnp.bfloat16

D_MODEL = 1024
SEQ = 2048
CTX_LEN = 256
NTOK = SEQ + CTX_LEN
GRID_W = 64
GRID_ROWS = SEQ // GRID_W
N_MOD = 9
D_FF = 2816
FFN_HALF = 0.5
N_BRANCH = 4
BRANCH_W = 256
N_HEADS = 4
NA_KH = 8
NA_KW = 16
ML_F_OFF = 4
GLA_DK = 32
GLA_RANK = 16
GLA_TAU = 16.0
DA_DQK = 32
ROPE_BASE = 10000.0
LN_EPS = 1e-5
HEAD_EPS = 1e-6
NEG = -1e30
LOG2E = math.log2(math.e)

V7X_VMEM_LIMIT = 56 * 1024 * 1024
FF_CHUNK = 256
BLK = 256
N_BLK = NTOK // BLK
N_LAT_BLK = SEQ // BLK
NA_BAND_ROWS = 12
NA_BAND = NA_BAND_ROWS * GRID_W
NA_QROWS = BLK // GRID_W

C_NA = 0
C_ML = 768
C_GLA = 1792
C_SMALL = 2560
C_DA = 2688
N_PROJ = 3968
SMALL_A_OFF = 16


def _dot(a, b):
    return jnp.dot(a, b, preferred_element_type=F32)


def _dot_nt(a, b):
    return lax.dot_general(a, b, (((1,), (1,)), ((), ())), preferred_element_type=F32)


def _dot_tn(a, b):
    return lax.dot_general(a, b, (((0,), (0,)), ((), ())), preferred_element_type=F32)


def _split(x, n):
    parts = []
    r = x
    for i in range(n):
        p = r.astype(BF16)
        parts.append(p)
        if i + 1 < n:
            r = r - p.astype(F32)
    return parts


def _dot_split(a, b, na, nb, dot=_dot):
    pa = _split(a, na) if na > 1 else [a.astype(BF16)]
    pb = _split(b, nb) if nb > 1 else [b.astype(BF16)]
    acc = None
    for i, x in enumerate(pa):
        for j, y in enumerate(pb):
            if i + j < max(na, nb):
                t = dot(x, y)
                acc = t if acc is None else acc + t
    return acc


def _layer_norm(y, g, b):
    mu = jnp.mean(y, axis=-1, keepdims=True)
    yc = y - mu
    var = jnp.mean(yc * yc, axis=-1, keepdims=True)
    return yc * lax.rsqrt(var + LN_EPS) * g + b


def _mod_rows(ref, b, n_batch, is_ctx):
    lat = ref[0, pl.ds(b, 1), :]
    ctx = ref[0, n_batch:n_batch + 1, :]
    return jnp.where(is_ctx, ctx, lat)


def _is_ctx_rows(t, tm):
    return (t * tm + lax.broadcasted_iota(jnp.int32, (tm, 1), 0)) >= SEQ


def _head_mask(width, head, n=1):
    lane = lax.broadcasted_iota(jnp.int32, (n, width), 1)
    per = width // N_HEADS
    return (lane >= head * per) & (lane < (head + 1) * per)


def _stack_heads(x):
    w = x.shape[-1]
    return jnp.concatenate([jnp.where(_head_mask(w, h), x, jnp.zeros_like(x)) for h in range(N_HEADS)], axis=0)


def _unstack_heads(x, m):
    w = x.shape[-1]
    out = jnp.zeros((m, w), x.dtype)
    for h in range(N_HEADS):
        out = jnp.where(_head_mask(w, h), x[h * m:(h + 1) * m], out)
    return out


def _ada_kernel(c_ref, w_ref, b_ref, o_ref):
    c = c_ref[...]
    s = c * jax.nn.sigmoid(c)
    o_ref[0] = _dot_split(s, w_ref[0], 2, 2) + b_ref[0]


def _ada(cc, w_ada, b_ada):
    depth = w_ada.shape[0]
    rows = cc.shape[0]
    return pl.pallas_call(
        _ada_kernel,
        grid=(depth, N_MOD),
        in_specs=[
            pl.BlockSpec((rows, D_MODEL), lambda l, k: (0, 0)),
            pl.BlockSpec((1, D_MODEL, D_MODEL), lambda l, k: (l, 0, k)),
            pl.BlockSpec((1, 1, D_MODEL), lambda l, k: (l, 0, k)),
        ],
        out_specs=pl.BlockSpec((1, rows, D_MODEL), lambda l, k: (l, 0, k)),
        out_shape=jax.ShapeDtypeStruct((depth, rows, N_MOD * D_MODEL), F32),
        compiler_params=pltpu.CompilerParams(
            dimension_semantics=("arbitrary", "arbitrary"), vmem_limit_bytes=V7X_VMEM_LIMIT),
        name="ada_mod",
    )(cc, w_ada, b_ada.reshape(depth, 1, N_MOD * D_MODEL))


def _ffn_kernel(x_ref, sh_ref, sc_ref, gt_ref, w_in_ref, w_out_ref, lng_ref, lnb_ref, o_ref, g_scr,
                *, n_batch, tm, alpha):
    b = pl.program_id(0)
    t = pl.program_id(1)
    is_ctx = _is_ctx_rows(t, tm)
    x = x_ref[0]
    h = (x * (1.0 + _mod_rows(sc_ref, b, n_batch, is_ctx)) + _mod_rows(sh_ref, b, n_batch, is_ctx)).astype(BF16)
    for j in range(D_FF // FF_CHUNK):
        lo = j * FF_CHUNK
        a = _dot(h, w_in_ref[:, lo:lo + FF_CHUNK])
        v = _dot(h, w_in_ref[:, D_FF + lo:D_FF + lo + FF_CHUNK])
        g_scr[:, lo:lo + FF_CHUNK] = (a * jax.nn.sigmoid(a) * v).astype(BF16)
    y = _dot(g_scr[...], w_out_ref[...])
    y = alpha * x + FFN_HALF * _mod_rows(gt_ref, b, n_batch, is_ctx) * y
    o_ref[0] = _layer_norm(y, lng_ref[0], lnb_ref[0])


def _const_spec(shape, index_map):
    return pl.BlockSpec(shape, index_map, pipeline_mode=pl.Buffered(1))


def _mod_spec(rows, layer, k):
    return pl.BlockSpec((1, rows, D_MODEL), lambda b, t: (layer, 0, k))


def _ffn(xs, mod, w_in, w_out, ln_g, ln_b, *, layer, sub, mod0, n_rows, tm, alpha):
    n_batch = xs.shape[0]
    rows = mod.shape[1]
    ln_idx = layer * 3 + (0 if sub == 0 else 2)
    kern = functools.partial(_ffn_kernel, n_batch=n_batch, tm=tm, alpha=alpha)
    return pl.pallas_call(
        kern,
        grid=(n_batch, n_rows // tm),
        in_specs=[
            pl.BlockSpec((1, tm, D_MODEL), lambda b, t: (b, t, 0)),
            _mod_spec(rows, layer, mod0), _mod_spec(rows, layer, mod0 + 1), _mod_spec(rows, layer, mod0 + 2),
            _const_spec((None, None, D_MODEL, 2 * D_FF), lambda b, t: (layer, sub, 0, 0)),
            _const_spec((None, None, D_FF, D_MODEL), lambda b, t: (layer, sub, 0, 0)),
            pl.BlockSpec((1, 1, D_MODEL), lambda b, t: (ln_idx, 0, 0)),
            pl.BlockSpec((1, 1, D_MODEL), lambda b, t: (ln_idx, 0, 0)),
        ],
        out_specs=pl.BlockSpec((1, tm, D_MODEL), lambda b, t: (b, t, 0)),
        out_shape=jax.ShapeDtypeStruct(xs.shape, F32),
        scratch_shapes=[pltpu.VMEM((tm, D_FF), BF16)],
        compiler_params=pltpu.CompilerParams(
            dimension_semantics=("arbitrary", "arbitrary"), vmem_limit_bytes=V7X_VMEM_LIMIT),
        name=f"ffn_l{layer}_s{sub}",
    )(xs, mod, mod, mod, w_in, w_out, ln_g, ln_b)


def _inproj_kernel(x_ref, sh_ref, sc_ref, w_ref, cs_ref, cos_ref, sin_ref,
                   na_ref, ml_ref, gla_ref, sm_ref, da_ref, *, n_batch, tm):
    b = pl.program_id(0)
    t = pl.program_id(1)
    is_ctx = _is_ctx_rows(t, tm)
    x = x_ref[0]
    h = (x * (1.0 + _mod_rows(sc_ref, b, n_batch, is_ctx)) + _mod_rows(sh_ref, b, n_batch, is_ctx)).astype(BF16)
    na_ref[0] = (_dot(h, w_ref[:, C_NA:C_ML]) * cs_ref[:, C_NA:C_ML]).astype(BF16)
    ml_ref[0] = _dot(h, w_ref[:, C_ML:C_GLA]) * cs_ref[:, C_ML:C_GLA]
    gla_ref[0] = _dot(h, w_ref[:, C_GLA:C_SMALL]) * cs_ref[:, C_GLA:C_SMALL]
    sm_ref[0] = _dot(h, w_ref[:, C_SMALL:C_DA])
    pd = _dot(h, w_ref[:, C_DA:N_PROJ])
    cos = cos_ref[...]
    sin = sin_ref[...]
    w = BRANCH_W
    da_ref[0, :, 0:w] = ((pd[:, 0:w] * cos + pd[:, w:2 * w] * sin) * (DA_DQK ** -0.5 * LOG2E)).astype(BF16)
    da_ref[0, :, w:2 * w] = (pd[:, 2 * w:3 * w] * cos + pd[:, 3 * w:4 * w] * sin).astype(BF16)
    da_ref[0, :, 2 * w:3 * w] = pd[:, 4 * w:5 * w].astype(BF16)


def _inproj(xs, mod, w_all, colscale, cos_tab, sin_tab, *, layer, tm):
    n_batch = xs.shape[0]
    rows = mod.shape[1]
    kern = functools.partial(_inproj_kernel, n_batch=n_batch, tm=tm)

    def out(width, dtype):
        return (pl.BlockSpec((1, tm, width), lambda b, t: (b, t, 0)),
                jax.ShapeDtypeStruct((n_batch, NTOK, width), dtype))

    outs = [out(768, BF16), out(1024, F32), out(768, F32), out(128, F32), out(768, BF16)]
    return pl.pallas_call(
        kern,
        grid=(n_batch, NTOK // tm),
        in_specs=[
            pl.BlockSpec((1, tm, D_MODEL), lambda b, t: (b, t, 0)),
            _mod_spec(rows, layer, 3), _mod_spec(rows, layer, 4),
            _const_spec((D_MODEL, N_PROJ), lambda b, t: (0, 0)),
            pl.BlockSpec((1, C_SMALL), lambda b, t: (0, 0)),
            pl.BlockSpec((tm, BRANCH_W), lambda b, t: (t, 0)),
            pl.BlockSpec((tm, BRANCH_W), lambda b, t: (t, 0)),
        ],
        out_specs=[o[0] for o in outs],
        out_shape=[o[1] for o in outs],
        compiler_params=pltpu.CompilerParams(
            dimension_semantics=("arbitrary", "arbitrary"), vmem_limit_bytes=V7X_VMEM_LIMIT),
        name=f"inproj_l{layer}",
    )(xs, mod, mod, w_all, colscale, cos_tab, sin_tab)


def _softmax_pv(parts, vals):
    m = None
    for s in parts:
        mi = jnp.max(s, axis=-1, keepdims=True)
        m = mi if m is None else jnp.maximum(m, mi)
    l = None
    o = None
    for s, v in zip(parts, vals):
        p = jnp.exp2(s - m)
        li = jnp.sum(p, axis=-1, keepdims=True)
        oi = _dot(p.astype(BF16), v)
        l = li if l is None else l + li
        o = oi if o is None else o + oi
    return o / l


def _na_kernel(q_ref, k_ref, v_ref, bias_ref, o_ref, *, with_ctx):
    i = pl.program_id(0)
    qs = _stack_heads(q_ref[0])
    kc = k_ref[0, SEQ:NTOK, :]
    vc = v_ref[0, SEQ:NTOK, :]

    @pl.when(i < N_LAT_BLK)
    def _():
        start = pl.multiple_of(jnp.clip(i - 1, 0, N_LAT_BLK - NA_BAND // BLK) * BLK, BLK)
        kb = k_ref[0, pl.ds(start, NA_BAND), :]
        vb = v_ref[0, pl.ds(start, NA_BAND), :]
        s_loc = _dot_nt(qs, kb) + bias_ref[...].reshape(N_HEADS * BLK, NA_BAND)
        s_ctx = _dot_nt(qs, kc)
        o_ref[0] = _unstack_heads(_softmax_pv([s_loc, s_ctx], [vb, vc]), BLK).astype(BF16)

    if with_ctx:
        @pl.when(i == N_LAT_BLK)
        def _():
            o_ref[0] = _unstack_heads(_softmax_pv([_dot_nt(qs, kc)], [vc]), BLK).astype(BF16)


def _na_pattern(i):
    return jnp.minimum(i, 1) + (i >= N_LAT_BLK - 2).astype(jnp.int32) + (i >= N_LAT_BLK - 1).astype(jnp.int32)


def _neighbourhood(na, bias, *, layer, with_ctx):
    n_batch = na.shape[0]
    n_blk = N_BLK if with_ctx else N_LAT_BLK
    return pl.pallas_call(
        functools.partial(_na_kernel, with_ctx=with_ctx),
        grid=(n_blk, n_batch),
        in_specs=[
            pl.BlockSpec((1, BLK, BRANCH_W), lambda i, b: (b, i, 0)),
            pl.BlockSpec((1, NTOK, BRANCH_W), lambda i, b: (b, 0, 1)),
            pl.BlockSpec((1, NTOK, BRANCH_W), lambda i, b: (b, 0, 2)),
            pl.BlockSpec((N_HEADS, None, BLK, NA_BAND),
                         lambda i, b: (0, _na_pattern(jnp.minimum(i, N_LAT_BLK - 1)), 0, 0)),
        ],
        out_specs=pl.BlockSpec((1, BLK, BRANCH_W), lambda i, b: (b, i, 0)),
        out_shape=jax.ShapeDtypeStruct((n_batch, n_blk * BLK, BRANCH_W), BF16),
        compiler_params=pltpu.CompilerParams(
            dimension_semantics=("arbitrary", "arbitrary"), vmem_limit_bytes=V7X_VMEM_LIMIT),
        name=f"na_l{layer}",
    )(na, na, na, bias)


def _na_bias_tables(rpb):
    blocks = np.array([0, 1, N_LAT_BLK - 2, N_LAT_BLK - 1])
    qr = blocks[:, None] * NA_QROWS + np.arange(NA_QROWS)[None, :]
    r0 = np.clip(qr - NA_KH // 2, 0, GRID_ROWS - NA_KH)
    band0 = np.clip(blocks - 1, 0, N_LAT_BLK - NA_BAND // BLK) * NA_QROWS
    kr = band0[:, None] + np.arange(NA_BAND_ROWS)[None, :]
    row_ok = (kr[:, None, :] >= r0[:, :, None]) & (kr[:, None, :] < r0[:, :, None] + NA_KH)
    dr = np.clip(kr[:, None, :] - qr[:, :, None] + NA_KH - 1, 0, 2 * NA_KH - 2)
    col = np.arange(GRID_W)
    c0 = np.clip(col - NA_KW // 2, 0, GRID_W - NA_KW)
    col_ok = (col[None, :] >= c0[:, None]) & (col[None, :] < c0[:, None] + NA_KW)
    dc = np.clip(col[None, :] - col[:, None] + NA_KW - 1, 0, 2 * NA_KW - 2)
    ok = row_ok[:, :, None, :, None] & col_ok[None, None, :, None, :]
    oh_c = jnp.asarray(dc[:, :, None] == np.arange(2 * NA_KW - 1), dtype=F32)
    oh_r = jnp.asarray(dr[..., None] == np.arange(2 * NA_KH - 1), dtype=F32)
    by_col = jnp.einsum("hrc,xyc->hrxy", rpb, oh_c, precision=lax.Precision.HIGHEST)
    vals = jnp.einsum("pqkr,hrxy->hpqxky", oh_r, by_col, precision=lax.Precision.HIGHEST)
    tab = jnp.where(ok[None], vals * LOG2E, NEG)
    return tab.reshape(N_HEADS, 4, BLK, NA_BAND)


def _da_kernel(q_ref, k_ref, v_ref, lam_ref, o_ref, *, lambda_init, with_ctx):
    i = pl.program_id(1)
    lp = lam_ref[...]
    lam = (jnp.exp(jnp.sum(lp[0:1] * lp[1:2], axis=-1, keepdims=True))
           - jnp.exp(jnp.sum(lp[2:3] * lp[3:4], axis=-1, keepdims=True)) + lambda_init)
    q = q_ref[0]
    lane = lax.broadcasted_iota(jnp.int32, (1, BRANCH_W), 1)

    def attend(k, v):
        out = jnp.zeros((BLK, BRANCH_W), F32)
        for h in range(N_HEADS):
            lo = h * 2 * DA_DQK
            q1 = jnp.where((lane >= lo) & (lane < lo + DA_DQK), q, jnp.zeros_like(q))
            q2 = jnp.where((lane >= lo + DA_DQK) & (lane < lo + 2 * DA_DQK), q, jnp.zeros_like(q))
            s = _dot_nt(jnp.concatenate([q1, q2], axis=0), k)
            p = jnp.exp2(s - jnp.max(s, axis=-1, keepdims=True))
            r = 1.0 / jnp.sum(p, axis=-1, keepdims=True)
            pb = p.astype(BF16)
            a = pb[:BLK] * r[:BLK].astype(BF16) - pb[BLK:] * (lam * r[BLK:]).astype(BF16)
            out = jnp.where(_head_mask(BRANCH_W, h), _dot(a, v), out)
        return out

    @pl.when(i < N_LAT_BLK)
    def _():
        o_ref[0] = attend(k_ref[0], v_ref[0])

    if with_ctx:
        @pl.when(i == N_LAT_BLK)
        def _():
            o_ref[0] = attend(k_ref[0, SEQ:NTOK, :], v_ref[0, SEQ:NTOK, :])


def _diff_attention(da, lam_p, *, layer, lambda_init, with_ctx):
    n_batch = da.shape[0]
    n_blk = N_BLK if with_ctx else N_LAT_BLK
    return pl.pallas_call(
        functools.partial(_da_kernel, lambda_init=lambda_init, with_ctx=with_ctx),
        grid=(n_batch, n_blk),
        in_specs=[
            pl.BlockSpec((1, BLK, BRANCH_W), lambda b, i: (b, i, 0)),
            pl.BlockSpec((1, NTOK, BRANCH_W), lambda b, i: (b, 0, 1)),
            pl.BlockSpec((1, NTOK, BRANCH_W), lambda b, i: (b, 0, 2)),
            pl.BlockSpec((None, 4, DA_DQK), lambda b, i: (layer, 0, 0)),
        ],
        out_specs=pl.BlockSpec((1, BLK, BRANCH_W), lambda b, i: (b, i, 0)),
        out_shape=jax.ShapeDtypeStruct((n_batch, n_blk * BLK, BRANCH_W), F32),
        compiler_params=pltpu.CompilerParams(
            dimension_semantics=("arbitrary", "arbitrary"), vmem_limit_bytes=V7X_VMEM_LIMIT),
        name=f"da_l{layer}",
    )(da, da, da, lam_p)


def _fwd_chunk(g):
    return jnp.where(g == 0, N_LAT_BLK, g - 1)


def _bwd_chunk(g):
    return jnp.where(g == 0, N_LAT_BLK, N_LAT_BLK - g)


def _tri_mask(reverse):
    r = lax.broadcasted_iota(jnp.int32, (BLK, BLK), 0)
    c = lax.broadcasted_iota(jnp.int32, (BLK, BLK), 1)
    return (c >= r) if reverse else (c <= r)


def _cumsum_rows(x, mask):
    tri = jnp.where(mask, 1.0, 0.0).astype(BF16)
    return _dot_split(tri, x, 1, 3)


def _mlstm_chunk(e, d, q_ref, k_ref, v_ref, s_ref, gb_ref, o_ref, c_scr, n_scr, m_scr):
    reverse = d == 1
    mask = _tri_mask(reverse)
    end = 0 if reverse else BLK - 1
    st = 2 * e + d
    q = q_ref[e]
    k = k_ref[e]
    v = v_ref[e]
    qb = q.astype(BF16)
    kb = k.astype(BF16)
    vb = v.astype(BF16)
    gs = s_ref[e] + gb_ref[...]
    lane = lax.broadcasted_iota(jnp.int32, (1, 128), 1)
    f_lo = 2 * N_HEADS * d + ML_F_OFF
    is_f = (lane >= f_lo) & (lane < f_lo + N_HEADS)
    gl = jnp.where(is_f, jax.nn.log_sigmoid(gs), gs)
    bc = _cumsum_rows(gl, mask)
    gl_t = gl.T
    bc_t = bc.T
    m_row = m_scr[st, 0:1, :]
    s_all = _dot_nt(_stack_heads(qb), kb)
    q_c = _dot(qb, c_scr[st].astype(BF16))
    q_n = _dot(qb, n_scr[st].astype(BF16))
    qk_parts, w_inters, m_ts, dens = [], [], [], []
    wk_l = jnp.zeros((BLK, BRANCH_W), F32)
    decay_l = jnp.zeros((1, BRANCH_W), F32)
    decay_n = jnp.zeros((1, 128), F32)
    m_new_row = m_row
    for h in range(N_HEADS):
        ci = 2 * N_HEADS * d + h
        cf = f_lo + h
        b_col = bc[:, cf:cf + 1]
        b_row = bc_t[cf:cf + 1, :]
        i_row = gl_t[ci:ci + 1, :]
        i_col = gl[:, ci:ci + 1]
        m_prev = m_row[:, h:h + 1]
        d_log = jnp.where(mask, b_col + (i_row - b_row), NEG)
        inter = b_col + m_prev
        m_t = jnp.maximum(inter, jnp.max(d_log, axis=-1, keepdims=True))
        w = jnp.exp(d_log - m_t)
        w_inter = jnp.exp(inter - m_t)
        qk = s_all[h * BLK:(h + 1) * BLK] * w
        dens.append(jnp.sum(qk, axis=-1, keepdims=True) + w_inter * q_n[:, h:h + 1])
        qk_parts.append(qk.astype(BF16))
        w_inters.append(w_inter)
        m_ts.append(m_t)
        b_end = bc[end:end + 1, cf:cf + 1]
        k_log = b_end - b_col + i_col
        m_new = jnp.maximum(b_end + m_prev, jnp.max(k_log, axis=0, keepdims=True))
        wk = jnp.exp(k_log - m_new)
        decay = jnp.exp(b_end + m_prev - m_new)
        hm = _head_mask(BRANCH_W, h)
        wk_l = jnp.where(hm, wk, wk_l)
        decay_l = jnp.where(hm, decay, decay_l)
        decay_n = jnp.where(lane == h, decay, decay_n)
        m_new_row = jnp.where(lane == h, m_new, m_new_row)
    num_all = _dot(jnp.concatenate(qk_parts, axis=0), vb)
    out = jnp.zeros((BLK, BRANCH_W), F32)
    for h in range(N_HEADS):
        num = num_all[h * BLK:(h + 1) * BLK] + w_inters[h] * q_c
        hv = num / jnp.maximum(jnp.abs(dens[h]), jnp.exp(-m_ts[h]))
        out = jnp.where(_head_mask(BRANCH_W, h), hv, out)
    o_ref[e] = out
    kw = (k * wk_l).astype(BF16)
    r_head = lax.broadcasted_iota(jnp.int32, (BRANCH_W, BRANCH_W), 0) // (BRANCH_W // N_HEADS)
    c_head = lax.broadcasted_iota(jnp.int32, (BRANCH_W, BRANCH_W), 1) // (BRANCH_W // N_HEADS)
    c_scr[st] = decay_l * c_scr[st] + jnp.where(r_head == c_head, _dot_tn(kw, vb), 0.0)
    n_sum = _dot_tn(kw, jnp.ones((BLK, 128), BF16))
    n_sel = (lax.broadcasted_iota(jnp.int32, (BRANCH_W, 128), 0) // (BRANCH_W // N_HEADS)
             == lax.broadcasted_iota(jnp.int32, (BRANCH_W, 128), 1))
    n_scr[st] = decay_n * n_scr[st] + jnp.where(n_sel, n_sum, 0.0)
    m_scr[st, 0:1, :] = m_new_row


def _mlstm_kernel(qf, kf, vf, sf, qb, kb, vb, sb, gb_ref, o_ref_f, o_ref_b, c_scr, n_scr, m_scr, *, group):
    @pl.when(pl.program_id(1) == 0)
    def _():
        c_scr[...] = jnp.zeros_like(c_scr)
        n_scr[...] = jnp.zeros_like(n_scr)
        m_scr[...] = jnp.zeros_like(m_scr)

    for e in range(group):
        _mlstm_chunk(e, 0, qf, kf, vf, sf, gb_ref, o_ref_f, c_scr, n_scr, m_scr)
        _mlstm_chunk(e, 1, qb, kb, vb, sb, gb_ref, o_ref_b, c_scr, n_scr, m_scr)


def _scan_group(n_batch):
    del n_batch
    return 1


def _scan_specs(group, width_blocks, chunk_fn):
    return [pl.BlockSpec((group, BLK, BRANCH_W), lambda b, g, j=j: (b, chunk_fn(g), j)) for j in width_blocks]


def _mlstm(ml, small, gate_b, *, layer):
    n_batch = ml.shape[0]
    group = _scan_group(n_batch)
    small_spec = lambda fn: pl.BlockSpec((group, BLK, 128), lambda b, g: (b, fn(g), 0))
    out_spec = lambda fn: pl.BlockSpec((group, BLK, BRANCH_W), lambda b, g: (b, fn(g), 0))
    return pl.pallas_call(
        functools.partial(_mlstm_kernel, group=group),
        grid=(n_batch // group, N_BLK),
        in_specs=(_scan_specs(group, (0, 1, 2), _fwd_chunk) + [small_spec(_fwd_chunk)]
                  + _scan_specs(group, (0, 1, 2), _bwd_chunk) + [small_spec(_bwd_chunk)]
                  + [pl.BlockSpec((1, 128), lambda b, g: (0, 0))]),
        out_specs=[out_spec(_fwd_chunk), out_spec(_bwd_chunk)],
        out_shape=[jax.ShapeDtypeStruct((n_batch, NTOK, BRANCH_W), F32)] * 2,
        scratch_shapes=[pltpu.VMEM((2 * group, BRANCH_W, BRANCH_W), F32),
                        pltpu.VMEM((2 * group, BRANCH_W, 128), F32),
                        pltpu.VMEM((2 * group, 8, 128), F32)],
        compiler_params=pltpu.CompilerParams(
            dimension_semantics=("arbitrary", "arbitrary"), vmem_limit_bytes=V7X_VMEM_LIMIT),
        name=f"mlstm_l{layer}",
    )(ml, ml, ml, small, ml, ml, ml, small, gate_b)


def _gla_chunk(e, d, qk_ref, v_ref, s_ref, w2_ref, ba_ref, o_ref, st_scr):
    reverse = d == 1
    mask = _tri_mask(reverse)
    end = 0 if reverse else BLK - 1
    half = BLK // 2
    slot = 2 * e + d
    q = qk_ref[e, :, 0:128]
    k = qk_ref[e, :, 128:256]
    vb = v_ref[e].astype(BF16)
    x = _dot_split(s_ref[e], w2_ref[d], 3, 3) + ba_ref[d]
    la = jax.nn.log_sigmoid(x) * (1.0 / GLA_TAU)
    bc = _cumsum_rows(la, mask)
    ref_row = bc[half:half + 1, :]
    qe = (q * jnp.exp(bc - ref_row)).astype(BF16)
    ke = (k * jnp.exp(ref_row - bc)).astype(BF16)
    att = _dot_nt(_stack_heads(qe), ke)
    mask4 = jnp.concatenate([mask] * N_HEADS, axis=0)
    o_all = _dot(jnp.where(mask4, att, 0.0).astype(BF16), vb)
    st = st_scr[slot]
    inter = _dot_nt((q * jnp.exp(bc)).astype(BF16), st.astype(BF16))
    o_ref[e] = _unstack_heads(o_all, BLK) + inter
    b_end = bc[end:end + 1, :]
    kend = (k * jnp.exp(b_end - bc)).astype(BF16)
    r_head = lax.broadcasted_iota(jnp.int32, (BRANCH_W, 128), 0) // (BRANCH_W // N_HEADS)
    c_head = lax.broadcasted_iota(jnp.int32, (BRANCH_W, 128), 1) // GLA_DK
    st_scr[slot] = jnp.exp(b_end) * st + jnp.where(r_head == c_head, _dot_tn(vb, kend), 0.0)


def _gla_kernel(qkf, vf, sf, qkb, vb, sb, w2_ref, ba_ref, o_ref_f, o_ref_b, st_scr, *, group):
    @pl.when(pl.program_id(1) == 0)
    def _():
        st_scr[...] = jnp.zeros_like(st_scr)

    for e in range(group):
        _gla_chunk(e, 0, qkf, vf, sf, w2_ref, ba_ref, o_ref_f, st_scr)
        _gla_chunk(e, 1, qkb, vb, sb, w2_ref, ba_ref, o_ref_b, st_scr)


def _gla(gla, small, w2pad, ba, *, layer):
    n_batch = gla.shape[0]
    group = _scan_group(n_batch)
    small_spec = lambda fn: pl.BlockSpec((group, BLK, 128), lambda b, g: (b, fn(g), 0))
    out_spec = lambda fn: pl.BlockSpec((group, BLK, BRANCH_W), lambda b, g: (b, fn(g), 0))
    return pl.pallas_call(
        functools.partial(_gla_kernel, group=group),
        grid=(n_batch // group, N_BLK),
        in_specs=(_scan_specs(group, (0, 1), _fwd_chunk) + [small_spec(_fwd_chunk)]
                  + _scan_specs(group, (0, 1), _bwd_chunk) + [small_spec(_bwd_chunk)]
                  + [pl.BlockSpec((2, 128, 128), lambda b, g: (0, 0, 0)),
                     pl.BlockSpec((2, 1, 128), lambda b, g: (0, 0, 0))]),
        out_specs=[out_spec(_fwd_chunk), out_spec(_bwd_chunk)],
        out_shape=[jax.ShapeDtypeStruct((n_batch, NTOK, BRANCH_W), F32)] * 2,
        scratch_shapes=[pltpu.VMEM((2 * group, BRANCH_W, 128), F32)],
        compiler_params=pltpu.CompilerParams(
            dimension_semantics=("arbitrary", "arbitrary"), vmem_limit_bytes=V7X_VMEM_LIMIT),
        name=f"gla_l{layer}",
    )(gla, gla, small, gla, gla, small, w2pad, ba)


def _group_mean(x, gmat):
    return _dot_split(x, gmat, 2, 1)


def _merge_kernel(x_ref, sh_ref, sc_ref, gt_ref, na_ref, mlf_ref, mlb_ref, mlo_ref, glf_ref, glb_ref, glg_ref,
                  da_ref, ng_ref, gm_ref, wg_ref, wb_ref, wo_ref, lng_ref, lnb_ref, o_ref,
                  *, n_batch, tm, alpha, lambda_init):
    b = pl.program_id(0)
    t = pl.program_id(1)
    is_ctx = _is_ctx_rows(t, tm)
    x = x_ref[0]
    h = (x * (1.0 + _mod_rows(sc_ref, b, n_batch, is_ctx)) + _mod_rows(sh_ref, b, n_batch, is_ctx)).astype(BF16)
    gmat = gm_ref[...]

    def rms_norm(y, gain):
        return y * lax.rsqrt(_group_mean(y * y, gmat) + HEAD_EPS) * gain

    h_ml = mlf_ref[0] + mlb_ref[0]
    y_ml = jax.nn.sigmoid(mlo_ref[0]) * rms_norm(h_ml - _group_mean(h_ml, gmat), ng_ref[0:1, :])
    g_gl = glg_ref[0]
    y_gla = g_gl * jax.nn.sigmoid(g_gl) * rms_norm(glf_ref[0] + glb_ref[0], ng_ref[1:2, :])
    y_da = (1.0 - lambda_init) * rms_norm(da_ref[0], ng_ref[2:3, :])
    ys = [na_ref[0], y_ml.astype(BF16), y_gla.astype(BF16), y_da.astype(BF16)]
    acc = None
    for j in range(N_BRANCH):
        gate = jax.nn.sigmoid(_dot(h, wg_ref[:, j * D_MODEL:(j + 1) * D_MODEL]))
        term = gate * _dot(ys[j], wb_ref[j])
        acc = term if acc is None else acc + term
    y = _dot(acc.astype(BF16), wo_ref[...])
    y = alpha * x + _mod_rows(gt_ref, b, n_batch, is_ctx) * y
    o_ref[0] = _layer_norm(y, lng_ref[0], lnb_ref[0])


def _merge(xs, mod, na_o, ml_o, ml, gla_o, gla, da_o, norm_g, gmat, w_gates, w_branch, w_out, ln_g, ln_b,
           *, layer, n_rows, tm, alpha, lambda_init):
    n_batch = xs.shape[0]
    rows = mod.shape[1]
    ln_idx = layer * 3 + 1
    kern = functools.partial(_merge_kernel, n_batch=n_batch, tm=tm, alpha=alpha, lambda_init=lambda_init)
    tok = lambda j: pl.BlockSpec((1, tm, BRANCH_W), lambda b, t: (b, t, j))
    return pl.pallas_call(
        kern,
        grid=(n_batch, n_rows // tm),
        in_specs=[
            pl.BlockSpec((1, tm, D_MODEL), lambda b, t: (b, t, 0)),
            _mod_spec(rows, layer, 3), _mod_spec(rows, layer, 4), _mod_spec(rows, layer, 5),
            tok(0), tok(0), tok(0), tok(3), tok(0), tok(0), tok(2), tok(0),
            pl.BlockSpec((8, BRANCH_W), lambda b, t: (0, 0)),
            pl.BlockSpec((BRANCH_W, BRANCH_W), lambda b, t: (0, 0)),
            _const_spec((D_MODEL, N_BRANCH * D_MODEL), lambda b, t: (0, 0)),
            _const_spec((None, N_BRANCH, BRANCH_W, D_MODEL), lambda b, t: (layer, 0, 0, 0)),
            _const_spec((None, D_MODEL, D_MODEL), lambda b, t: (layer, 0, 0)),
            pl.BlockSpec((1, 1, D_MODEL), lambda b, t: (ln_idx, 0, 0)),
            pl.BlockSpec((1, 1, D_MODEL), lambda b, t: (ln_idx, 0, 0)),
        ],
        out_specs=pl.BlockSpec((1, tm, D_MODEL), lambda b, t: (b, t, 0)),
        out_shape=jax.ShapeDtypeStruct((n_batch, n_rows, D_MODEL), F32),
        compiler_params=pltpu.CompilerParams(
            dimension_semantics=("arbitrary", "arbitrary"), vmem_limit_bytes=V7X_VMEM_LIMIT),
        name=f"merge_l{layer}",
    )(xs, mod, mod, mod, na_o, ml_o[0], ml_o[1], ml, gla_o[0], gla_o[1], gla, da_o, norm_g, gmat,
      w_gates, w_branch, w_out, ln_g, ln_b)


def _rope_tables():
    t = jnp.arange(SEQ)
    n_f = DA_DQK // 4
    inv = ROPE_BASE ** (-jnp.arange(n_f, dtype=F32) / n_f)

    def cs(pos):
        ang = pos.astype(F32)[:, None] * inv
        return jnp.cos(ang), jnp.sin(ang)

    (cr, sr), (cc, sc) = cs(t // GRID_W), cs(t % GRID_W)
    cos32 = jnp.concatenate([cr, cr, cc, cc], axis=-1)
    sin32 = jnp.concatenate([-sr, sr, -sc, sc], axis=-1)
    reps = BRANCH_W // DA_DQK
    cos = jnp.concatenate([jnp.tile(cos32, (1, reps)), jnp.ones((CTX_LEN, BRANCH_W), F32)], axis=0)
    sin = jnp.concatenate([jnp.tile(sin32, (1, reps)), jnp.zeros((CTX_LEN, BRANCH_W), F32)], axis=0)
    return cos, sin


def _swap_perm():
    n_f = DA_DQK // 4
    base = np.concatenate([np.arange(n_f, 2 * n_f), np.arange(0, n_f),
                           np.arange(3 * n_f, 4 * n_f), np.arange(2 * n_f, 3 * n_f)])
    return np.concatenate([g * DA_DQK + base for g in range(BRANCH_W // DA_DQK)])


def _prep_mix_weights(w):
    na, ml, ml_if = w[:, 0:768], w[:, 768:1792], w[:, 1792:1808]
    gla, gla_a = w[:, 1808:2576], w[:, 2576:2608]
    dq, dk, dv = w[:, 2608:2864], w[:, 2864:3120], w[:, 3120:3376]
    gates = w[:, 3376:]
    perm = _swap_perm()
    pad = jnp.zeros((D_MODEL, 128 - ml_if.shape[1] - gla_a.shape[1]), w.dtype)
    w_all = jnp.concatenate([na, ml, gla, ml_if, gla_a, pad, dq, dq[:, perm], dk, dk[:, perm], dv], axis=1)
    return w_all.astype(BF16), gates.astype(BF16)


def _col_scale():
    cs = np.ones((1, C_SMALL), np.float32)
    cs[:, C_NA:C_NA + 256] = 64 ** -0.5 * LOG2E
    cs[:, C_ML:C_ML + 256] = 64 ** -0.5
    cs[:, C_GLA:C_GLA + 128] = GLA_DK ** -0.5
    return jnp.asarray(cs)


def _group_mean_matrix():
    head = np.arange(BRANCH_W) // (BRANCH_W // N_HEADS)
    return jnp.asarray((head[:, None] == head[None, :]) / (BRANCH_W // N_HEADS), dtype=BF16)


def kernel(x, c, ctx, c_ctx, w_ada, b_ada, ln_g, ln_b, ffn_w_in, ffn_w_out, w_mix_in, na_rpb, ml_gate_b,
           ml_norm_g, gla_w_a2, gla_b_a, gla_norm_g, da_lambda, da_norm_g, w_branch, w_out):
    n_batch = x.shape[0]
    depth = w_ada.shape[0]
    alpha = (2 * depth) ** 0.25
    mod_rows = -(-(n_batch + 1) // 8) * 8
    cc = jnp.concatenate([c, c_ctx[None, :], jnp.zeros((mod_rows - n_batch - 1, D_MODEL), F32)], axis=0)
    mod = _ada(cc, w_ada, b_ada)

    xs = jnp.concatenate([x, ctx], axis=1)
    w_in_bf = ffn_w_in.astype(BF16)
    w_out_bf = ffn_w_out.astype(BF16)
    w_branch_bf = w_branch.astype(BF16)
    w_o_bf = w_out.astype(BF16)
    ln_g3 = ln_g.reshape(depth * 3, 1, D_MODEL)
    ln_b3 = ln_b.reshape(depth * 3, 1, D_MODEL)
    cos_tab, sin_tab = _rope_tables()
    colscale = _col_scale()
    gmat = _group_mean_matrix()

    for l in range(depth):
        ctx_out = l < depth - 1
        lambda_init = 0.8 - 0.6 * math.exp(-0.3 * l)
        w_all, w_gates = _prep_mix_weights(w_mix_in[l])
        bias = _na_bias_tables(na_rpb[l])
        gate_b = jnp.concatenate([ml_gate_b[l], jnp.zeros((128 - ml_gate_b.shape[1],), F32)])[None, :]
        w2pad = jnp.zeros((2, 128, 128), F32)
        for d in range(2):
            lo = SMALL_A_OFF + d * GLA_RANK
            w2pad = w2pad.at[d, lo:lo + GLA_RANK, :].set(gla_w_a2[l, d])
        ba = gla_b_a[l][:, None, :]
        norm_g = jnp.concatenate([ml_norm_g[l][None], gla_norm_g[l][None], da_norm_g[l][None],
                                  jnp.zeros((5, BRANCH_W), F32)], axis=0)

        xs = _ffn(xs, mod, w_in_bf, w_out_bf, ln_g3, ln_b3, layer=l, sub=0, mod0=0,
                  n_rows=NTOK, tm=768, alpha=alpha)
        na, ml, gla, small, da = _inproj(xs, mod, w_all, colscale, cos_tab, sin_tab, layer=l, tm=768)
        na_o = _neighbourhood(na, bias, layer=l, with_ctx=ctx_out)
        da_o = _diff_attention(da, da_lambda, layer=l, lambda_init=lambda_init, with_ctx=ctx_out)
        ml_o = _mlstm(ml, small, gate_b, layer=l)
        gla_o = _gla(gla, small, w2pad, ba, layer=l)
        n_rows, tm = (NTOK, 768) if ctx_out else (SEQ, 512)
        xs = _merge(xs, mod, na_o, ml_o, ml, gla_o, gla, da_o, norm_g, gmat, w_gates, w_branch_bf, w_o_bf,
                    ln_g3, ln_b3, layer=l, n_rows=n_rows, tm=tm, alpha=alpha, lambda_init=lambda_init)
        xs = _ffn(xs, mod, w_in_bf, w_out_bf, ln_g3, ln_b3, layer=l, sub=1, mod0=6,
                  n_rows=n_rows, tm=tm, alpha=alpha)
    return xs
```

```python
import functools
import math

import jax
import jax.numpy as jnp
import numpy as np
from jax import lax
from jax.experimental import pallas as pl
from jax.experimental.pallas import tpu as pltpu

F32 = jnp.float32
BF16 = jnp.bfloat16

D_MODEL = 1024
SEQ = 2048
CTX_LEN = 256
NTOK = SEQ + CTX_LEN
GRID_W = 64
GRID_ROWS = SEQ // GRID_W
N_MOD = 9
D_FF = 2816
FFN_HALF = 0.5
N_BRANCH = 4
BRANCH_W = 256
N_HEADS = 4
NA_KH = 8
NA_KW = 16
ML_F_OFF = 4
GLA_DK = 32
GLA_RANK = 16
GLA_TAU = 16.0
DA_DQK = 32
ROPE_BASE = 10000.0
LN_EPS = 1e-5
HEAD_EPS = 1e-6
NEG = -1e30
LOG2E = math.log2(math.e)
DA_SAFE_LOG2 = 100.0
DA_BOUND_SLACK = 1.1

V7X_VMEM_LIMIT = 56 * 1024 * 1024
FF_CHUNK = 256
BLK = 256
N_BLK = NTOK // BLK
N_LAT_BLK = SEQ // BLK
NA_BAND_ROWS = 12
NA_BAND = NA_BAND_ROWS * GRID_W
NA_QROWS = BLK // GRID_W

C_NA = 0
C_ML = 768
C_GLA = 1792
C_SMALL = 2560
C_DA = 2688
N_PROJ = 3968
SMALL_A_OFF = 16


def _dot(a, b):
    return jnp.dot(a, b, preferred_element_type=F32)


def _dot_nt(a, b):
    return lax.dot_general(a, b, (((1,), (1,)), ((), ())), preferred_element_type=F32)


def _dot_tn(a, b):
    return lax.dot_general(a, b, (((0,), (0,)), ((), ())), preferred_element_type=F32)


def _split_f32(x, n):
    parts = []
    r = x
    for i in range(n):
        p = r.astype(BF16).astype(F32)
        parts.append(p)
        if i + 1 < n:
            r = r - p
    return parts


def _split(x, n):
    return [p.astype(BF16) for p in _split_f32(x, n)]


def _dot_split(a, b, na, nb, dot=_dot):
    pa = _split(a, na) if na > 1 else [a.astype(BF16)]
    pb = _split(b, nb) if nb > 1 else [b.astype(BF16)]
    acc = None
    for i, x in enumerate(pa):
        for j, y in enumerate(pb):
            if i + j < max(na, nb):
                t = dot(x, y)
                acc = t if acc is None else acc + t
    return acc


def _layer_norm(y, g, b):
    mu = jnp.mean(y, axis=-1, keepdims=True)
    yc = y - mu
    var = jnp.mean(yc * yc, axis=-1, keepdims=True)
    return yc * lax.rsqrt(var + LN_EPS) * g + b


def _mod_rows(ref, b, n_batch, is_ctx):
    lat = ref[0, pl.ds(b, 1), :]
    ctx = ref[0, n_batch:n_batch + 1, :]
    return jnp.where(is_ctx, ctx, lat)


def _is_ctx_rows(t, tm):
    return (t * tm + lax.broadcasted_iota(jnp.int32, (tm, 1), 0)) >= SEQ


def _head_mask(width, head, n=1):
    lane = lax.broadcasted_iota(jnp.int32, (n, width), 1)
    per = width // N_HEADS
    return (lane >= head * per) & (lane < (head + 1) * per)


def _stack_heads(x):
    w = x.shape[-1]
    return jnp.concatenate([jnp.where(_head_mask(w, h), x, jnp.zeros_like(x)) for h in range(N_HEADS)], axis=0)


def _unstack_heads(x, m):
    w = x.shape[-1]
    out = jnp.zeros((m, w), x.dtype)
    for h in range(N_HEADS):
        out = jnp.where(_head_mask(w, h), x[h * m:(h + 1) * m], out)
    return out


def _ada_kernel(c_ref, w_ref, b_ref, o_ref):
    c = c_ref[...]
    s = c * jax.nn.sigmoid(c)
    o_ref[0] = _dot_split(s, w_ref[0], 2, 2) + b_ref[0]


def _ada(cc, w_ada, b_ada):
    depth = w_ada.shape[0]
    rows = cc.shape[0]
    return pl.pallas_call(
        _ada_kernel,
        grid=(depth, N_MOD),
        in_specs=[
            pl.BlockSpec((rows, D_MODEL), lambda l, k: (0, 0)),
            pl.BlockSpec((1, D_MODEL, D_MODEL), lambda l, k: (l, 0, k)),
            pl.BlockSpec((1, 1, D_MODEL), lambda l, k: (l, 0, k)),
        ],
        out_specs=pl.BlockSpec((1, rows, D_MODEL), lambda l, k: (l, 0, k)),
        out_shape=jax.ShapeDtypeStruct((depth, rows, N_MOD * D_MODEL), F32),
        compiler_params=pltpu.CompilerParams(
            dimension_semantics=("arbitrary", "arbitrary"), vmem_limit_bytes=V7X_VMEM_LIMIT),
        name="ada_mod",
    )(cc, w_ada, b_ada.reshape(depth, 1, N_MOD * D_MODEL))


def _ffn_kernel(x_ref, sh_ref, sc_ref, gt_ref, w_in_ref, w_out_ref, lng_ref, lnb_ref, o_ref, g_scr,
                *, n_batch, tm, alpha):
    b = pl.program_id(0)
    t = pl.program_id(1)
    is_ctx = _is_ctx_rows(t, tm)
    x = x_ref[0]
    h = (x * (1.0 + _mod_rows(sc_ref, b, n_batch, is_ctx)) + _mod_rows(sh_ref, b, n_batch, is_ctx)).astype(BF16)
    for j in range(D_FF // FF_CHUNK):
        lo = j * FF_CHUNK
        a = _dot(h, w_in_ref[:, lo:lo + FF_CHUNK])
        v = _dot(h, w_in_ref[:, D_FF + lo:D_FF + lo + FF_CHUNK])
        g_scr[:, lo:lo + FF_CHUNK] = (a * jax.nn.sigmoid(a) * v).astype(BF16)
    y = _dot(g_scr[...], w_out_ref[...])
    y = alpha * x + FFN_HALF * _mod_rows(gt_ref, b, n_batch, is_ctx) * y
    o_ref[0] = _layer_norm(y, lng_ref[0], lnb_ref[0])


def _const_spec(shape, index_map):
    return pl.BlockSpec(shape, index_map, pipeline_mode=pl.Buffered(1))


def _mod_spec(rows, layer, k):
    return pl.BlockSpec((1, rows, D_MODEL), lambda b, t: (layer, 0, k))


def _ffn(xs, mod, w_in, w_out, ln_g, ln_b, *, layer, sub, mod0, n_rows, tm, alpha):
    n_batch = xs.shape[0]
    rows = mod.shape[1]
    ln_idx = layer * 3 + (0 if sub == 0 else 2)
    kern = functools.partial(_ffn_kernel, n_batch=n_batch, tm=tm, alpha=alpha)
    return pl.pallas_call(
        kern,
        grid=(n_batch, n_rows // tm),
        in_specs=[
            pl.BlockSpec((1, tm, D_MODEL), lambda b, t: (b, t, 0)),
            _mod_spec(rows, layer, mod0), _mod_spec(rows, layer, mod0 + 1), _mod_spec(rows, layer, mod0 + 2),
            _const_spec((None, None, D_MODEL, 2 * D_FF), lambda b, t: (layer, sub, 0, 0)),
            _const_spec((None, None, D_FF, D_MODEL), lambda b, t: (layer, sub, 0, 0)),
            pl.BlockSpec((1, 1, D_MODEL), lambda b, t: (ln_idx, 0, 0)),
            pl.BlockSpec((1, 1, D_MODEL), lambda b, t: (ln_idx, 0, 0)),
        ],
        out_specs=pl.BlockSpec((1, tm, D_MODEL), lambda b, t: (b, t, 0)),
        out_shape=jax.ShapeDtypeStruct(xs.shape, F32),
        scratch_shapes=[pltpu.VMEM((tm, D_FF), BF16)],
        compiler_params=pltpu.CompilerParams(
            dimension_semantics=("arbitrary", "arbitrary"), vmem_limit_bytes=V7X_VMEM_LIMIT),
        name=f"ffn_l{layer}_s{sub}",
    )(xs, mod, mod, mod, w_in, w_out, ln_g, ln_b)


def _inproj_kernel(x_ref, sh_ref, sc_ref, w_ref, cs_ref, cos_ref, sin_ref,
                   na_ref, ml_ref, gla_ref, sm_ref, da_ref, *, n_batch, tm):
    b = pl.program_id(0)
    t = pl.program_id(1)
    is_ctx = _is_ctx_rows(t, tm)
    x = x_ref[0]
    h = (x * (1.0 + _mod_rows(sc_ref, b, n_batch, is_ctx)) + _mod_rows(sh_ref, b, n_batch, is_ctx)).astype(BF16)
    na_ref[0] = (_dot(h, w_ref[:, C_NA:C_ML]) * cs_ref[:, C_NA:C_ML]).astype(BF16)
    ml_ref[0] = _dot(h, w_ref[:, C_ML:C_GLA]) * cs_ref[:, C_ML:C_GLA]
    gla_ref[0] = _dot(h, w_ref[:, C_GLA:C_SMALL]) * cs_ref[:, C_GLA:C_SMALL]
    sm_ref[0] = _dot(h, w_ref[:, C_SMALL:C_DA])
    pd = _dot(h, w_ref[:, C_DA:N_PROJ])
    cos = cos_ref[...]
    sin = sin_ref[...]
    w = BRANCH_W
    da_ref[0, :, 0:w] = ((pd[:, 0:w] * cos + pd[:, w:2 * w] * sin) * (DA_DQK ** -0.5 * LOG2E)).astype(BF16)
    da_ref[0, :, w:2 * w] = (pd[:, 2 * w:3 * w] * cos + pd[:, 3 * w:4 * w] * sin).astype(BF16)
    da_ref[0, :, 2 * w:3 * w] = pd[:, 4 * w:5 * w].astype(BF16)


def _inproj(xs, mod, w_all, colscale, cos_tab, sin_tab, *, layer, tm):
    n_batch = xs.shape[0]
    rows = mod.shape[1]
    kern = functools.partial(_inproj_kernel, n_batch=n_batch, tm=tm)

    def out(width, dtype):
        return (pl.BlockSpec((1, tm, width), lambda b, t: (b, t, 0)),
                jax.ShapeDtypeStruct((n_batch, NTOK, width), dtype))

    outs = [out(768, BF16), out(1024, F32), out(768, F32), out(128, F32), out(768, BF16)]
    return pl.pallas_call(
        kern,
        grid=(n_batch, NTOK // tm),
        in_specs=[
            pl.BlockSpec((1, tm, D_MODEL), lambda b, t: (b, t, 0)),
            _mod_spec(rows, layer, 3), _mod_spec(rows, layer, 4),
            _const_spec((D_MODEL, N_PROJ), lambda b, t: (0, 0)),
            pl.BlockSpec((1, C_SMALL), lambda b, t: (0, 0)),
            pl.BlockSpec((tm, BRANCH_W), lambda b, t: (t, 0)),
            pl.BlockSpec((tm, BRANCH_W), lambda b, t: (t, 0)),
        ],
        out_specs=[o[0] for o in outs],
        out_shape=[o[1] for o in outs],
        compiler_params=pltpu.CompilerParams(
            dimension_semantics=("arbitrary", "arbitrary"), vmem_limit_bytes=V7X_VMEM_LIMIT),
        name=f"inproj_l{layer}",
    )(xs, mod, mod, w_all, colscale, cos_tab, sin_tab)


def _softmax_pv(parts, vals):
    m = None
    for s in parts:
        mi = jnp.max(s, axis=-1, keepdims=True)
        m = mi if m is None else jnp.maximum(m, mi)
    l = None
    o = None
    for s, v in zip(parts, vals):
        p = jnp.exp2(s - m)
        li = jnp.sum(p, axis=-1, keepdims=True)
        oi = _dot(p.astype(BF16), v)
        l = li if l is None else l + li
        o = oi if o is None else o + oi
    return o / l


def _na_kernel(q_ref, k_ref, v_ref, bias_ref, o_ref, *, with_ctx):
    i = pl.program_id(0)
    qs = _stack_heads(q_ref[0])
    kc = k_ref[0, SEQ:NTOK, :]
    vc = v_ref[0, SEQ:NTOK, :]

    @pl.when(i < N_LAT_BLK)
    def _():
        start = pl.multiple_of(jnp.clip(i - 1, 0, N_LAT_BLK - NA_BAND // BLK) * BLK, BLK)
        kb = k_ref[0, pl.ds(start, NA_BAND), :]
        vb = v_ref[0, pl.ds(start, NA_BAND), :]
        s_loc = _dot_nt(qs, kb) + bias_ref[...].reshape(N_HEADS * BLK, NA_BAND)
        s_ctx = _dot_nt(qs, kc)
        o_ref[0] = _unstack_heads(_softmax_pv([s_loc, s_ctx], [vb, vc]), BLK).astype(BF16)

    if with_ctx:
        @pl.when(i == N_LAT_BLK)
        def _():
            o_ref[0] = _unstack_heads(_softmax_pv([_dot_nt(qs, kc)], [vc]), BLK).astype(BF16)


def _na_pattern(i):
    return jnp.minimum(i, 1) + (i >= N_LAT_BLK - 2).astype(jnp.int32) + (i >= N_LAT_BLK - 1).astype(jnp.int32)


def _neighbourhood(na, bias, *, layer, with_ctx):
    n_batch = na.shape[0]
    n_blk = N_BLK if with_ctx else N_LAT_BLK
    return pl.pallas_call(
        functools.partial(_na_kernel, with_ctx=with_ctx),
        grid=(n_blk, n_batch),
        in_specs=[
            pl.BlockSpec((1, BLK, BRANCH_W), lambda i, b: (b, i, 0)),
            pl.BlockSpec((1, NTOK, BRANCH_W), lambda i, b: (b, 0, 1)),
            pl.BlockSpec((1, NTOK, BRANCH_W), lambda i, b: (b, 0, 2)),
            pl.BlockSpec((N_HEADS, None, BLK, NA_BAND),
                         lambda i, b: (0, _na_pattern(jnp.minimum(i, N_LAT_BLK - 1)), 0, 0)),
        ],
        out_specs=pl.BlockSpec((1, BLK, BRANCH_W), lambda i, b: (b, i, 0)),
        out_shape=jax.ShapeDtypeStruct((n_batch, n_blk * BLK, BRANCH_W), BF16),
        compiler_params=pltpu.CompilerParams(
            dimension_semantics=("arbitrary", "arbitrary"), vmem_limit_bytes=V7X_VMEM_LIMIT),
        name=f"na_l{layer}",
    )(na, na, na, bias)


def _na_bias_tables(rpb):
    blocks = np.array([0, 1, N_LAT_BLK - 2, N_LAT_BLK - 1])
    qr = blocks[:, None] * NA_QROWS + np.arange(NA_QROWS)[None, :]
    r0 = np.clip(qr - NA_KH // 2, 0, GRID_ROWS - NA_KH)
    band0 = np.clip(blocks - 1, 0, N_LAT_BLK - NA_BAND // BLK) * NA_QROWS
    kr = band0[:, None] + np.arange(NA_BAND_ROWS)[None, :]
    row_ok = (kr[:, None, :] >= r0[:, :, None]) & (kr[:, None, :] < r0[:, :, None] + NA_KH)
    dr = np.clip(kr[:, None, :] - qr[:, :, None] + NA_KH - 1, 0, 2 * NA_KH - 2)
    col = np.arange(GRID_W)
    c0 = np.clip(col - NA_KW // 2, 0, GRID_W - NA_KW)
    col_ok = (col[None, :] >= c0[:, None]) & (col[None, :] < c0[:, None] + NA_KW)
    dc = np.clip(col[None, :] - col[:, None] + NA_KW - 1, 0, 2 * NA_KW - 2)
    ok = row_ok[:, :, None, :, None] & col_ok[None, None, :, None, :]
    oh_c = jnp.asarray(dc[:, :, None] == np.arange(2 * NA_KW - 1), dtype=F32)
    oh_r = jnp.asarray(dr[..., None] == np.arange(2 * NA_KH - 1), dtype=F32)
    by_col = jnp.einsum("hrc,xyc->hrxy", rpb, oh_c, precision=lax.Precision.HIGHEST)
    vals = jnp.einsum("pqkr,hrxy->hpqxky", oh_r, by_col, precision=lax.Precision.HIGHEST)
    tab = jnp.where(ok[None], vals * LOG2E, NEG)
    return tab.reshape(N_HEADS, 4, BLK, NA_BAND)


def _da_kernel(q_ref, k_ref, v_ref, lam_ref, g_ref, o_ref, kmax_scr, *, lambda_init, with_ctx):
    i = pl.program_id(1)
    lp = lam_ref[...]
    lam = (jnp.exp(jnp.sum(lp[0:1] * lp[1:2], axis=-1, keepdims=True))
           - jnp.exp(jnp.sum(lp[2:3] * lp[3:4], axis=-1, keepdims=True)) + lambda_init)
    q = q_ref[0]
    lane = lax.broadcasted_iota(jnp.int32, (1, BRANCH_W), 1)

    @pl.when(i == 0)
    def _():
        kf = k_ref[0].astype(F32)
        kn2 = _dot((kf * kf).astype(BF16), g_ref[...])
        kmax_scr[...] = jnp.broadcast_to(jnp.max(kn2, axis=0, keepdims=True), kmax_scr.shape)

    qf = q.astype(F32)
    qn2 = _dot((qf * qf).astype(BF16), g_ref[...])
    bound2 = jnp.max(qn2 * kmax_scr[0:1, :]) * DA_BOUND_SLACK
    safe = bound2 <= DA_SAFE_LOG2 ** 2

    def attend(k, v, stabilise):
        out = jnp.zeros((BLK, BRANCH_W), F32)
        for h in range(N_HEADS):
            lo = h * 2 * DA_DQK
            q1 = jnp.where((lane >= lo) & (lane < lo + DA_DQK), q, jnp.zeros_like(q))
            q2 = jnp.where((lane >= lo + DA_DQK) & (lane < lo + 2 * DA_DQK), q, jnp.zeros_like(q))
            s = _dot_nt(jnp.concatenate([q1, q2], axis=0), k)
            if stabilise:
                s = s - jnp.max(s, axis=-1, keepdims=True)
            p = jnp.exp2(s)
            r = 1.0 / jnp.sum(p, axis=-1, keepdims=True)
            o = _dot(p.astype(BF16), v)
            out = jnp.where(_head_mask(BRANCH_W, h), o[:BLK] * r[:BLK] - o[BLK:] * (lam * r[BLK:]), out)
        return out

    def both(k_fn, v_fn):
        @pl.when(safe)
        def _():
            o_ref[0] = attend(k_fn(), v_fn(), False)

        @pl.when(jnp.logical_not(safe))
        def _():
            o_ref[0] = attend(k_fn(), v_fn(), True)

    @pl.when(i < N_LAT_BLK)
    def _():
        both(lambda: k_ref[0], lambda: v_ref[0])

    if with_ctx:
        @pl.when(i == N_LAT_BLK)
        def _():
            both(lambda: k_ref[0, SEQ:NTOK, :], lambda: v_ref[0, SEQ:NTOK, :])


def _diff_attention(da, lam_p, gsum, *, layer, lambda_init, with_ctx):
    n_batch = da.shape[0]
    n_blk = N_BLK if with_ctx else N_LAT_BLK
    return pl.pallas_call(
        functools.partial(_da_kernel, lambda_init=lambda_init, with_ctx=with_ctx),
        grid=(n_batch, n_blk),
        in_specs=[
            pl.BlockSpec((1, BLK, BRANCH_W), lambda b, i: (b, i, 0)),
            pl.BlockSpec((1, NTOK, BRANCH_W), lambda b, i: (b, 0, 1)),
            pl.BlockSpec((1, NTOK, BRANCH_W), lambda b, i: (b, 0, 2)),
            pl.BlockSpec((None, 4, DA_DQK), lambda b, i: (layer, 0, 0)),
            pl.BlockSpec((BRANCH_W, BRANCH_W), lambda b, i: (0, 0)),
        ],
        out_specs=pl.BlockSpec((1, BLK, BRANCH_W), lambda b, i: (b, i, 0)),
        out_shape=jax.ShapeDtypeStruct((n_batch, n_blk * BLK, BRANCH_W), F32),
        scratch_shapes=[pltpu.VMEM((8, BRANCH_W), F32)],
        compiler_params=pltpu.CompilerParams(
            dimension_semantics=("arbitrary", "arbitrary"), vmem_limit_bytes=V7X_VMEM_LIMIT),
        name=f"da_l{layer}",
    )(da, da, da, lam_p, gsum)


def _fwd_chunk(g):
    return jnp.where(g == 0, N_LAT_BLK, g - 1)


def _bwd_chunk(g):
    return jnp.where(g == 0, N_LAT_BLK, N_LAT_BLK - g)


def _tri_mask(reverse):
    r = lax.broadcasted_iota(jnp.int32, (BLK, BLK), 0)
    c = lax.broadcasted_iota(jnp.int32, (BLK, BLK), 1)
    return (c >= r) if reverse else (c <= r)


def _cumsum_rows(x, mask):
    tri = jnp.where(mask, 1.0, 0.0).astype(BF16)
    return _dot_split(tri, x, 1, 3)


def _cummax_rows(x, reverse):
    n = x.shape[0]
    row = lax.broadcasted_iota(jnp.int32, (n, 1), 0)
    sh = 1
    while sh < n:
        if reverse:
            moved, ok = pltpu.roll(x, n - sh, axis=0), row < n - sh
        else:
            moved, ok = pltpu.roll(x, sh, axis=0), row >= sh
        x = jnp.maximum(x, jnp.where(ok, moved, NEG))
        sh *= 2
    return x


def _mlstm_chunk(e, d, q_ref, k_ref, v_ref, s_ref, gb_ref, ex_ref, o_ref, c_scr, n_scr, m_scr):
    reverse = d == 1
    mask = _tri_mask(reverse)
    end = 0 if reverse else BLK - 1
    st = 2 * e + d
    q = q_ref[e]
    k = k_ref[e]
    v = v_ref[e]
    qb = q.astype(BF16)
    kb = k.astype(BF16)
    vb = v.astype(BF16)
    il = 2 * N_HEADS * d
    lane = lax.broadcasted_iota(jnp.int32, (1, 128), 1)
    sel = (lane >= il) & (lane < il + N_HEADS)
    is_f = (lane >= il + ML_F_OFF) & (lane < il + ML_F_OFF + N_HEADS)
    gs = s_ref[e] + gb_ref[...]
    gl = jnp.where(is_f, jax.nn.log_sigmoid(gs), gs)
    bc = _cumsum_rows(gl, mask)
    b_al = pltpu.roll(bc, 128 - ML_F_OFF, axis=1)
    b_al = jnp.where(sel, b_al, 0.0)
    i_g = jnp.where(sel, gl, 0.0)
    u = i_g - b_al
    m_prev = m_scr[st, 0:1, :]
    c = jnp.maximum(_cummax_rows(u, reverse), m_prev)
    c3 = _split_f32(c * LOG2E, 3)
    u3 = _split_f32(u * LOG2E, 3)
    pc = c3[0] + pltpu.roll(c3[1], 16, axis=1) + pltpu.roll(c3[2], 32, axis=1)
    pu = pltpu.roll(u3[0], 48, axis=1) + pltpu.roll(u3[1], 64, axis=1) + pltpu.roll(u3[2], 80, axis=1)
    y = jnp.where(lane < 48, 1.0, pu).astype(BF16)
    xs = []
    for h in range(N_HEADS):
        l = il + h
        neg_c = (lane == l) | (lane == l + 16) | (lane == l + 32)
        ones = (lane == l + 48) | (lane == l + 64) | (lane == l + 80)
        xs.append(jnp.where(neg_c, -pc, jnp.where(ones, 1.0, 0.0)))
    e_all = _dot_nt(jnp.concatenate(xs, axis=0).astype(BF16), y)
    mask4 = jnp.concatenate([mask] * N_HEADS, axis=0)
    w_all = jnp.exp2(jnp.where(mask4, e_all, NEG))
    qk = (_dot_nt(_stack_heads(qb), kb) * w_all).astype(BF16)
    num = _unstack_heads(_dot(qk, vb), BLK)
    den = _unstack_heads(_dot(qk, jnp.ones((BLK, BRANCH_W), BF16)), BLK)
    ex = ex_ref[d]
    args = jnp.concatenate([jnp.where(sel, m_prev - c, 0.0), -(b_al + c)], axis=0)
    wide = jnp.exp(_dot_split(args, ex, 3, 1))
    w_inter = wide[:BLK]
    exp_neg_m = wide[BLK:]
    q_c = _dot(qb, c_scr[st].astype(BF16))
    q_n = _dot(qb, n_scr[st].astype(BF16))
    o_ref[e] = (num + w_inter * q_c) / jnp.maximum(jnp.abs(den + w_inter * q_n), exp_neg_m)
    b_end = b_al[end:end + 1, :]
    k_log = b_end - b_al + i_g
    m_new = jnp.maximum(b_end + m_prev, jnp.max(k_log, axis=0, keepdims=True))
    wk = jnp.where(sel, jnp.exp(k_log - m_new), 0.0)
    decay = jnp.where(sel, jnp.exp(b_end + m_prev - m_new), 0.0)
    kw = (k * _dot_split(wk, ex, 2, 1)).astype(BF16)
    decay_l = _dot_split(jnp.broadcast_to(decay, (8, 128)), ex, 3, 1)[0:1]
    r_head = lax.broadcasted_iota(jnp.int32, (BRANCH_W, BRANCH_W), 0) // (BRANCH_W // N_HEADS)
    c_head = lax.broadcasted_iota(jnp.int32, (BRANCH_W, BRANCH_W), 1) // (BRANCH_W // N_HEADS)
    diag = r_head == c_head
    c_scr[st] = decay_l * c_scr[st] + jnp.where(diag, _dot_tn(kw, vb), 0.0)
    n_sum = _dot_tn(kw, jnp.ones((BLK, BRANCH_W), BF16))
    n_scr[st] = decay_l * n_scr[st] + jnp.where(diag, n_sum, 0.0)
    m_scr[st, 0:1, :] = jnp.where(sel, m_new, 0.0)


def _mlstm_kernel(qf, kf, vf, sf, qb, kb, vb, sb, gb_ref, ex_ref, o_ref_f, o_ref_b, c_scr, n_scr, m_scr, *, group):
    @pl.when(pl.program_id(1) == 0)
    def _():
        c_scr[...] = jnp.zeros_like(c_scr)
        n_scr[...] = jnp.zeros_like(n_scr)
        m_scr[...] = jnp.zeros_like(m_scr)

    for e in range(group):
        _mlstm_chunk(e, 0, qf, kf, vf, sf, gb_ref, ex_ref, o_ref_f, c_scr, n_scr, m_scr)
        _mlstm_chunk(e, 1, qb, kb, vb, sb, gb_ref, ex_ref, o_ref_b, c_scr, n_scr, m_scr)


def _scan_group(n_batch):
    del n_batch
    return 1


def _scan_specs(group, width_blocks, chunk_fn):
    return [pl.BlockSpec((group, BLK, BRANCH_W), lambda b, g, j=j: (b, chunk_fn(g), j)) for j in width_blocks]


def _mlstm_expand_matrix():
    ex = np.zeros((2, 128, BRANCH_W), np.float32)
    per = BRANCH_W // N_HEADS
    for d in range(2):
        for h in range(N_HEADS):
            ex[d, 2 * N_HEADS * d + h, h * per:(h + 1) * per] = 1.0
    return jnp.asarray(ex, dtype=BF16)


def _mlstm(ml, small, gate_b, *, layer):
    n_batch = ml.shape[0]
    group = _scan_group(n_batch)
    small_spec = lambda fn: pl.BlockSpec((group, BLK, 128), lambda b, g: (b, fn(g), 0))
    out_spec = lambda fn: pl.BlockSpec((group, BLK, BRANCH_W), lambda b, g: (b, fn(g), 0))
    return pl.pallas_call(
        functools.partial(_mlstm_kernel, group=group),
        grid=(n_batch // group, N_BLK),
        in_specs=(_scan_specs(group, (0, 1, 2), _fwd_chunk) + [small_spec(_fwd_chunk)]
                  + _scan_specs(group, (0, 1, 2), _bwd_chunk) + [small_spec(_bwd_chunk)]
                  + [pl.BlockSpec((1, 128), lambda b, g: (0, 0)),
                     pl.BlockSpec((2, 128, BRANCH_W), lambda b, g: (0, 0, 0))]),
        out_specs=[out_spec(_fwd_chunk), out_spec(_bwd_chunk)],
        out_shape=[jax.ShapeDtypeStruct((n_batch, NTOK, BRANCH_W), F32)] * 2,
        scratch_shapes=[pltpu.VMEM((2 * group, BRANCH_W, BRANCH_W), F32),
                        pltpu.VMEM((2 * group, BRANCH_W, BRANCH_W), F32),
                        pltpu.VMEM((2 * group, 8, 128), F32)],
        compiler_params=pltpu.CompilerParams(
            dimension_semantics=("arbitrary", "arbitrary"), vmem_limit_bytes=V7X_VMEM_LIMIT),
        name=f"mlstm_l{layer}",
    )(ml, ml, ml, small, ml, ml, ml, small, gate_b, _mlstm_expand_matrix())


def _gla_chunk(e, d, qk_ref, v_ref, s_ref, w2_ref, ba_ref, o_ref, st_scr):
    reverse = d == 1
    mask = _tri_mask(reverse)
    end = 0 if reverse else BLK - 1
    half = BLK // 2
    slot = 2 * e + d
    q = qk_ref[e, :, 0:128]
    k = qk_ref[e, :, 128:256]
    vb = v_ref[e].astype(BF16)
    x = _dot_split(s_ref[e], w2_ref[d], 3, 3) + ba_ref[d]
    la = jax.nn.log_sigmoid(x) * (1.0 / GLA_TAU)
    bc = _cumsum_rows(la, mask)
    ref_row = bc[half:half + 1, :]
    qe = (q * jnp.exp(bc - ref_row)).astype(BF16)
    ke = (k * jnp.exp(ref_row - bc)).astype(BF16)
    att = _dot_nt(_stack_heads(qe), ke)
    mask4 = jnp.concatenate([mask] * N_HEADS, axis=0)
    o_all = _dot(jnp.where(mask4, att, 0.0).astype(BF16), vb)
    st = st_scr[slot]
    inter = _dot_nt((q * jnp.exp(bc)).astype(BF16), st.astype(BF16))
    o_ref[e] = _unstack_heads(o_all, BLK) + inter
    b_end = bc[end:end + 1, :]
    kend = (k * jnp.exp(b_end - bc)).astype(BF16)
    r_head = lax.broadcasted_iota(jnp.int32, (BRANCH_W, 128), 0) // (BRANCH_W // N_HEADS)
    c_head = lax.broadcasted_iota(jnp.int32, (BRANCH_W, 128), 1) // GLA_DK
    st_scr[slot] = jnp.exp(b_end) * st + jnp.where(r_head == c_head, _dot_tn(vb, kend), 0.0)


def _gla_kernel(qkf, vf, sf, qkb, vb, sb, w2_ref, ba_ref, o_ref_f, o_ref_b, st_scr, *, group):
    @pl.when(pl.program_id(1) == 0)
    def _():
        st_scr[...] = jnp.zeros_like(st_scr)

    for e in range(group):
        _gla_chunk(e, 0, qkf, vf, sf, w2_ref, ba_ref, o_ref_f, st_scr)
        _gla_chunk(e, 1, qkb, vb, sb, w2_ref, ba_ref, o_ref_b, st_scr)


def _gla(gla, small, w2pad, ba, *, layer):
    n_batch = gla.shape[0]
    group = _scan_group(n_batch)
    small_spec = lambda fn: pl.BlockSpec((group, BLK, 128), lambda b, g: (b, fn(g), 0))
    out_spec = lambda fn: pl.BlockSpec((group, BLK, BRANCH_W), lambda b, g: (b, fn(g), 0))
    return pl.pallas_call(
        functools.partial(_gla_kernel, group=group),
        grid=(n_batch // group, N_BLK),
        in_specs=(_scan_specs(group, (0, 1), _fwd_chunk) + [small_spec(_fwd_chunk)]
                  + _scan_specs(group, (0, 1), _bwd_chunk) + [small_spec(_bwd_chunk)]
                  + [pl.BlockSpec((2, 128, 128), lambda b, g: (0, 0, 0)),
                     pl.BlockSpec((2, 1, 128), lambda b, g: (0, 0, 0))]),
        out_specs=[out_spec(_fwd_chunk), out_spec(_bwd_chunk)],
        out_shape=[jax.ShapeDtypeStruct((n_batch, NTOK, BRANCH_W), F32)] * 2,
        scratch_shapes=[pltpu.VMEM((2 * group, BRANCH_W, 128), F32)],
        compiler_params=pltpu.CompilerParams(
            dimension_semantics=("arbitrary", "arbitrary"), vmem_limit_bytes=V7X_VMEM_LIMIT),
        name=f"gla_l{layer}",
    )(gla, gla, small, gla, gla, small, w2pad, ba)


def _group_mean(x, gmat):
    return _dot_split(x, gmat, 2, 1)


def _merge_kernel(x_ref, sh_ref, sc_ref, gt_ref, na_ref, mlf_ref, mlb_ref, mlo_ref, glf_ref, glb_ref, glg_ref,
                  da_ref, ng_ref, gm_ref, wg_ref, wb_ref, wo_ref, lng_ref, lnb_ref, o_ref,
                  *, n_batch, tm, alpha, lambda_init):
    b = pl.program_id(0)
    t = pl.program_id(1)
    is_ctx = _is_ctx_rows(t, tm)
    x = x_ref[0]
    h = (x * (1.0 + _mod_rows(sc_ref, b, n_batch, is_ctx)) + _mod_rows(sh_ref, b, n_batch, is_ctx)).astype(BF16)
    gmat = gm_ref[...]

    def rms_norm(y, gain):
        return y * lax.rsqrt(_group_mean(y * y, gmat) + HEAD_EPS) * gain

    h_ml = mlf_ref[0] + mlb_ref[0]
    y_ml = jax.nn.sigmoid(mlo_ref[0]) * rms_norm(h_ml - _group_mean(h_ml, gmat), ng_ref[0:1, :])
    g_gl = glg_ref[0]
    y_gla = g_gl * jax.nn.sigmoid(g_gl) * rms_norm(glf_ref[0] + glb_ref[0], ng_ref[1:2, :])
    y_da = (1.0 - lambda_init) * rms_norm(da_ref[0], ng_ref[2:3, :])
    ys = [na_ref[0], y_ml.astype(BF16), y_gla.astype(BF16), y_da.astype(BF16)]
    acc = None
    for j in range(N_BRANCH):
        gate = jax.nn.sigmoid(_dot(h, wg_ref[:, j * D_MODEL:(j + 1) * D_MODEL]))
        term = gate * _dot(ys[j], wb_ref[j])
        acc = term if acc is None else acc + term
    y = _dot(acc.astype(BF16), wo_ref[...])
    y = alpha * x + _mod_rows(gt_ref, b, n_batch, is_ctx) * y
    o_ref[0] = _layer_norm(y, lng_ref[0], lnb_ref[0])


def _merge(xs, mod, na_o, ml_o, ml, gla_o, gla, da_o, norm_g, gmat, w_gates, w_branch, w_out, ln_g, ln_b,
           *, layer, n_rows, tm, alpha, lambda_init):
    n_batch = xs.shape[0]
    rows = mod.shape[1]
    ln_idx = layer * 3 + 1
    kern = functools.partial(_merge_kernel, n_batch=n_batch, tm=tm, alpha=alpha, lambda_init=lambda_init)
    tok = lambda j: pl.BlockSpec((1, tm, BRANCH_W), lambda b, t: (b, t, j))
    return pl.pallas_call(
        kern,
        grid=(n_batch, n_rows // tm),
        in_specs=[
            pl.BlockSpec((1, tm, D_MODEL), lambda b, t: (b, t, 0)),
            _mod_spec(rows, layer, 3), _mod_spec(rows, layer, 4), _mod_spec(rows, layer, 5),
            tok(0), tok(0), tok(0), tok(3), tok(0), tok(0), tok(2), tok(0),
            pl.BlockSpec((8, BRANCH_W), lambda b, t: (0, 0)),
            pl.BlockSpec((BRANCH_W, BRANCH_W), lambda b, t: (0, 0)),
            _const_spec((D_MODEL, N_BRANCH * D_MODEL), lambda b, t: (0, 0)),
            _const_spec((None, N_BRANCH, BRANCH_W, D_MODEL), lambda b, t: (layer, 0, 0, 0)),
            _const_spec((None, D_MODEL, D_MODEL), lambda b, t: (layer, 0, 0)),
            pl.BlockSpec((1, 1, D_MODEL), lambda b, t: (ln_idx, 0, 0)),
            pl.BlockSpec((1, 1, D_MODEL), lambda b, t: (ln_idx, 0, 0)),
        ],
        out_specs=pl.BlockSpec((1, tm, D_MODEL), lambda b, t: (b, t, 0)),
        out_shape=jax.ShapeDtypeStruct((n_batch, n_rows, D_MODEL), F32),
        compiler_params=pltpu.CompilerParams(
            dimension_semantics=("arbitrary", "arbitrary"), vmem_limit_bytes=V7X_VMEM_LIMIT),
        name=f"merge_l{layer}",
    )(xs, mod, mod, mod, na_o, ml_o[0], ml_o[1], ml, gla_o[0], gla_o[1], gla, da_o, norm_g, gmat,
      w_gates, w_branch, w_out, ln_g, ln_b)


def _rope_tables():
    t = jnp.arange(SEQ)
    n_f = DA_DQK // 4
    inv = ROPE_BASE ** (-jnp.arange(n_f, dtype=F32) / n_f)

    def cs(pos):
        ang = pos.astype(F32)[:, None] * inv
        return jnp.cos(ang), jnp.sin(ang)

    (cr, sr), (cc, sc) = cs(t // GRID_W), cs(t % GRID_W)
    cos32 = jnp.concatenate([cr, cr, cc, cc], axis=-1)
    sin32 = jnp.concatenate([-sr, sr, -sc, sc], axis=-1)
    reps = BRANCH_W // DA_DQK
    cos = jnp.concatenate([jnp.tile(cos32, (1, reps)), jnp.ones((CTX_LEN, BRANCH_W), F32)], axis=0)
    sin = jnp.concatenate([jnp.tile(sin32, (1, reps)), jnp.zeros((CTX_LEN, BRANCH_W), F32)], axis=0)
    return cos, sin


def _swap_perm():
    n_f = DA_DQK // 4
    base = np.concatenate([np.arange(n_f, 2 * n_f), np.arange(0, n_f),
                           np.arange(3 * n_f, 4 * n_f), np.arange(2 * n_f, 3 * n_f)])
    return np.concatenate([g * DA_DQK + base for g in range(BRANCH_W // DA_DQK)])


def _prep_mix_weights(w):
    na, ml, ml_if = w[:, 0:768], w[:, 768:1792], w[:, 1792:1808]
    gla, gla_a = w[:, 1808:2576], w[:, 2576:2608]
    dq, dk, dv = w[:, 2608:2864], w[:, 2864:3120], w[:, 3120:3376]
    gates = w[:, 3376:]
    perm = _swap_perm()
    pad = jnp.zeros((D_MODEL, 128 - ml_if.shape[1] - gla_a.shape[1]), w.dtype)
    w_all = jnp.concatenate([na, ml, gla, ml_if, gla_a, pad, dq, dq[:, perm], dk, dk[:, perm], dv], axis=1)
    return w_all.astype(BF16), gates.astype(BF16)


def _col_scale():
    cs = np.ones((1, C_SMALL), np.float32)
    cs[:, C_NA:C_NA + 256] = 64 ** -0.5 * LOG2E
    cs[:, C_ML:C_ML + 256] = 64 ** -0.5
    cs[:, C_GLA:C_GLA + 128] = GLA_DK ** -0.5
    return jnp.asarray(cs)


def _group_mean_matrix():
    head = np.arange(BRANCH_W) // (BRANCH_W // N_HEADS)
    return jnp.asarray((head[:, None] == head[None, :]) / (BRANCH_W // N_HEADS), dtype=BF16)


def _da_group_sum_matrix():
    grp = np.arange(BRANCH_W) // DA_DQK
    return jnp.asarray(grp[:, None] == grp[None, :], dtype=BF16)


def kernel(x, c, ctx, c_ctx, w_ada, b_ada, ln_g, ln_b, ffn_w_in, ffn_w_out, w_mix_in, na_rpb, ml_gate_b,
           ml_norm_g, gla_w_a2, gla_b_a, gla_norm_g, da_lambda, da_norm_g, w_branch, w_out):
    n_batch = x.shape[0]
    depth = w_ada.shape[0]
    alpha = (2 * depth) ** 0.25
    mod_rows = -(-(n_batch + 1) // 8) * 8
    cc = jnp.concatenate([c, c_ctx[None, :], jnp.zeros((mod_rows - n_batch - 1, D_MODEL), F32)], axis=0)
    mod = _ada(cc, w_ada, b_ada)

    xs = jnp.concatenate([x, ctx], axis=1)
    w_in_bf = ffn_w_in.astype(BF16)
    w_out_bf = ffn_w_out.astype(BF16)
    w_branch_bf = w_branch.astype(BF16)
    w_o_bf = w_out.astype(BF16)
    ln_g3 = ln_g.reshape(depth * 3, 1, D_MODEL)
    ln_b3 = ln_b.reshape(depth * 3, 1, D_MODEL)
    cos_tab, sin_tab = _rope_tables()
    colscale = _col_scale()
    gmat = _group_mean_matrix()
    da_gsum = _da_group_sum_matrix()

    for l in range(depth):
        ctx_out = l < depth - 1
        lambda_init = 0.8 - 0.6 * math.exp(-0.3 * l)
        w_all, w_gates = _prep_mix_weights(w_mix_in[l])
        bias = _na_bias_tables(na_rpb[l])
        gate_b = jnp.concatenate([ml_gate_b[l], jnp.zeros((128 - ml_gate_b.shape[1],), F32)])[None, :]
        w2pad = jnp.zeros((2, 128, 128), F32)
        for d in range(2):
            lo = SMALL_A_OFF + d * GLA_RANK
            w2pad = w2pad.at[d, lo:lo + GLA_RANK, :].set(gla_w_a2[l, d])
        ba = gla_b_a[l][:, None, :]
        norm_g = jnp.concatenate([ml_norm_g[l][None], gla_norm_g[l][None], da_norm_g[l][None],
                                  jnp.zeros((5, BRANCH_W), F32)], axis=0)

        xs = _ffn(xs, mod, w_in_bf, w_out_bf, ln_g3, ln_b3, layer=l, sub=0, mod0=0,
                  n_rows=NTOK, tm=768, alpha=alpha)
        na, ml, gla, small, da = _inproj(xs, mod, w_all, colscale, cos_tab, sin_tab, layer=l, tm=768)
        na_o = _neighbourhood(na, bias, layer=l, with_ctx=ctx_out)
        da_o = _diff_attention(da, da_lambda, da_gsum, layer=l, lambda_init=lambda_init, with_ctx=ctx_out)
        ml_o = _mlstm(ml, small, gate_b, layer=l)
        gla_o = _gla(gla, small, w2pad, ba, layer=l)
        n_rows, tm = (NTOK, 768) if ctx_out else (SEQ, 512)
        xs = _merge(xs, mod, na_o, ml_o, ml, gla_o, gla, da_o, norm_g, gmat, w_gates, w_branch_bf, w_o_bf,
                    ln_g3, ln_b3, layer=l, n_rows=n_rows, tm=tm, alpha=alpha, lambda_init=lambda_init)
        xs = _ffn(xs, mod, w_in_bf, w_out_bf, ln_g3, ln_b3, layer=l, sub=1, mod0=6,
                  n_rows=n_rows, tm=tm, alpha=alpha)
    return xs
```

```python
import functools
import math

import jax
import jax.numpy as jnp
import numpy as np
from jax import lax
from jax.experimental import pallas as pl
from jax.experimental.pallas import tpu as pltpu

F32 = jnp.float32
BF16 = jnp.bfloat16

D_MODEL = 1024
SEQ = 2048
CTX_LEN = 256
NTOK = SEQ + CTX_LEN
GRID_W = 64
GRID_ROWS = SEQ // GRID_W
N_MOD = 9
D_FF = 2816
FFN_HALF = 0.5
N_BRANCH = 4
BRANCH_W = 256
N_HEADS = 4
NA_KH = 8
NA_KW = 16
ML_F_OFF = 4
GLA_DK = 32
GLA_RANK = 16
GLA_TAU = 16.0
DA_DQK = 32
ROPE_BASE = 10000.0
LN_EPS = 1e-5
HEAD_EPS = 1e-6
NEG = -1e30
LOG2E = math.log2(math.e)
DA_SAFE_LOG2 = 100.0
DA_BOUND_SLACK = 1.1

V7X_VMEM_LIMIT = 56 * 1024 * 1024
FF_CHUNK = 256
BLK = 256
N_BLK = NTOK // BLK
N_LAT_BLK = SEQ // BLK
NA_BAND_ROWS = 12
NA_BAND = NA_BAND_ROWS * GRID_W
NA_QROWS = BLK // GRID_W

C_NA = 0
C_ML = 768
C_GLA = 1792
C_SMALL = 2560
C_DA = 2688
N_PROJ = 3968
SMALL_A_OFF = 16


def _dot(a, b):
    return jnp.dot(a, b, preferred_element_type=F32)


def _dot_nt(a, b):
    return lax.dot_general(a, b, (((1,), (1,)), ((), ())), preferred_element_type=F32)


def _dot_tn(a, b):
    return lax.dot_general(a, b, (((0,), (0,)), ((), ())), preferred_element_type=F32)


def _split_f32(x, n):
    parts = []
    r = x
    for i in range(n):
        p = r.astype(BF16).astype(F32)
        parts.append(p)
        if i + 1 < n:
            r = r - p
    return parts


def _split(x, n):
    return [p.astype(BF16) for p in _split_f32(x, n)]


def _dot_split(a, b, na, nb, dot=_dot):
    pa = _split(a, na) if na > 1 else [a.astype(BF16)]
    pb = _split(b, nb) if nb > 1 else [b.astype(BF16)]
    acc = None
    for i, x in enumerate(pa):
        for j, y in enumerate(pb):
            if i + j < max(na, nb):
                t = dot(x, y)
                acc = t if acc is None else acc + t
    return acc


def _layer_norm(y, g, b):
    mu = jnp.mean(y, axis=-1, keepdims=True)
    yc = y - mu
    var = jnp.mean(yc * yc, axis=-1, keepdims=True)
    return yc * lax.rsqrt(var + LN_EPS) * g + b


def _mod_rows(ref, b, n_batch, is_ctx):
    lat = ref[0, pl.ds(b, 1), :]
    ctx = ref[0, n_batch:n_batch + 1, :]
    return jnp.where(is_ctx, ctx, lat)


def _is_ctx_rows(t, tm):
    return (t * tm + lax.broadcasted_iota(jnp.int32, (tm, 1), 0)) >= SEQ


def _head_mask(width, head, n=1):
    lane = lax.broadcasted_iota(jnp.int32, (n, width), 1)
    per = width // N_HEADS
    return (lane >= head * per) & (lane < (head + 1) * per)


def _stack_heads(x):
    w = x.shape[-1]
    return jnp.concatenate([jnp.where(_head_mask(w, h), x, jnp.zeros_like(x)) for h in range(N_HEADS)], axis=0)


def _batch_group(n_batch):
    return 4 if n_batch % 4 == 0 else 2 if n_batch % 2 == 0 else 1


def _unstack_heads(x, m):
    w = x.shape[-1]
    out = jnp.zeros((m, w), x.dtype)
    for h in range(N_HEADS):
        out = jnp.where(_head_mask(w, h), x[h * m:(h + 1) * m], out)
    return out


def _ada_kernel(c_ref, w_ref, b_ref, o_ref):
    c = c_ref[...]
    s = c * jax.nn.sigmoid(c)
    o_ref[0] = _dot_split(s, w_ref[0], 2, 2) + b_ref[0]


def _ada(cc, w_ada, b_ada):
    depth = w_ada.shape[0]
    rows = cc.shape[0]
    return pl.pallas_call(
        _ada_kernel,
        grid=(depth, N_MOD),
        in_specs=[
            pl.BlockSpec((rows, D_MODEL), lambda l, k: (0, 0)),
            pl.BlockSpec((1, D_MODEL, D_MODEL), lambda l, k: (l, 0, k)),
            pl.BlockSpec((1, 1, D_MODEL), lambda l, k: (l, 0, k)),
        ],
        out_specs=pl.BlockSpec((1, rows, D_MODEL), lambda l, k: (l, 0, k)),
        out_shape=jax.ShapeDtypeStruct((depth, rows, N_MOD * D_MODEL), F32),
        compiler_params=pltpu.CompilerParams(
            dimension_semantics=("arbitrary", "arbitrary"), vmem_limit_bytes=V7X_VMEM_LIMIT),
        name="ada_mod",
    )(cc, w_ada, b_ada.reshape(depth, 1, N_MOD * D_MODEL))


def _ffn_kernel(*refs, n_batch, tm, alpha, n_x):
    x_refs = refs[:n_x]
    sh_ref, sc_ref, gt_ref, w_in_ref, w_out_ref, lng_ref, lnb_ref, o_ref, g_scr = refs[n_x:]
    b = pl.program_id(0)
    t = pl.program_id(1)
    is_ctx = _is_ctx_rows(t, tm)
    if n_x == 1:
        x = x_refs[0][0]
    else:
        parts = [r[0] for r in x_refs[:-1]]
        parts[-1] = jnp.where(t == pl.num_programs(1) - 1, x_refs[-1][0], parts[-1])
        x = jnp.concatenate(parts, axis=0)
    h =(x * (1.0 + _mod_rows(sc_ref, b, n_batch, is_ctx)) + _mod_rows(sh_ref, b, n_batch, is_ctx)).astype(BF16)
    for j in range(D_FF // FF_CHUNK):
        lo = j * FF_CHUNK
        a = _dot(h, w_in_ref[:, lo:lo + FF_CHUNK])
        v = _dot(h, w_in_ref[:, D_FF + lo:D_FF + lo + FF_CHUNK])
        g_scr[:, lo:lo + FF_CHUNK] = (a * jax.nn.sigmoid(a) * v).astype(BF16)
    y = _dot(g_scr[...], w_out_ref[...])
    y = alpha * x + FFN_HALF * _mod_rows(gt_ref, b, n_batch, is_ctx) * y
    o_ref[0] = _layer_norm(y, lng_ref[0], lnb_ref[0])


def _const_spec(shape, index_map):
    return pl.BlockSpec(shape, index_map, pipeline_mode=pl.Buffered(1))


def _mod_spec(rows, layer, k):
    return pl.BlockSpec((1, rows, D_MODEL), lambda b, t: (layer, 0, k))


def _ffn(xs, mod, w_in, w_out, ln_g, ln_b, *, layer, sub, mod0, n_rows, tm, alpha, ctx=None):
    n_batch = xs.shape[0]
    rows = mod.shape[1]
    ln_idx = layer * 3 + (0 if sub == 0 else 2)
    if ctx is None:
        x_args = [xs]
        x_specs = [pl.BlockSpec((1, tm, D_MODEL), lambda b, t: (b, t, 0))]
    else:
        per = tm // BLK
        x_args = [xs] * per + [ctx]
        x_specs = [pl.BlockSpec((1, BLK, D_MODEL), lambda b, t, j=j: (b, jnp.minimum(t * per + j, N_LAT_BLK - 1), 0))
                   for j in range(per)]
        x_specs.append(pl.BlockSpec((1, CTX_LEN, D_MODEL), lambda b, t: (b, 0, 0)))
    kern = functools.partial(_ffn_kernel, n_batch=n_batch, tm=tm, alpha=alpha, n_x=len(x_args))
    return pl.pallas_call(
        kern,
        grid=(n_batch, n_rows // tm),
        in_specs=x_specs + [
            _mod_spec(rows, layer, mod0), _mod_spec(rows, layer, mod0 + 1), _mod_spec(rows, layer, mod0 + 2),
            _const_spec((None, None, D_MODEL, 2 * D_FF), lambda b, t: (layer, sub, 0, 0)),
            _const_spec((None, None, D_FF, D_MODEL), lambda b, t: (layer, sub, 0, 0)),
            pl.BlockSpec((1, 1, D_MODEL), lambda b, t: (ln_idx, 0, 0)),
            pl.BlockSpec((1, 1, D_MODEL), lambda b, t: (ln_idx, 0, 0)),
        ],
        out_specs=pl.BlockSpec((1, tm, D_MODEL), lambda b, t: (b, t, 0)),
        out_shape=jax.ShapeDtypeStruct((n_batch, n_rows, D_MODEL), F32),
        scratch_shapes=[pltpu.VMEM((tm, D_FF), BF16)],
        compiler_params=pltpu.CompilerParams(
            dimension_semantics=("arbitrary", "arbitrary"), vmem_limit_bytes=V7X_VMEM_LIMIT),
        name=f"ffn_l{layer}_s{sub}",
    )(*x_args, mod, mod, mod, w_in, w_out, ln_g, ln_b)


def _inproj_kernel(x_ref, sh_ref, sc_ref, w_ref, cs_ref, cos_ref, sin_ref,
                   na_ref, ml_ref, gla_ref, sm_ref, da_ref, *, n_batch, tm):
    b = pl.program_id(0)
    t = pl.program_id(1)
    is_ctx = _is_ctx_rows(t, tm)
    x = x_ref[0]
    h = (x * (1.0 + _mod_rows(sc_ref, b, n_batch, is_ctx)) + _mod_rows(sh_ref, b, n_batch, is_ctx)).astype(BF16)
    na_ref[0] = (_dot(h, w_ref[:, C_NA:C_ML]) * cs_ref[:, C_NA:C_ML]).astype(BF16)
    ml_ref[0] = _dot(h, w_ref[:, C_ML:C_GLA]) * cs_ref[:, C_ML:C_GLA]
    gla_ref[0] = _dot(h, w_ref[:, C_GLA:C_SMALL]) * cs_ref[:, C_GLA:C_SMALL]
    sm_ref[0] = _dot(h, w_ref[:, C_SMALL:C_DA])
    pd = _dot(h, w_ref[:, C_DA:N_PROJ])
    cos = cos_ref[...]
    sin = sin_ref[...]
    w = BRANCH_W
    da_ref[0, :, 0:w] = ((pd[:, 0:w] * cos + pd[:, w:2 * w] * sin) * (DA_DQK ** -0.5 * LOG2E)).astype(BF16)
    da_ref[0, :, w:2 * w] = (pd[:, 2 * w:3 * w] * cos + pd[:, 3 * w:4 * w] * sin).astype(BF16)
    da_ref[0, :, 2 * w:3 * w] = pd[:, 4 * w:5 * w].astype(BF16)


def _inproj(xs, mod, w_all, colscale, cos_tab, sin_tab, *, layer, tm):
    n_batch = xs.shape[0]
    rows = mod.shape[1]
    kern = functools.partial(_inproj_kernel, n_batch=n_batch, tm=tm)

    def out(width, dtype):
        return (pl.BlockSpec((1, tm, width), lambda b, t: (b, t, 0)),
                jax.ShapeDtypeStruct((n_batch, NTOK, width), dtype))

    outs = [out(768, BF16), out(1024, F32), out(768, F32), out(128, F32), out(768, BF16)]
    return pl.pallas_call(
        kern,
        grid=(n_batch, NTOK // tm),
        in_specs=[
            pl.BlockSpec((1, tm, D_MODEL), lambda b, t: (b, t, 0)),
            _mod_spec(rows, layer, 3), _mod_spec(rows, layer, 4),
            _const_spec((D_MODEL, N_PROJ), lambda b, t: (0, 0)),
            pl.BlockSpec((1, C_SMALL), lambda b, t: (0, 0)),
            pl.BlockSpec((tm, BRANCH_W), lambda b, t: (t, 0)),
            pl.BlockSpec((tm, BRANCH_W), lambda b, t: (t, 0)),
        ],
        out_specs=[o[0] for o in outs],
        out_shape=[o[1] for o in outs],
        compiler_params=pltpu.CompilerParams(
            dimension_semantics=("arbitrary", "arbitrary"), vmem_limit_bytes=V7X_VMEM_LIMIT),
        name=f"inproj_l{layer}",
    )(xs, mod, mod, w_all, colscale, cos_tab, sin_tab)


def _softmax_pv(parts, vals):
    m = None
    for s in parts:
        mi = jnp.max(s, axis=-1, keepdims=True)
        m = mi if m is None else jnp.maximum(m, mi)
    l = None
    o = None
    for s, v in zip(parts, vals):
        p = jnp.exp2(s - m)
        li = jnp.sum(p, axis=-1, keepdims=True)
        oi = _dot(p.astype(BF16), v)
        l = li if l is None else l + li
        o = oi if o is None else o + oi
    return o / l


def _na_kernel(q_ref, k_ref, v_ref, bias_ref, o_ref, *, with_ctx, group):
    i = pl.program_id(0)

    @pl.when(i < N_LAT_BLK)
    def _():
        start = pl.multiple_of(jnp.clip(i - 1, 0, N_LAT_BLK - NA_BAND // BLK) * BLK, BLK)
        bias = bias_ref[...].reshape(N_HEADS * BLK, NA_BAND)
        for e in range(group):
            qs = _stack_heads(q_ref[e])
            kb = k_ref[e, pl.ds(start, NA_BAND), :]
            vb = v_ref[e, pl.ds(start, NA_BAND), :]
            s_loc = _dot_nt(qs, kb) + bias
            s_ctx = _dot_nt(qs, k_ref[e, SEQ:NTOK, :])
            o = _softmax_pv([s_loc, s_ctx], [vb, v_ref[e, SEQ:NTOK, :]])
            o_ref[e] = _unstack_heads(o, BLK).astype(BF16)

    if with_ctx:
        @pl.when(i == N_LAT_BLK)
        def _():
            for e in range(group):
                qs = _stack_heads(q_ref[e])
                vc = v_ref[e, SEQ:NTOK, :]
                o = _softmax_pv([_dot_nt(qs, k_ref[e, SEQ:NTOK, :])], [vc])
                o_ref[e] = _unstack_heads(o, BLK).astype(BF16)


def _na_pattern(i):
    return jnp.minimum(i, 1) + (i >= N_LAT_BLK - 2).astype(jnp.int32) + (i >= N_LAT_BLK - 1).astype(jnp.int32)


def _neighbourhood(na, bias, *, layer, with_ctx):
    n_batch = na.shape[0]
    n_blk = N_BLK if with_ctx else N_LAT_BLK
    group = _batch_group(n_batch)
    return pl.pallas_call(
        functools.partial(_na_kernel, with_ctx=with_ctx, group=group),
        grid=(n_blk, n_batch // group),
        in_specs=[
            pl.BlockSpec((group, BLK, BRANCH_W), lambda i, b: (b, i, 0)),
            pl.BlockSpec((group, NTOK, BRANCH_W), lambda i, b: (b, 0, 1)),
            pl.BlockSpec((group, NTOK, BRANCH_W), lambda i, b: (b, 0, 2)),
            pl.BlockSpec((N_HEADS, None, BLK, NA_BAND),
                         lambda i, b: (0, _na_pattern(jnp.minimum(i, N_LAT_BLK - 1)), 0, 0)),
        ],
        out_specs=pl.BlockSpec((group, BLK, BRANCH_W), lambda i, b: (b, i, 0)),
        out_shape=jax.ShapeDtypeStruct((n_batch, n_blk * BLK, BRANCH_W), BF16),
        compiler_params=pltpu.CompilerParams(
            dimension_semantics=("arbitrary", "arbitrary"), vmem_limit_bytes=V7X_VMEM_LIMIT),
        name=f"na_l{layer}",
    )(na, na, na, bias)


def _na_bias_tables(rpb):
    blocks = np.array([0, 1, N_LAT_BLK - 2, N_LAT_BLK - 1])
    qr = blocks[:, None] * NA_QROWS + np.arange(NA_QROWS)[None, :]
    r0 = np.clip(qr - NA_KH // 2, 0, GRID_ROWS - NA_KH)
    band0 = np.clip(blocks - 1, 0, N_LAT_BLK - NA_BAND // BLK) * NA_QROWS
    kr = band0[:, None] + np.arange(NA_BAND_ROWS)[None, :]
    row_ok = (kr[:, None, :] >= r0[:, :, None]) & (kr[:, None, :] < r0[:, :, None] + NA_KH)
    dr = np.clip(kr[:, None, :] - qr[:, :, None] + NA_KH - 1, 0, 2 * NA_KH - 2)
    col = np.arange(GRID_W)
    c0 = np.clip(col - NA_KW // 2, 0, GRID_W - NA_KW)
    col_ok = (col[None, :] >= c0[:, None]) & (col[None, :] < c0[:, None] + NA_KW)
    dc = np.clip(col[None, :] - col[:, None] + NA_KW - 1, 0, 2 * NA_KW - 2)
    ok = row_ok[:, :, None, :, None] & col_ok[None, None, :, None, :]
    oh_c = jnp.asarray(dc[:, :, None] == np.arange(2 * NA_KW - 1), dtype=F32)
    oh_r = jnp.asarray(dr[..., None] == np.arange(2 * NA_KH - 1), dtype=F32)
    by_col = jnp.einsum("hrc,xyc->hrxy", rpb, oh_c, precision=lax.Precision.HIGHEST)
    vals = jnp.einsum("pqkr,hrxy->hpqxky", oh_r, by_col, precision=lax.Precision.HIGHEST)
    tab = jnp.where(ok[None], vals * LOG2E, NEG)
    return tab.reshape(N_HEADS, 4, BLK, NA_BAND)


def _da_kernel(q_ref, k_ref, v_ref, lam_ref, g_ref, o_ref, kmax_scr, *, lambda_init, with_ctx):
    i = pl.program_id(1)
    lp = lam_ref[...]
    lam = (jnp.exp(jnp.sum(lp[0:1] * lp[1:2], axis=-1, keepdims=True))
           - jnp.exp(jnp.sum(lp[2:3] * lp[3:4], axis=-1, keepdims=True)) + lambda_init)
    q = q_ref[0]
    lane = lax.broadcasted_iota(jnp.int32, (1, BRANCH_W), 1)

    @pl.when(i == 0)
    def _():
        kf = k_ref[0].astype(F32)
        kn2 = _dot((kf * kf).astype(BF16), g_ref[...])
        kmax_scr[...] = jnp.broadcast_to(jnp.max(kn2, axis=0, keepdims=True), kmax_scr.shape)

    qf = q.astype(F32)
    qn2 = _dot((qf * qf).astype(BF16), g_ref[...])
    bound2 = jnp.max(qn2 * kmax_scr[0:1, :]) * DA_BOUND_SLACK
    safe = bound2 <= DA_SAFE_LOG2 ** 2

    def attend(k, v, stabilise):
        out = jnp.zeros((BLK, BRANCH_W), F32)
        for h in range(N_HEADS):
            lo = h * 2 * DA_DQK
            q1 = jnp.where((lane >= lo) & (lane < lo + DA_DQK), q, jnp.zeros_like(q))
            q2 = jnp.where((lane >= lo + DA_DQK) & (lane < lo + 2 * DA_DQK), q, jnp.zeros_like(q))
            s = _dot_nt(jnp.concatenate([q1, q2], axis=0), k)
            if stabilise:
                s = s - jnp.max(s, axis=-1, keepdims=True)
            p = jnp.exp2(s)
            r = 1.0 / jnp.sum(p, axis=-1, keepdims=True)
            o = _dot(p.astype(BF16), v)
            out = jnp.where(_head_mask(BRANCH_W, h), o[:BLK] * r[:BLK] - o[BLK:] * (lam * r[BLK:]), out)
        return out

    def both(k_fn, v_fn):
        @pl.when(safe)
        def _():
            o_ref[0] = attend(k_fn(), v_fn(), False)

        @pl.when(jnp.logical_not(safe))
        def _():
            o_ref[0] = attend(k_fn(), v_fn(), True)

    @pl.when(i < N_LAT_BLK)
    def _():
        both(lambda: k_ref[0], lambda: v_ref[0])

    if with_ctx:
        @pl.when(i == N_LAT_BLK)
        def _():
            both(lambda: k_ref[0, SEQ:NTOK, :], lambda: v_ref[0, SEQ:NTOK, :])


def _diff_attention(da, lam_p, gsum, *, layer, lambda_init, with_ctx):
    n_batch = da.shape[0]
    n_blk = N_BLK if with_ctx else N_LAT_BLK
    return pl.pallas_call(
        functools.partial(_da_kernel, lambda_init=lambda_init, with_ctx=with_ctx),
        grid=(n_batch, n_blk),
        in_specs=[
            pl.BlockSpec((1, BLK, BRANCH_W), lambda b, i: (b, i, 0)),
            pl.BlockSpec((1, NTOK, BRANCH_W), lambda b, i: (b, 0, 1)),
            pl.BlockSpec((1, NTOK, BRANCH_W), lambda b, i: (b, 0, 2)),
            pl.BlockSpec((None, 4, DA_DQK), lambda b, i: (layer, 0, 0)),
            pl.BlockSpec((BRANCH_W, BRANCH_W), lambda b, i: (0, 0)),
        ],
        out_specs=pl.BlockSpec((1, BLK, BRANCH_W), lambda b, i: (b, i, 0)),
        out_shape=jax.ShapeDtypeStruct((n_batch, n_blk * BLK, BRANCH_W), F32),
        scratch_shapes=[pltpu.VMEM((8, BRANCH_W), F32)],
        compiler_params=pltpu.CompilerParams(
            dimension_semantics=("arbitrary", "arbitrary"), vmem_limit_bytes=V7X_VMEM_LIMIT),
        name=f"da_l{layer}",
    )(da, da, da, lam_p, gsum)


def _fwd_chunk(g):
    return jnp.where(g == 0, N_LAT_BLK, g - 1)


def _bwd_chunk(g):
    return jnp.where(g == 0, N_LAT_BLK, N_LAT_BLK - g)


def _tri_mask(reverse):
    r = lax.broadcasted_iota(jnp.int32, (BLK, BLK), 0)
    c = lax.broadcasted_iota(jnp.int32, (BLK, BLK), 1)
    return (c >= r) if reverse else (c <= r)


def _cumsum_rows(x, mask):
    tri = jnp.where(mask, 1.0, 0.0).astype(BF16)
    return _dot_split(tri, x, 1, 3)


def _cummax_rows(x, reverse):
    n = x.shape[0]
    row = lax.broadcasted_iota(jnp.int32, (n, 1), 0)
    sh = 1
    while sh < n:
        if reverse:
            moved, ok = pltpu.roll(x, n - sh, axis=0), row < n - sh
        else:
            moved, ok = pltpu.roll(x, sh, axis=0), row >= sh
        x = jnp.maximum(x, jnp.where(ok, moved, NEG))
        sh *= 2
    return x


def _mlstm_chunk(e, d, q_ref, k_ref, v_ref, s_ref, gb_ref, ex_ref, o_ref, c_scr, n_scr, m_scr):
    reverse = d == 1
    mask = _tri_mask(reverse)
    end = 0 if reverse else BLK - 1
    st = 2 * e + d
    q = q_ref[e]
    k = k_ref[e]
    v = v_ref[e]
    qb = q.astype(BF16)
    kb = k.astype(BF16)
    vb = v.astype(BF16)
    il = 2 * N_HEADS * d
    lane = lax.broadcasted_iota(jnp.int32, (1, 128), 1)
    sel = (lane >= il) & (lane < il + N_HEADS)
    is_f = (lane >= il + ML_F_OFF) & (lane < il + ML_F_OFF + N_HEADS)
    gs = s_ref[e] + gb_ref[...]
    gl = jnp.where(is_f, jax.nn.log_sigmoid(gs), gs)
    bc = _cumsum_rows(gl, mask)
    b_al = pltpu.roll(bc, 128 - ML_F_OFF, axis=1)
    b_al = jnp.where(sel, b_al, 0.0)
    i_g = jnp.where(sel, gl, 0.0)
    u = i_g - b_al
    m_prev = m_scr[st, 0:1, :]
    c = jnp.maximum(_cummax_rows(u, reverse), m_prev)
    c3 = _split_f32(c * LOG2E, 3)
    u3 = _split_f32(u * LOG2E, 3)
    pc = c3[0] + pltpu.roll(c3[1], 16, axis=1) + pltpu.roll(c3[2], 32, axis=1)
    pu = pltpu.roll(u3[0], 48, axis=1) + pltpu.roll(u3[1], 64, axis=1) + pltpu.roll(u3[2], 80, axis=1)
    y = jnp.where(lane < 48, 1.0, pu).astype(BF16)
    xs = []
    for h in range(N_HEADS):
        l = il + h
        neg_c = (lane == l) | (lane == l + 16) | (lane == l + 32)
        ones = (lane == l + 48) | (lane == l + 64) | (lane == l + 80)
        xs.append(jnp.where(neg_c, -pc, jnp.where(ones, 1.0, 0.0)))
    e_all = _dot_nt(jnp.concatenate(xs, axis=0).astype(BF16), y)
    mask4 = jnp.concatenate([mask] * N_HEADS, axis=0)
    w_all = jnp.exp2(jnp.where(mask4, e_all, NEG))
    qk = (_dot_nt(_stack_heads(qb), kb) * w_all).astype(BF16)
    num = _unstack_heads(_dot(qk, vb), BLK)
    den = _unstack_heads(_dot(qk, jnp.ones((BLK, BRANCH_W), BF16)), BLK)
    ex = ex_ref[d]
    args = jnp.concatenate([jnp.where(sel, m_prev - c, 0.0), -(b_al + c)], axis=0)
    wide = jnp.exp(_dot_split(args, ex, 3, 1))
    w_inter = wide[:BLK]
    exp_neg_m = wide[BLK:]
    q_c = _dot(qb, c_scr[st].astype(BF16))
    q_n = _dot(qb, n_scr[st].astype(BF16))
    o_ref[e] = (num + w_inter * q_c) / jnp.maximum(jnp.abs(den + w_inter * q_n), exp_neg_m)
    b_end = b_al[end:end + 1, :]
    k_log = b_end - b_al + i_g
    m_new = jnp.maximum(b_end + m_prev, jnp.max(k_log, axis=0, keepdims=True))
    wk = jnp.where(sel, jnp.exp(k_log - m_new), 0.0)
    decay = jnp.where(sel, jnp.exp(b_end + m_prev - m_new), 0.0)
    kw = (k * _dot_split(wk, ex, 2, 1)).astype(BF16)
    decay_l = _dot_split(jnp.broadcast_to(decay, (8, 128)), ex, 3, 1)[0:1]
    r_head = lax.broadcasted_iota(jnp.int32, (BRANCH_W, BRANCH_W), 0) // (BRANCH_W // N_HEADS)
    c_head = lax.broadcasted_iota(jnp.int32, (BRANCH_W, BRANCH_W), 1) // (BRANCH_W // N_HEADS)
    diag = r_head == c_head
    c_scr[st] = decay_l * c_scr[st] + jnp.where(diag, _dot_tn(kw, vb), 0.0)
    n_sum = _dot_tn(kw, jnp.ones((BLK, BRANCH_W), BF16))
    n_scr[st] = decay_l * n_scr[st] + jnp.where(diag, n_sum, 0.0)
    m_scr[st, 0:1, :] = jnp.where(sel, m_new, 0.0)


def _mlstm_kernel(qf, kf, vf, sf, qb, kb, vb, sb, gb_ref, ex_ref, o_ref_f, o_ref_b, c_scr, n_scr, m_scr, *, group):
    @pl.when(pl.program_id(1) == 0)
    def _():
        c_scr[...] = jnp.zeros_like(c_scr)
        n_scr[...] = jnp.zeros_like(n_scr)
        m_scr[...] = jnp.zeros_like(m_scr)

    for e in range(group):
        _mlstm_chunk(e, 0, qf, kf, vf, sf, gb_ref, ex_ref, o_ref_f, c_scr, n_scr, m_scr)
        _mlstm_chunk(e, 1, qb, kb, vb, sb, gb_ref, ex_ref, o_ref_b, c_scr, n_scr, m_scr)


def _scan_specs(group, width_blocks, chunk_fn):
    return [pl.BlockSpec((group, BLK, BRANCH_W), lambda b, g, j=j: (b, chunk_fn(g), j)) for j in width_blocks]


def _mlstm_expand_matrix():
    ex = np.zeros((2, 128, BRANCH_W), np.float32)
    per = BRANCH_W // N_HEADS
    for d in range(2):
        for h in range(N_HEADS):
            ex[d, 2 * N_HEADS * d + h, h * per:(h + 1) * per] = 1.0
    return jnp.asarray(ex, dtype=BF16)


def _mlstm(ml, small, gate_b, *, layer):
    n_batch = ml.shape[0]
    group = 2 if n_batch % 2 == 0 else 1
    small_spec = lambda fn: pl.BlockSpec((group, BLK, 128), lambda b, g: (b, fn(g), 0))
    out_spec = lambda fn: pl.BlockSpec((group, BLK, BRANCH_W), lambda b, g: (b, fn(g), 0))
    return pl.pallas_call(
        functools.partial(_mlstm_kernel, group=group),
        grid=(n_batch // group, N_BLK),
        in_specs=(_scan_specs(group, (0, 1, 2), _fwd_chunk) + [small_spec(_fwd_chunk)]
                  + _scan_specs(group, (0, 1, 2), _bwd_chunk) + [small_spec(_bwd_chunk)]
                  + [pl.BlockSpec((1, 128), lambda b, g: (0, 0)),
                     pl.BlockSpec((2, 128, BRANCH_W), lambda b, g: (0, 0, 0))]),
        out_specs=[out_spec(_fwd_chunk), out_spec(_bwd_chunk)],
        out_shape=[jax.ShapeDtypeStruct((n_batch, NTOK, BRANCH_W), F32)] * 2,
        scratch_shapes=[pltpu.VMEM((2 * group, BRANCH_W, BRANCH_W), F32),
                        pltpu.VMEM((2 * group, BRANCH_W, BRANCH_W), F32),
                        pltpu.VMEM((2 * group, 8, 128), F32)],
        compiler_params=pltpu.CompilerParams(
            dimension_semantics=("arbitrary", "arbitrary"), vmem_limit_bytes=V7X_VMEM_LIMIT),
        name=f"mlstm_l{layer}",
    )(ml, ml, ml, small, ml, ml, ml, small, gate_b, _mlstm_expand_matrix())


def _gla_chunk(e, d, qk_ref, v_ref, s_ref, w2_ref, ba_ref, o_ref, st_scr):
    reverse = d == 1
    mask = _tri_mask(reverse)
    end = 0 if reverse else BLK - 1
    half = BLK // 2
    slot = 2 * e + d
    q = qk_ref[e, :, 0:128]
    k = qk_ref[e, :, 128:256]
    vb = v_ref[e].astype(BF16)
    x = _dot_split(s_ref[e], w2_ref[d], 2, 2) + ba_ref[d]
    la = jax.nn.log_sigmoid(x) * (1.0 / GLA_TAU)
    bc = _cumsum_rows(la, mask)
    ref_row = bc[half:half + 1, :]
    qe = (q * jnp.exp(bc - ref_row)).astype(BF16)
    ke = (k * jnp.exp(ref_row - bc)).astype(BF16)
    att = _dot_nt(_stack_heads(qe), ke)
    mask4 = jnp.concatenate([mask] * N_HEADS, axis=0)
    o_all = _dot(jnp.where(mask4, att, 0.0).astype(BF16), vb)
    st = st_scr[slot]
    inter = _dot_nt((q * jnp.exp(bc)).astype(BF16), st.astype(BF16))
    o_ref[e] = _unstack_heads(o_all, BLK) + inter
    b_end = bc[end:end + 1, :]
    kend = (k * jnp.exp(b_end - bc)).astype(BF16)
    r_head = lax.broadcasted_iota(jnp.int32, (BRANCH_W, 128), 0) // (BRANCH_W // N_HEADS)
    c_head = lax.broadcasted_iota(jnp.int32, (BRANCH_W, 128), 1) // GLA_DK
    st_scr[slot] = jnp.exp(b_end) * st + jnp.where(r_head == c_head, _dot_tn(vb, kend), 0.0)


def _gla_kernel(qkf, vf, sf, qkb, vb, sb, w2_ref, ba_ref, o_ref_f, o_ref_b, st_scr, *, group):
    @pl.when(pl.program_id(1) == 0)
    def _():
        st_scr[...] = jnp.zeros_like(st_scr)

    for e in range(group):
        _gla_chunk(e, 0, qkf, vf, sf, w2_ref, ba_ref, o_ref_f, st_scr)
        _gla_chunk(e, 1, qkb, vb, sb, w2_ref, ba_ref, o_ref_b, st_scr)


def _gla(gla, small, w2pad, ba, *, layer):
    n_batch = gla.shape[0]
    group = _batch_group(n_batch)
    small_spec = lambda fn: pl.BlockSpec((group, BLK, 128), lambda b, g: (b, fn(g), 0))
    out_spec = lambda fn: pl.BlockSpec((group, BLK, BRANCH_W), lambda b, g: (b, fn(g), 0))
    return pl.pallas_call(
        functools.partial(_gla_kernel, group=group),
        grid=(n_batch // group, N_BLK),
        in_specs=(_scan_specs(group, (0, 1), _fwd_chunk) + [small_spec(_fwd_chunk)]
                  + _scan_specs(group, (0, 1), _bwd_chunk) + [small_spec(_bwd_chunk)]
                  + [pl.BlockSpec((2, 128, 128), lambda b, g: (0, 0, 0)),
                     pl.BlockSpec((2, 1, 128), lambda b, g: (0, 0, 0))]),
        out_specs=[out_spec(_fwd_chunk), out_spec(_bwd_chunk)],
        out_shape=[jax.ShapeDtypeStruct((n_batch, NTOK, BRANCH_W), F32)] * 2,
        scratch_shapes=[pltpu.VMEM((2 * group, BRANCH_W, 128), F32)],
        compiler_params=pltpu.CompilerParams(
            dimension_semantics=("arbitrary", "arbitrary"), vmem_limit_bytes=V7X_VMEM_LIMIT),
        name=f"gla_l{layer}",
    )(gla, gla, small, gla, gla, small, w2pad, ba)


def _group_mean(x, gmat):
    return _dot_split(x, gmat, 2, 1)


def _merge_kernel(x_ref, sh_ref, sc_ref, gt_ref, na_ref, mlf_ref, mlb_ref, mlo_ref, glf_ref, glb_ref, glg_ref,
                  da_ref, ng_ref, gm_ref, wg_ref, wb_ref, wo_ref, lng_ref, lnb_ref, o_ref,
                  *, n_batch, tm, alpha, lambda_init):
    b = pl.program_id(0)
    t = pl.program_id(1)
    is_ctx = _is_ctx_rows(t, tm)
    x = x_ref[0]
    h = (x * (1.0 + _mod_rows(sc_ref, b, n_batch, is_ctx)) + _mod_rows(sh_ref, b, n_batch, is_ctx)).astype(BF16)
    gmat = gm_ref[...]

    def rms_norm(y, gain):
        return y * lax.rsqrt(_group_mean(y * y, gmat) + HEAD_EPS) * gain

    h_ml = mlf_ref[0] + mlb_ref[0]
    y_ml = jax.nn.sigmoid(mlo_ref[0]) * rms_norm(h_ml - _group_mean(h_ml, gmat), ng_ref[0:1, :])
    g_gl = glg_ref[0]
    y_gla = g_gl * jax.nn.sigmoid(g_gl) * rms_norm(glf_ref[0] + glb_ref[0], ng_ref[1:2, :])
    y_da = (1.0 - lambda_init) * rms_norm(da_ref[0], ng_ref[2:3, :])
    ys = [na_ref[0], y_ml.astype(BF16), y_gla.astype(BF16), y_da.astype(BF16)]
    acc = None
    for j in range(N_BRANCH):
        gate = jax.nn.sigmoid(_dot(h, wg_ref[:, j * D_MODEL:(j + 1) * D_MODEL]))
        term = gate * _dot(ys[j], wb_ref[j])
        acc = term if acc is None else acc + term
    y = _dot(acc.astype(BF16), wo_ref[...])
    y = alpha * x + _mod_rows(gt_ref, b, n_batch, is_ctx) * y
    o_ref[0] = _layer_norm(y, lng_ref[0], lnb_ref[0])


def _merge(xs, mod, na_o, ml_o, ml, gla_o, gla, da_o, norm_g, gmat, w_gates, w_branch, w_out, ln_g, ln_b,
           *, layer, n_rows, tm, alpha, lambda_init):
    n_batch = xs.shape[0]
    rows = mod.shape[1]
    ln_idx = layer * 3 + 1
    kern = functools.partial(_merge_kernel, n_batch=n_batch, tm=tm, alpha=alpha, lambda_init=lambda_init)
    tok = lambda j: pl.BlockSpec((1, tm, BRANCH_W), lambda b, t: (b, t, j))
    return pl.pallas_call(
        kern,
        grid=(n_batch, n_rows // tm),
        in_specs=[
            pl.BlockSpec((1, tm, D_MODEL), lambda b, t: (b, t, 0)),
            _mod_spec(rows, layer, 3), _mod_spec(rows, layer, 4), _mod_spec(rows, layer, 5),
            tok(0), tok(0), tok(0), tok(3), tok(0), tok(0), tok(2), tok(0),
            pl.BlockSpec((8, BRANCH_W), lambda b, t: (0, 0)),
            pl.BlockSpec((BRANCH_W, BRANCH_W), lambda b, t: (0, 0)),
            _const_spec((D_MODEL, N_BRANCH * D_MODEL), lambda b, t: (0, 0)),
            _const_spec((None, N_BRANCH, BRANCH_W, D_MODEL), lambda b, t: (layer, 0, 0, 0)),
            _const_spec((None, D_MODEL, D_MODEL), lambda b, t: (layer, 0, 0)),
            pl.BlockSpec((1, 1, D_MODEL), lambda b, t: (ln_idx, 0, 0)),
            pl.BlockSpec((1, 1, D_MODEL), lambda b, t: (ln_idx, 0, 0)),
        ],
        out_specs=pl.BlockSpec((1, tm, D_MODEL), lambda b, t: (b, t, 0)),
        out_shape=jax.ShapeDtypeStruct((n_batch, n_rows, D_MODEL), F32),
        compiler_params=pltpu.CompilerParams(
            dimension_semantics=("arbitrary", "arbitrary"), vmem_limit_bytes=V7X_VMEM_LIMIT),
        name=f"merge_l{layer}",
    )(xs, mod, mod, mod, na_o, ml_o[0], ml_o[1], ml, gla_o[0], gla_o[1], gla, da_o, norm_g, gmat,
      w_gates, w_branch, w_out, ln_g, ln_b)


def _rope_tables():
    t = jnp.arange(SEQ)
    n_f = DA_DQK // 4
    inv = ROPE_BASE ** (-jnp.arange(n_f, dtype=F32) / n_f)

    def cs(pos):
        ang = pos.astype(F32)[:, None] * inv
        return jnp.cos(ang), jnp.sin(ang)

    (cr, sr), (cc, sc) = cs(t // GRID_W), cs(t % GRID_W)
    cos32 = jnp.concatenate([cr, cr, cc, cc], axis=-1)
    sin32 = jnp.concatenate([-sr, sr, -sc, sc], axis=-1)
    reps = BRANCH_W // DA_DQK
    cos = jnp.concatenate([jnp.tile(cos32, (1, reps)), jnp.ones((CTX_LEN, BRANCH_W), F32)], axis=0)
    sin = jnp.concatenate([jnp.tile(sin32, (1, reps)), jnp.zeros((CTX_LEN, BRANCH_W), F32)], axis=0)
    return cos, sin


def _swap_perm():
    n_f = DA_DQK // 4
    base = np.concatenate([np.arange(n_f, 2 * n_f), np.arange(0, n_f),
                           np.arange(3 * n_f, 4 * n_f), np.arange(2 * n_f, 3 * n_f)])
    return np.concatenate([g * DA_DQK + base for g in range(BRANCH_W // DA_DQK)])


def _prep_mix_weights(w):
    na, ml, ml_if = w[:, 0:768], w[:, 768:1792], w[:, 1792:1808]
    gla, gla_a = w[:, 1808:2576], w[:, 2576:2608]
    dq, dk, dv = w[:, 2608:2864], w[:, 2864:3120], w[:, 3120:3376]
    gates = w[:, 3376:]
    perm = _swap_perm()
    pad = jnp.zeros((D_MODEL, 128 - ml_if.shape[1] - gla_a.shape[1]), w.dtype)
    w_all = jnp.concatenate([na, ml, gla, ml_if, gla_a, pad, dq, dq[:, perm], dk, dk[:, perm], dv], axis=1)
    return w_all.astype(BF16), gates.astype(BF16)


def _col_scale():
    cs = np.ones((1, C_SMALL), np.float32)
    cs[:, C_NA:C_NA + 256] = 64 ** -0.5 * LOG2E
    cs[:, C_ML:C_ML + 256] = 64 ** -0.5
    cs[:, C_GLA:C_GLA + 128] = GLA_DK ** -0.5
    return jnp.asarray(cs)


def _group_mean_matrix():
    head = np.arange(BRANCH_W) // (BRANCH_W // N_HEADS)
    return jnp.asarray((head[:, None] == head[None, :]) / (BRANCH_W // N_HEADS), dtype=BF16)


def _da_group_sum_matrix():
    grp = np.arange(BRANCH_W) // DA_DQK
    return jnp.asarray(grp[:, None] == grp[None, :], dtype=BF16)


def kernel(x, c, ctx, c_ctx, w_ada, b_ada, ln_g, ln_b, ffn_w_in, ffn_w_out, w_mix_in, na_rpb, ml_gate_b,
           ml_norm_g, gla_w_a2, gla_b_a, gla_norm_g, da_lambda, da_norm_g, w_branch, w_out):
    n_batch = x.shape[0]
    depth = w_ada.shape[0]
    alpha = (2 * depth) ** 0.25
    mod_rows = -(-(n_batch + 1) // 8) * 8
    cc = jnp.concatenate([c, c_ctx[None, :], jnp.zeros((mod_rows - n_batch - 1, D_MODEL), F32)], axis=0)
    mod = _ada(cc, w_ada, b_ada)

    xs = x
    w_in_bf = ffn_w_in.astype(BF16)
    w_out_bf = ffn_w_out.astype(BF16)
    w_branch_bf = w_branch.astype(BF16)
    w_o_bf = w_out.astype(BF16)
    ln_g3 = ln_g.reshape(depth * 3, 1, D_MODEL)
    ln_b3 = ln_b.reshape(depth * 3, 1, D_MODEL)
    cos_tab, sin_tab = _rope_tables()
    colscale = _col_scale()
    gmat = _group_mean_matrix()
    da_gsum = _da_group_sum_matrix()

    for l in range(depth):
        ctx_out = l < depth - 1
        lambda_init = 0.8 - 0.6 * math.exp(-0.3 * l)
        w_all, w_gates = _prep_mix_weights(w_mix_in[l])
        bias = _na_bias_tables(na_rpb[l])
        gate_b = jnp.concatenate([ml_gate_b[l], jnp.zeros((128 - ml_gate_b.shape[1],), F32)])[None, :]
        w2pad = jnp.zeros((2, 128, 128), F32)
        for d in range(2):
            lo = SMALL_A_OFF + d * GLA_RANK
            w2pad = w2pad.at[d, lo:lo + GLA_RANK, :].set(gla_w_a2[l, d])
        ba = gla_b_a[l][:, None, :]
        norm_g = jnp.concatenate([ml_norm_g[l][None], gla_norm_g[l][None], da_norm_g[l][None],
                                  jnp.zeros((5, BRANCH_W), F32)], axis=0)

        xs = _ffn(xs, mod, w_in_bf, w_out_bf, ln_g3, ln_b3, layer=l, sub=0, mod0=0,
                  n_rows=NTOK, tm=768, alpha=alpha, ctx=ctx if l == 0 else None)
        na, ml, gla, small, da = _inproj(xs, mod, w_all, colscale, cos_tab, sin_tab, layer=l, tm=768)
        na_o = _neighbourhood(na, bias, layer=l, with_ctx=ctx_out)
        da_o = _diff_attention(da, da_lambda, da_gsum, layer=l, lambda_init=lambda_init, with_ctx=ctx_out)
        ml_o = _mlstm(ml, small, gate_b, layer=l)
        gla_o = _gla(gla, small, w2pad, ba, layer=l)
        n_rows, tm = (NTOK, 768) if ctx_out else (SEQ, 512)
        xs = _merge(xs, mod, na_o, ml_o, ml, gla_o, gla, da_o, norm_g, gmat, w_gates, w_branch_bf, w_o_bf,
                    ln_g3, ln_b3, layer=l, n_rows=n_rows, tm=tm, alpha=alpha, lambda_init=lambda_init)
        xs = _ffn(xs, mod, w_in_bf, w_out_bf, ln_g3, ln_b3, layer=l, sub=1, mod0=6,
                  n_rows=n_rows, tm=tm, alpha=alpha)
    return xs
```

```python
import functools
import math

import jax
import jax.numpy as jnp
import numpy as np
from jax import lax
from jax.experimental import pallas as pl
from jax.experimental.pallas import tpu as pltpu

F32 = jnp.float32
BF16 = jnp.bfloat16

D_MODEL = 1024
SEQ = 2048
CTX_LEN = 256
NTOK = SEQ + CTX_LEN
GRID_W = 64
GRID_ROWS = SEQ // GRID_W
N_MOD = 9
D_FF = 2816
FFN_HALF = 0.5
N_BRANCH = 4
BRANCH_W = 256
N_HEADS = 4
NA_KH = 8
NA_KW = 16
ML_F_OFF = 4
GLA_DK = 32
GLA_RANK = 16
GLA_TAU = 16.0
DA_DQK = 32
ROPE_BASE = 10000.0
LN_EPS = 1e-5
HEAD_EPS = 1e-6
NEG = -1e30
LOG2E = math.log2(math.e)
DA_SAFE_LOG2 = 100.0
DA_BOUND_SLACK = 1.1

V7X_VMEM_LIMIT = 56 * 1024 * 1024
FF_CHUNK = 256
BLK = 256
N_BLK = NTOK // BLK
N_LAT_BLK = SEQ // BLK
NA_BAND_ROWS = 12
NA_BAND = NA_BAND_ROWS * GRID_W
NA_QROWS = BLK // GRID_W

C_NA = 0
C_ML = 768
C_GLA = 1792
C_SMALL = 2560
C_DA = 2688
N_PROJ = 3968
SMALL_A_OFF = 16


def _dot(a, b):
    return jnp.dot(a, b, preferred_element_type=F32)


def _dot_nt(a, b):
    return lax.dot_general(a, b, (((1,), (1,)), ((), ())), preferred_element_type=F32)


def _dot_tn(a, b):
    return lax.dot_general(a, b, (((0,), (0,)), ((), ())), preferred_element_type=F32)


def _split_f32(x, n):
    parts = []
    r = x
    for i in range(n):
        p = r.astype(BF16).astype(F32)
        parts.append(p)
        if i + 1 < n:
            r = r - p
    return parts


def _split(x, n):
    return [p.astype(BF16) for p in _split_f32(x, n)]


def _dot_split(a, b, na, nb, dot=_dot):
    pa = _split(a, na) if na > 1 else [a.astype(BF16)]
    pb = _split(b, nb) if nb > 1 else [b.astype(BF16)]
    acc = None
    for i, x in enumerate(pa):
        for j, y in enumerate(pb):
            if i + j < max(na, nb):
                t = dot(x, y)
                acc = t if acc is None else acc + t
    return acc


def _layer_norm(y, g, b):
    mu = jnp.mean(y, axis=-1, keepdims=True)
    yc = y - mu
    var = jnp.mean(yc * yc, axis=-1, keepdims=True)
    return yc * lax.rsqrt(var + LN_EPS) * g + b


def _mod_rows(ref, b, n_batch, is_ctx):
    lat = ref[0, pl.ds(b, 1), :]
    ctx = ref[0, n_batch:n_batch + 1, :]
    return jnp.where(is_ctx, ctx, lat)


def _is_ctx_rows(t, tm):
    return (t * tm + lax.broadcasted_iota(jnp.int32, (tm, 1), 0)) >= SEQ


def _head_mask(width, head, n=1):
    lane = lax.broadcasted_iota(jnp.int32, (n, width), 1)
    per = width // N_HEADS
    return (lane >= head * per) & (lane < (head + 1) * per)


def _stack_heads(x):
    w = x.shape[-1]
    return jnp.concatenate([jnp.where(_head_mask(w, h), x, jnp.zeros_like(x)) for h in range(N_HEADS)], axis=0)


def _batch_group(n_batch):
    return 4 if n_batch % 4 == 0 else 2 if n_batch % 2 == 0 else 1


def _unstack_heads(x, m):
    w = x.shape[-1]
    out = jnp.zeros((m, w), x.dtype)
    for h in range(N_HEADS):
        out = jnp.where(_head_mask(w, h), x[h * m:(h + 1) * m], out)
    return out


def _ada_kernel(c_ref, w_ref, b_ref, o_ref):
    c = c_ref[...]
    s = c * jax.nn.sigmoid(c)
    o_ref[0] = _dot_split(s, w_ref[0], 2, 2) + b_ref[0]


def _ada(cc, w_ada, b_ada):
    depth = w_ada.shape[0]
    rows = cc.shape[0]
    return pl.pallas_call(
        _ada_kernel,
        grid=(depth, N_MOD),
        in_specs=[
            pl.BlockSpec((rows, D_MODEL), lambda l, k: (0, 0)),
            pl.BlockSpec((1, D_MODEL, D_MODEL), lambda l, k: (l, 0, k)),
            pl.BlockSpec((1, 1, D_MODEL), lambda l, k: (l, 0, k)),
        ],
        out_specs=pl.BlockSpec((1, rows, D_MODEL), lambda l, k: (l, 0, k)),
        out_shape=jax.ShapeDtypeStruct((depth, rows, N_MOD * D_MODEL), F32),
        compiler_params=pltpu.CompilerParams(
            dimension_semantics=("arbitrary", "arbitrary"), vmem_limit_bytes=V7X_VMEM_LIMIT),
        name="ada_mod",
    )(cc, w_ada, b_ada.reshape(depth, 1, N_MOD * D_MODEL))


def _ffn_kernel(*refs, n_batch, tm, alpha, n_x):
    x_refs = refs[:n_x]
    sh_ref, sc_ref, gt_ref, w_in_ref, w_out_ref, lng_ref, lnb_ref, o_ref, g_scr = refs[n_x:]
    b = pl.program_id(0)
    t = pl.program_id(1)
    is_ctx = _is_ctx_rows(t, tm)
    if n_x == 1:
        x = x_refs[0][0]
    else:
        parts = [r[0] for r in x_refs[:-1]]
        parts[-1] = jnp.where(t == pl.num_programs(1) - 1, x_refs[-1][0], parts[-1])
        x = jnp.concatenate(parts, axis=0)
    h = (x * (1.0 + _mod_rows(sc_ref, b, n_batch, is_ctx)) + _mod_rows(sh_ref, b, n_batch, is_ctx)).astype(BF16)
    for j in range(D_FF // FF_CHUNK):
        lo = j * FF_CHUNK
        a = _dot(h, w_in_ref[:, lo:lo + FF_CHUNK])
        v = _dot(h, w_in_ref[:, D_FF + lo:D_FF + lo + FF_CHUNK])
        g_scr[:, lo:lo + FF_CHUNK] = (a * jax.nn.sigmoid(a) * v).astype(BF16)
    y = _dot(g_scr[...], w_out_ref[...])
    y = alpha * x + FFN_HALF * _mod_rows(gt_ref, b, n_batch, is_ctx) * y
    o_ref[0] = _layer_norm(y, lng_ref[0], lnb_ref[0])


def _const_spec(shape, index_map):
    return pl.BlockSpec(shape, index_map, pipeline_mode=pl.Buffered(1))


def _mod_spec(rows, layer, k):
    return pl.BlockSpec((1, rows, D_MODEL), lambda b, t: (layer, 0, k))


def _ffn(xs, mod, w_in, w_out, ln_g, ln_b, *, layer, sub, mod0, n_rows, tm, alpha, ctx=None):
    n_batch = xs.shape[0]
    rows = mod.shape[1]
    ln_idx = layer * 3 + (0 if sub == 0 else 2)
    if ctx is None:
        x_args = [xs]
        x_specs = [pl.BlockSpec((1, tm, D_MODEL), lambda b, t: (b, t, 0))]
    else:
        per = tm // BLK
        x_args = [xs] * per + [ctx]
        x_specs = [pl.BlockSpec((1, BLK, D_MODEL), lambda b, t, j=j: (b, jnp.minimum(t * per + j, N_LAT_BLK - 1), 0))
                   for j in range(per)]
        x_specs.append(pl.BlockSpec((1, CTX_LEN, D_MODEL), lambda b, t: (b, 0, 0)))
    kern = functools.partial(_ffn_kernel, n_batch=n_batch, tm=tm, alpha=alpha, n_x=len(x_args))
    return pl.pallas_call(
        kern,
        grid=(n_batch, n_rows // tm),
        in_specs=x_specs + [
            _mod_spec(rows, layer, mod0), _mod_spec(rows, layer, mod0 + 1), _mod_spec(rows, layer, mod0 + 2),
            _const_spec((None, None, D_MODEL, 2 * D_FF), lambda b, t: (layer, sub, 0, 0)),
            _const_spec((None, None, D_FF, D_MODEL), lambda b, t: (layer, sub, 0, 0)),
            pl.BlockSpec((1, 1, D_MODEL), lambda b, t: (ln_idx, 0, 0)),
            pl.BlockSpec((1, 1, D_MODEL), lambda b, t: (ln_idx, 0, 0)),
        ],
        out_specs=pl.BlockSpec((1, tm, D_MODEL), lambda b, t: (b, t, 0)),
        out_shape=jax.ShapeDtypeStruct((n_batch, n_rows, D_MODEL), F32),
        scratch_shapes=[pltpu.VMEM((tm, D_FF), BF16)],
        compiler_params=pltpu.CompilerParams(
            dimension_semantics=("arbitrary", "arbitrary"), vmem_limit_bytes=V7X_VMEM_LIMIT),
        name=f"ffn_l{layer}_s{sub}",
    )(*x_args, mod, mod, mod, w_in, w_out, ln_g, ln_b)


def _inproj_kernel(x_ref, sh_ref, sc_ref, w_ref, cs_ref, cos_ref, sin_ref,
                   na_ref, ml_ref, gla_ref, sm_ref, da_ref, dvt_ref, *, n_batch, tm):
    b = pl.program_id(0)
    t = pl.program_id(1)
    is_ctx = _is_ctx_rows(t, tm)
    x = x_ref[0]
    h = (x * (1.0 + _mod_rows(sc_ref, b, n_batch, is_ctx)) + _mod_rows(sh_ref, b, n_batch, is_ctx)).astype(BF16)
    na_ref[0] = (_dot(h, w_ref[:, C_NA:C_ML]) * cs_ref[:, C_NA:C_ML]).astype(BF16)
    ml_ref[0] = _dot(h, w_ref[:, C_ML:C_GLA]) * cs_ref[:, C_ML:C_GLA]
    gla_ref[0] = _dot(h, w_ref[:, C_GLA:C_SMALL]) * cs_ref[:, C_GLA:C_SMALL]
    sm_ref[0] = _dot(h, w_ref[:, C_SMALL:C_DA])
    pd = _dot(h, w_ref[:, C_DA:N_PROJ])
    cos = cos_ref[...]
    sin = sin_ref[...]
    w = BRANCH_W
    da_ref[0, :, 0:w] = ((pd[:, 0:w] * cos + pd[:, w:2 * w] * sin) * (DA_DQK ** -0.5 * LOG2E)).astype(BF16)
    da_ref[0, :, w:2 * w] = (pd[:, 2 * w:3 * w] * cos + pd[:, 3 * w:4 * w] * sin).astype(BF16)
    dvt_ref[0] = pd[:, 4 * w:5 * w].T.astype(BF16)


def _inproj(xs, mod, w_all, colscale, cos_tab, sin_tab, *, layer, tm):
    n_batch = xs.shape[0]
    rows = mod.shape[1]
    kern = functools.partial(_inproj_kernel, n_batch=n_batch, tm=tm)

    def out(width, dtype):
        return (pl.BlockSpec((1, tm, width), lambda b, t: (b, t, 0)),
                jax.ShapeDtypeStruct((n_batch, NTOK, width), dtype))

    outs = [out(768, BF16), out(1024, F32), out(768, F32), out(128, F32), out(512, BF16),
            (pl.BlockSpec((1, BRANCH_W, tm), lambda b, t: (b, 0, t)),
             jax.ShapeDtypeStruct((n_batch, BRANCH_W, NTOK), BF16))]
    return pl.pallas_call(
        kern,
        grid=(n_batch, NTOK // tm),
        in_specs=[
            pl.BlockSpec((1, tm, D_MODEL), lambda b, t: (b, t, 0)),
            _mod_spec(rows, layer, 3), _mod_spec(rows, layer, 4),
            _const_spec((D_MODEL, N_PROJ), lambda b, t: (0, 0)),
            pl.BlockSpec((1, C_SMALL), lambda b, t: (0, 0)),
            pl.BlockSpec((tm, BRANCH_W), lambda b, t: (t, 0)),
            pl.BlockSpec((tm, BRANCH_W), lambda b, t: (t, 0)),
        ],
        out_specs=[o[0] for o in outs],
        out_shape=[o[1] for o in outs],
        compiler_params=pltpu.CompilerParams(
            dimension_semantics=("arbitrary", "arbitrary"), vmem_limit_bytes=V7X_VMEM_LIMIT),
        name=f"inproj_l{layer}",
    )(xs, mod, mod, w_all, colscale, cos_tab, sin_tab)


def _softmax_pv(parts, vals):
    m = None
    for s in parts:
        mi = jnp.max(s, axis=-1, keepdims=True)
        m = mi if m is None else jnp.maximum(m, mi)
    l = None
    o = None
    for s, v in zip(parts, vals):
        p = jnp.exp2(s - m)
        li = jnp.sum(p, axis=-1, keepdims=True)
        oi = _dot(p.astype(BF16), v)
        l = li if l is None else l + li
        o = oi if o is None else o + oi
    return o / l


def _na_kernel(q_ref, k_ref, v_ref, bias_ref, o_ref, *, with_ctx, group):
    i = pl.program_id(0)

    @pl.when(i < N_LAT_BLK)
    def _():
        start = pl.multiple_of(jnp.clip(i - 1, 0, N_LAT_BLK - NA_BAND // BLK) * BLK, BLK)
        bias = bias_ref[...].reshape(N_HEADS * BLK, NA_BAND)
        for e in range(group):
            qs = _stack_heads(q_ref[e])
            kb = k_ref[e, pl.ds(start, NA_BAND), :]
            vb = v_ref[e, pl.ds(start, NA_BAND), :]
            s_loc = _dot_nt(qs, kb) + bias
            s_ctx = _dot_nt(qs, k_ref[e, SEQ:NTOK, :])
            o = _softmax_pv([s_loc, s_ctx], [vb, v_ref[e, SEQ:NTOK, :]])
            o_ref[e] = _unstack_heads(o, BLK).astype(BF16)

    if with_ctx:
        @pl.when(i == N_LAT_BLK)
        def _():
            for e in range(group):
                qs = _stack_heads(q_ref[e])
                vc = v_ref[e, SEQ:NTOK, :]
                o = _softmax_pv([_dot_nt(qs, k_ref[e, SEQ:NTOK, :])], [vc])
                o_ref[e] = _unstack_heads(o, BLK).astype(BF16)


def _na_pattern(i):
    return jnp.minimum(i, 1) + (i >= N_LAT_BLK - 2).astype(jnp.int32) + (i >= N_LAT_BLK - 1).astype(jnp.int32)


def _neighbourhood(na, bias, *, layer, with_ctx):
    n_batch = na.shape[0]
    n_blk = N_BLK if with_ctx else N_LAT_BLK
    group = _batch_group(n_batch)
    return pl.pallas_call(
        functools.partial(_na_kernel, with_ctx=with_ctx, group=group),
        grid=(n_blk, n_batch // group),
        in_specs=[
            pl.BlockSpec((group, BLK, BRANCH_W), lambda i, b: (b, i, 0)),
            pl.BlockSpec((group, NTOK, BRANCH_W), lambda i, b: (b, 0, 1)),
            pl.BlockSpec((group, NTOK, BRANCH_W), lambda i, b: (b, 0, 2)),
            pl.BlockSpec((N_HEADS, None, BLK, NA_BAND),
                         lambda i, b: (0, _na_pattern(jnp.minimum(i, N_LAT_BLK - 1)), 0, 0)),
        ],
        out_specs=pl.BlockSpec((group, BLK, BRANCH_W), lambda i, b: (b, i, 0)),
        out_shape=jax.ShapeDtypeStruct((n_batch, n_blk * BLK, BRANCH_W), BF16),
        compiler_params=pltpu.CompilerParams(
            dimension_semantics=("arbitrary", "arbitrary"), vmem_limit_bytes=V7X_VMEM_LIMIT),
        name=f"na_l{layer}",
    )(na, na, na, bias)


def _na_bias_tables(rpb):
    blocks = np.array([0, 1, N_LAT_BLK - 2, N_LAT_BLK - 1])
    qr = blocks[:, None] * NA_QROWS + np.arange(NA_QROWS)[None, :]
    r0 = np.clip(qr - NA_KH // 2, 0, GRID_ROWS - NA_KH)
    band0 = np.clip(blocks - 1, 0, N_LAT_BLK - NA_BAND // BLK) * NA_QROWS
    kr = band0[:, None] + np.arange(NA_BAND_ROWS)[None, :]
    row_ok = (kr[:, None, :] >= r0[:, :, None]) & (kr[:, None, :] < r0[:, :, None] + NA_KH)
    dr = np.clip(kr[:, None, :] - qr[:, :, None] + NA_KH - 1, 0, 2 * NA_KH - 2)
    col = np.arange(GRID_W)
    c0 = np.clip(col - NA_KW // 2, 0, GRID_W - NA_KW)
    col_ok = (col[None, :] >= c0[:, None]) & (col[None, :] < c0[:, None] + NA_KW)
    dc = np.clip(col[None, :] - col[:, None] + NA_KW - 1, 0, 2 * NA_KW - 2)
    ok = row_ok[:, :, None, :, None] & col_ok[None, None, :, None, :]
    oh_c = jnp.asarray(dc[:, :, None] == np.arange(2 * NA_KW - 1), dtype=F32)
    oh_r = jnp.asarray(dr[..., None] == np.arange(2 * NA_KH - 1), dtype=F32)
    by_col = jnp.einsum("hrc,xyc->hrxy", rpb, oh_c, precision=lax.Precision.HIGHEST)
    vals = jnp.einsum("pqkr,hrxy->hpqxky", oh_r, by_col, precision=lax.Precision.HIGHEST)
    tab = jnp.where(ok[None], vals * LOG2E, NEG)
    return tab.reshape(N_HEADS, 4, BLK, NA_BAND)


def _da_kernel(q_ref, k_ref, vt_ref, lam_ref, g_ref, o_ref, kmax_scr, *, lambda_init, with_ctx):
    i = pl.program_id(1)
    lp = lam_ref[...]
    lam = (jnp.exp(jnp.sum(lp[0:1] * lp[1:2], axis=-1, keepdims=True))
           - jnp.exp(jnp.sum(lp[2:3] * lp[3:4], axis=-1, keepdims=True)) + lambda_init)
    q = q_ref[0]
    lane = lax.broadcasted_iota(jnp.int32, (1, BRANCH_W), 1)

    @pl.when(i == 0)
    def _():
        kf = k_ref[0].astype(F32)
        kn2 = _dot((kf * kf).astype(BF16), g_ref[...])
        kmax_scr[...] = jnp.broadcast_to(jnp.max(kn2, axis=0, keepdims=True), kmax_scr.shape)

    qf = q.astype(F32)
    qn2 = _dot((qf * qf).astype(BF16), g_ref[...])
    bound2 = jnp.max(qn2 * kmax_scr[0:1, :]) * DA_BOUND_SLACK
    safe = bound2 <= DA_SAFE_LOG2 ** 2

    def attend(k, vt, stabilise):
        dv = BRANCH_W // N_HEADS
        outs = []
        for h in range(N_HEADS):
            lo = h * 2 * DA_DQK
            q1 = jnp.where((lane >= lo) & (lane < lo + DA_DQK), q, jnp.zeros_like(q))
            q2 = jnp.where((lane >= lo + DA_DQK) & (lane < lo + 2 * DA_DQK), q, jnp.zeros_like(q))
            st = _dot_nt(k, jnp.concatenate([q1, q2], axis=0))
            if stabilise:
                st = st - jnp.max(st, axis=0, keepdims=True)
            pt = jnp.exp2(st)
            r = 1.0 / jnp.sum(pt, axis=0, keepdims=True)
            ot = _dot(vt[h * dv:(h + 1) * dv, :], pt.astype(BF16))
            outs.append(ot[:, :BLK] * r[:, :BLK] - ot[:, BLK:] * (lam * r[:, BLK:]))
        return jnp.concatenate(outs, axis=0).T

    def both(k_fn, vt_fn):
        @pl.when(safe)
        def _():
            o_ref[0] = attend(k_fn(), vt_fn(), False)

        @pl.when(jnp.logical_not(safe))
        def _():
            o_ref[0] = attend(k_fn(), vt_fn(), True)

    @pl.when(i < N_LAT_BLK)
    def _():
        both(lambda: k_ref[0], lambda: vt_ref[0])

    if with_ctx:
        @pl.when(i == N_LAT_BLK)
        def _():
            both(lambda: k_ref[0, SEQ:NTOK, :], lambda: vt_ref[0, :, SEQ:NTOK])


def _diff_attention(da, da_vt, lam_p, gsum, *, layer, lambda_init, with_ctx):
    n_batch = da.shape[0]
    n_blk = N_BLK if with_ctx else N_LAT_BLK
    return pl.pallas_call(
        functools.partial(_da_kernel, lambda_init=lambda_init, with_ctx=with_ctx),
        grid=(n_batch, n_blk),
        in_specs=[
            pl.BlockSpec((1, BLK, BRANCH_W), lambda b, i: (b, i, 0)),
            pl.BlockSpec((1, NTOK, BRANCH_W), lambda b, i: (b, 0, 1)),
            pl.BlockSpec((1, BRANCH_W, NTOK), lambda b, i: (b, 0, 0)),
            pl.BlockSpec((None, 4, DA_DQK), lambda b, i: (layer, 0, 0)),
            pl.BlockSpec((BRANCH_W, BRANCH_W), lambda b, i: (0, 0)),
        ],
        out_specs=pl.BlockSpec((1, BLK, BRANCH_W), lambda b, i: (b, i, 0)),
        out_shape=jax.ShapeDtypeStruct((n_batch, n_blk * BLK, BRANCH_W), F32),
        scratch_shapes=[pltpu.VMEM((8, BRANCH_W), F32)],
        compiler_params=pltpu.CompilerParams(
            dimension_semantics=("arbitrary", "arbitrary"), vmem_limit_bytes=V7X_VMEM_LIMIT),
        name=f"da_l{layer}",
    )(da, da, da_vt, lam_p, gsum)


def _fwd_chunk(g):
    return jnp.where(g == 0, N_LAT_BLK, g - 1)


def _bwd_chunk(g):
    return jnp.where(g == 0, N_LAT_BLK, N_LAT_BLK - g)


def _tri_mask(reverse):
    r = lax.broadcasted_iota(jnp.int32, (BLK, BLK), 0)
    c = lax.broadcasted_iota(jnp.int32, (BLK, BLK), 1)
    return (c >= r) if reverse else (c <= r)


def _cumsum_rows(x, mask):
    tri = jnp.where(mask, 1.0, 0.0).astype(BF16)
    return _dot_split(tri, x, 1, 3)


def _cummax_rows(x, reverse):
    n = x.shape[0]
    row = lax.broadcasted_iota(jnp.int32, (n, 1), 0)
    sh = 1
    while sh < n:
        if reverse:
            moved, ok = pltpu.roll(x, n - sh, axis=0), row < n - sh
        else:
            moved, ok = pltpu.roll(x, sh, axis=0), row >= sh
        x = jnp.maximum(x, jnp.where(ok, moved, NEG))
        sh *= 2
    return x


def _mlstm_chunk(e, d, q_ref, k_ref, v_ref, s_ref, gb_ref, ex_ref, o_ref, c_scr, n_scr, m_scr):
    reverse = d == 1
    mask = _tri_mask(reverse)
    end = 0 if reverse else BLK - 1
    st = 2 * e + d
    q = q_ref[e]
    k = k_ref[e]
    v = v_ref[e]
    qb = q.astype(BF16)
    kb = k.astype(BF16)
    vb = v.astype(BF16)
    il = 2 * N_HEADS * d
    lane = lax.broadcasted_iota(jnp.int32, (1, 128), 1)
    sel = (lane >= il) & (lane < il + N_HEADS)
    is_f = (lane >= il + ML_F_OFF) & (lane < il + ML_F_OFF + N_HEADS)
    gs = s_ref[e] + gb_ref[...]
    gl = jnp.where(is_f, jax.nn.log_sigmoid(gs), gs)
    bc = _cumsum_rows(gl, mask)
    b_al = pltpu.roll(bc, 128 - ML_F_OFF, axis=1)
    b_al = jnp.where(sel, b_al, 0.0)
    i_g = jnp.where(sel, gl, 0.0)
    u = i_g - b_al
    m_prev = m_scr[st, 0:1, :]
    c = jnp.maximum(_cummax_rows(u, reverse), m_prev)
    c3 = _split_f32(c * LOG2E, 3)
    u3 = _split_f32(u * LOG2E, 3)
    pc = c3[0] + pltpu.roll(c3[1], 16, axis=1) + pltpu.roll(c3[2], 32, axis=1)
    pu = pltpu.roll(u3[0], 48, axis=1) + pltpu.roll(u3[1], 64, axis=1) + pltpu.roll(u3[2], 80, axis=1)
    y = jnp.where(lane < 48, 1.0, pu).astype(BF16)
    xs = []
    for h in range(N_HEADS):
        l = il + h
        neg_c = (lane == l) | (lane == l + 16) | (lane == l + 32)
        ones = (lane == l + 48) | (lane == l + 64) | (lane == l + 80)
        xs.append(jnp.where(neg_c, -pc, jnp.where(ones, 1.0, 0.0)))
    e_all = _dot_nt(jnp.concatenate(xs, axis=0).astype(BF16), y)
    mask4 = jnp.concatenate([mask] * N_HEADS, axis=0)
    w_all = jnp.exp2(jnp.where(mask4, e_all, NEG))
    qk = (_dot_nt(_stack_heads(qb), kb) * w_all).astype(BF16)
    num = _unstack_heads(_dot(qk, vb), BLK)
    den = _unstack_heads(_dot(qk, jnp.ones((BLK, BRANCH_W), BF16)), BLK)
    ex = ex_ref[d]
    args = jnp.concatenate([jnp.where(sel, m_prev - c, 0.0), -(b_al + c)], axis=0)
    wide = jnp.exp(_dot_split(args, ex, 3, 1))
    w_inter = wide[:BLK]
    exp_neg_m = wide[BLK:]
    q_c = _dot(qb, c_scr[st].astype(BF16))
    q_n = _dot(qb, n_scr[st].astype(BF16))
    o_ref[e] = (num + w_inter * q_c) / jnp.maximum(jnp.abs(den + w_inter * q_n), exp_neg_m)
    b_end = b_al[end:end + 1, :]
    k_log = b_end - b_al + i_g
    m_new = jnp.maximum(b_end + m_prev, jnp.max(k_log, axis=0, keepdims=True))
    wk = jnp.where(sel, jnp.exp(k_log - m_new), 0.0)
    decay = jnp.where(sel, jnp.exp(b_end + m_prev - m_new), 0.0)
    kw = (k * _dot_split(wk, ex, 2, 1)).astype(BF16)
    decay_l = _dot_split(jnp.broadcast_to(decay, (8, 128)), ex, 3, 1)[0:1]
    r_head = lax.broadcasted_iota(jnp.int32, (BRANCH_W, BRANCH_W), 0) // (BRANCH_W // N_HEADS)
    c_head = lax.broadcasted_iota(jnp.int32, (BRANCH_W, BRANCH_W), 1) // (BRANCH_W // N_HEADS)
    diag = r_head == c_head
    c_scr[st] = decay_l * c_scr[st] + jnp.where(diag, _dot_tn(kw, vb), 0.0)
    n_sum = _dot_tn(kw, jnp.ones((BLK, BRANCH_W), BF16))
    n_scr[st] = decay_l * n_scr[st] + jnp.where(diag, n_sum, 0.0)
    m_scr[st, 0:1, :] = jnp.where(sel, m_new, 0.0)


def _mlstm_kernel(qf, kf, vf, sf, qb, kb, vb, sb, gb_ref, ex_ref, o_ref_f, o_ref_b, c_scr, n_scr, m_scr, *, group):
    @pl.when(pl.program_id(1) == 0)
    def _():
        c_scr[...] = jnp.zeros_like(c_scr)
        n_scr[...] = jnp.zeros_like(n_scr)
        m_scr[...] = jnp.zeros_like(m_scr)

    for e in range(group):
        _mlstm_chunk(e, 0, qf, kf, vf, sf, gb_ref, ex_ref, o_ref_f, c_scr, n_scr, m_scr)
        _mlstm_chunk(e, 1, qb, kb, vb, sb, gb_ref, ex_ref, o_ref_b, c_scr, n_scr, m_scr)


def _scan_specs(group, width_blocks, chunk_fn):
    return [pl.BlockSpec((group, BLK, BRANCH_W), lambda b, g, j=j: (b, chunk_fn(g), j)) for j in width_blocks]


def _mlstm_expand_matrix():
    ex = np.zeros((2, 128, BRANCH_W), np.float32)
    per = BRANCH_W // N_HEADS
    for d in range(2):
        for h in range(N_HEADS):
            ex[d, 2 * N_HEADS * d + h, h * per:(h + 1) * per] = 1.0
    return jnp.asarray(ex, dtype=BF16)


def _mlstm(ml, small, gate_b, *, layer):
    n_batch = ml.shape[0]
    group = 2 if n_batch % 2 == 0 else 1
    small_spec = lambda fn: pl.BlockSpec((group, BLK, 128), lambda b, g: (b, fn(g), 0))
    out_spec = lambda fn: pl.BlockSpec((group, BLK, BRANCH_W), lambda b, g: (b, fn(g), 0))
    return pl.pallas_call(
        functools.partial(_mlstm_kernel, group=group),
        grid=(n_batch // group, N_BLK),
        in_specs=(_scan_specs(group, (0, 1, 2), _fwd_chunk) + [small_spec(_fwd_chunk)]
                  + _scan_specs(group, (0, 1, 2), _bwd_chunk) + [small_spec(_bwd_chunk)]
                  + [pl.BlockSpec((1, 128), lambda b, g: (0, 0)),
                     pl.BlockSpec((2, 128, BRANCH_W), lambda b, g: (0, 0, 0))]),
        out_specs=[out_spec(_fwd_chunk), out_spec(_bwd_chunk)],
        out_shape=[jax.ShapeDtypeStruct((n_batch, NTOK, BRANCH_W), F32)] * 2,
        scratch_shapes=[pltpu.VMEM((2 * group, BRANCH_W, BRANCH_W), F32),
                        pltpu.VMEM((2 * group, BRANCH_W, BRANCH_W), F32),
                        pltpu.VMEM((2 * group, 8, 128), F32)],
        compiler_params=pltpu.CompilerParams(
            dimension_semantics=("arbitrary", "arbitrary"), vmem_limit_bytes=V7X_VMEM_LIMIT),
        name=f"mlstm_l{layer}",
    )(ml, ml, ml, small, ml, ml, ml, small, gate_b, _mlstm_expand_matrix())


def _gla_chunk(e, d, qk_ref, v_ref, s_ref, w2_ref, ba_ref, o_ref, st_scr):
    reverse = d == 1
    mask = _tri_mask(reverse)
    end = 0 if reverse else BLK - 1
    half = BLK // 2
    slot = 2 * e + d
    q = qk_ref[e, :, 0:128]
    k = qk_ref[e, :, 128:256]
    vb = v_ref[e].astype(BF16)
    x = _dot_split(s_ref[e], w2_ref[d], 2, 2) + ba_ref[d]
    la = jax.nn.log_sigmoid(x) * (1.0 / GLA_TAU)
    bc = _cumsum_rows(la, mask)
    ref_row = bc[half:half + 1, :]
    qe = (q * jnp.exp(bc - ref_row)).astype(BF16)
    ke = (k * jnp.exp(ref_row - bc)).astype(BF16)
    att = _dot_nt(_stack_heads(qe), ke)
    mask4 = jnp.concatenate([mask] * N_HEADS, axis=0)
    o_all = _dot(jnp.where(mask4, att, 0.0).astype(BF16), vb)
    st = st_scr[slot]
    inter = _dot_nt((q * jnp.exp(bc)).astype(BF16), st.astype(BF16))
    o_ref[e] = _unstack_heads(o_all, BLK) + inter
    b_end = bc[end:end + 1, :]
    kend = (k * jnp.exp(b_end - bc)).astype(BF16)
    r_head = lax.broadcasted_iota(jnp.int32, (BRANCH_W, 128), 0) // (BRANCH_W // N_HEADS)
    c_head = lax.broadcasted_iota(jnp.int32, (BRANCH_W, 128), 1) // GLA_DK
    st_scr[slot] = jnp.exp(b_end) * st + jnp.where(r_head == c_head, _dot_tn(vb, kend), 0.0)


def _gla_kernel(qkf, vf, sf, qkb, vb, sb, w2_ref, ba_ref, o_ref_f, o_ref_b, st_scr, *, group):
    @pl.when(pl.program_id(1) == 0)
    def _():
        st_scr[...] = jnp.zeros_like(st_scr)

    for e in range(group):
        _gla_chunk(e, 0, qkf, vf, sf, w2_ref, ba_ref, o_ref_f, st_scr)
        _gla_chunk(e, 1, qkb, vb, sb, w2_ref, ba_ref, o_ref_b, st_scr)


def _gla(gla, small, w2pad, ba, *, layer):
    n_batch = gla.shape[0]
    group = _batch_group(n_batch)
    small_spec = lambda fn: pl.BlockSpec((group, BLK, 128), lambda b, g: (b, fn(g), 0))
    out_spec = lambda fn: pl.BlockSpec((group, BLK, BRANCH_W), lambda b, g: (b, fn(g), 0))
    return pl.pallas_call(
        functools.partial(_gla_kernel, group=group),
        grid=(n_batch // group, N_BLK),
        in_specs=(_scan_specs(group, (0, 1), _fwd_chunk) + [small_spec(_fwd_chunk)]
                  + _scan_specs(group, (0, 1), _bwd_chunk) + [small_spec(_bwd_chunk)]
                  + [pl.BlockSpec((2, 128, 128), lambda b, g: (0, 0, 0)),
                     pl.BlockSpec((2, 1, 128), lambda b, g: (0, 0, 0))]),
        out_specs=[out_spec(_fwd_chunk), out_spec(_bwd_chunk)],
        out_shape=[jax.ShapeDtypeStruct((n_batch, NTOK, BRANCH_W), F32)] * 2,
        scratch_shapes=[pltpu.VMEM((2 * group, BRANCH_W, 128), F32)],
        compiler_params=pltpu.CompilerParams(
            dimension_semantics=("arbitrary", "arbitrary"), vmem_limit_bytes=V7X_VMEM_LIMIT),
        name=f"gla_l{layer}",
    )(gla, gla, small, gla, gla, small, w2pad, ba)


def _group_mean(x, gmat):
    return _dot_split(x, gmat, 2, 1)


def _merge_kernel(x_ref, sh_ref, sc_ref, gt_ref, na_ref, mlf_ref, mlb_ref, mlo_ref, glf_ref, glb_ref, glg_ref,
                  da_ref, ng_ref, gm_ref, wg_ref, wb_ref, wo_ref, lng_ref, lnb_ref, o_ref,
                  *, n_batch, tm, alpha, lambda_init):
    b = pl.program_id(0)
    t = pl.program_id(1)
    is_ctx = _is_ctx_rows(t, tm)
    x = x_ref[0]
    h = (x * (1.0 + _mod_rows(sc_ref, b, n_batch, is_ctx)) + _mod_rows(sh_ref, b, n_batch, is_ctx)).astype(BF16)
    gmat = gm_ref[...]

    def rms_norm(y, gain):
        return y * lax.rsqrt(_group_mean(y * y, gmat) + HEAD_EPS) * gain

    h_ml = mlf_ref[0] + mlb_ref[0]
    y_ml = jax.nn.sigmoid(mlo_ref[0]) * rms_norm(h_ml - _group_mean(h_ml, gmat), ng_ref[0:1, :])
    g_gl = glg_ref[0]
    y_gla = g_gl * jax.nn.sigmoid(g_gl) * rms_norm(glf_ref[0] + glb_ref[0], ng_ref[1:2, :])
    y_da = (1.0 - lambda_init) * rms_norm(da_ref[0], ng_ref[2:3, :])
    ys = [na_ref[0], y_ml.astype(BF16), y_gla.astype(BF16), y_da.astype(BF16)]
    acc = None
    for j in range(N_BRANCH):
        gate = jax.nn.sigmoid(_dot(h, wg_ref[:, j * D_MODEL:(j + 1) * D_MODEL]))
        term = gate * _dot(ys[j], wb_ref[j])
        acc = term if acc is None else acc + term
    y = _dot(acc.astype(BF16), wo_ref[...])
    y = alpha * x + _mod_rows(gt_ref, b, n_batch, is_ctx) * y
    o_ref[0] = _layer_norm(y, lng_ref[0], lnb_ref[0])


def _merge(xs, mod, na_o, ml_o, ml, gla_o, gla, da_o, norm_g, gmat, w_gates, w_branch, w_out, ln_g, ln_b,
           *, layer, n_rows, tm, alpha, lambda_init):
    n_batch = xs.shape[0]
    rows = mod.shape[1]
    ln_idx = layer * 3 + 1
    kern = functools.partial(_merge_kernel, n_batch=n_batch, tm=tm, alpha=alpha, lambda_init=lambda_init)
    tok = lambda j: pl.BlockSpec((1, tm, BRANCH_W), lambda b, t: (b, t, j))
    return pl.pallas_call(
        kern,
        grid=(n_batch, n_rows // tm),
        in_specs=[
            pl.BlockSpec((1, tm, D_MODEL), lambda b, t: (b, t, 0)),
            _mod_spec(rows, layer, 3), _mod_spec(rows, layer, 4), _mod_spec(rows, layer, 5),
            tok(0), tok(0), tok(0), tok(3), tok(0), tok(0), tok(2), tok(0),
            pl.BlockSpec((8, BRANCH_W), lambda b, t: (0, 0)),
            pl.BlockSpec((BRANCH_W, BRANCH_W), lambda b, t: (0, 0)),
            _const_spec((D_MODEL, N_BRANCH * D_MODEL), lambda b, t: (0, 0)),
            _const_spec((None, N_BRANCH, BRANCH_W, D_MODEL), lambda b, t: (layer, 0, 0, 0)),
            _const_spec((None, D_MODEL, D_MODEL), lambda b, t: (layer, 0, 0)),
            pl.BlockSpec((1, 1, D_MODEL), lambda b, t: (ln_idx, 0, 0)),
            pl.BlockSpec((1, 1, D_MODEL), lambda b, t: (ln_idx, 0, 0)),
        ],
        out_specs=pl.BlockSpec((1, tm, D_MODEL), lambda b, t: (b, t, 0)),
        out_shape=jax.ShapeDtypeStruct((n_batch, n_rows, D_MODEL), F32),
        compiler_params=pltpu.CompilerParams(
            dimension_semantics=("arbitrary", "arbitrary"), vmem_limit_bytes=V7X_VMEM_LIMIT),
        name=f"merge_l{layer}",
    )(xs, mod, mod, mod, na_o, ml_o[0], ml_o[1], ml, gla_o[0], gla_o[1], gla, da_o, norm_g, gmat,
      w_gates, w_branch, w_out, ln_g, ln_b)


def _rope_tables():
    t = jnp.arange(SEQ)
    n_f = DA_DQK // 4
    inv = ROPE_BASE ** (-jnp.arange(n_f, dtype=F32) / n_f)

    def cs(pos):
        ang = pos.astype(F32)[:, None] * inv
        return jnp.cos(ang), jnp.sin(ang)

    (cr, sr), (cc, sc) = cs(t // GRID_W), cs(t % GRID_W)
    cos32 = jnp.concatenate([cr, cr, cc, cc], axis=-1)
    sin32 = jnp.concatenate([-sr, sr, -sc, sc], axis=-1)
    reps = BRANCH_W // DA_DQK
    cos = jnp.concatenate([jnp.tile(cos32, (1, reps)), jnp.ones((CTX_LEN, BRANCH_W), F32)], axis=0)
    sin = jnp.concatenate([jnp.tile(sin32, (1, reps)), jnp.zeros((CTX_LEN, BRANCH_W), F32)], axis=0)
    return cos, sin


def _swap_perm():
    n_f = DA_DQK // 4
    base = np.concatenate([np.arange(n_f, 2 * n_f), np.arange(0, n_f),
                           np.arange(3 * n_f, 4 * n_f), np.arange(2 * n_f, 3 * n_f)])
    return np.concatenate([g * DA_DQK + base for g in range(BRANCH_W // DA_DQK)])


def _prep_mix_weights(w):
    na, ml, ml_if = w[:, 0:768], w[:, 768:1792], w[:, 1792:1808]
    gla, gla_a = w[:, 1808:2576], w[:, 2576:2608]
    dq, dk, dv = w[:, 2608:2864], w[:, 2864:3120], w[:, 3120:3376]
    gates = w[:, 3376:]
    perm = _swap_perm()
    pad = jnp.zeros((D_MODEL, 128 - ml_if.shape[1] - gla_a.shape[1]), w.dtype)
    w_all = jnp.concatenate([na, ml, gla, ml_if, gla_a, pad, dq, dq[:, perm], dk, dk[:, perm], dv], axis=1)
    return w_all.astype(BF16), gates.astype(BF16)


def _col_scale():
    cs = np.ones((1, C_SMALL), np.float32)
    cs[:, C_NA:C_NA + 256] = 64 ** -0.5 * LOG2E
    cs[:, C_ML:C_ML + 256] = 64 ** -0.5
    cs[:, C_GLA:C_GLA + 128] = GLA_DK ** -0.5
    return jnp.asarray(cs)


def _group_mean_matrix():
    head = np.arange(BRANCH_W) // (BRANCH_W // N_HEADS)
    return jnp.asarray((head[:, None] == head[None, :]) / (BRANCH_W // N_HEADS), dtype=BF16)


def _da_group_sum_matrix():
    grp = np.arange(BRANCH_W) // DA_DQK
    return jnp.asarray(grp[:, None] == grp[None, :], dtype=BF16)


def kernel(x, c, ctx, c_ctx, w_ada, b_ada, ln_g, ln_b, ffn_w_in, ffn_w_out, w_mix_in, na_rpb, ml_gate_b,
           ml_norm_g, gla_w_a2, gla_b_a, gla_norm_g, da_lambda, da_norm_g, w_branch, w_out):
    n_batch = x.shape[0]
    depth = w_ada.shape[0]
    alpha = (2 * depth) ** 0.25
    mod_rows = -(-(n_batch + 1) // 8) * 8
    cc = jnp.concatenate([c, c_ctx[None, :], jnp.zeros((mod_rows - n_batch - 1, D_MODEL), F32)], axis=0)
    mod = _ada(cc, w_ada, b_ada)

    xs = x
    w_in_bf = ffn_w_in.astype(BF16)
    w_out_bf = ffn_w_out.astype(BF16)
    w_branch_bf = w_branch.astype(BF16)
    w_o_bf = w_out.astype(BF16)
    ln_g3 = ln_g.reshape(depth * 3, 1, D_MODEL)
    ln_b3 = ln_b.reshape(depth * 3, 1, D_MODEL)
    cos_tab, sin_tab = _rope_tables()
    colscale = _col_scale()
    gmat = _group_mean_matrix()
    da_gsum = _da_group_sum_matrix()

    for l in range(depth):
        ctx_out = l < depth - 1
        lambda_init = 0.8 - 0.6 * math.exp(-0.3 * l)
        w_all, w_gates = _prep_mix_weights(w_mix_in[l])
        bias = _na_bias_tables(na_rpb[l])
        gate_b = jnp.concatenate([ml_gate_b[l], jnp.zeros((128 - ml_gate_b.shape[1],), F32)])[None, :]
        w2pad = jnp.zeros((2, 128, 128), F32)
        for d in range(2):
            lo = SMALL_A_OFF + d * GLA_RANK
            w2pad = w2pad.at[d, lo:lo + GLA_RANK, :].set(gla_w_a2[l, d])
        ba = gla_b_a[l][:, None, :]
        norm_g = jnp.concatenate([ml_norm_g[l][None], gla_norm_g[l][None], da_norm_g[l][None],
                                  jnp.zeros((5, BRANCH_W), F32)], axis=0)

        xs = _ffn(xs, mod, w_in_bf, w_out_bf, ln_g3, ln_b3, layer=l, sub=0, mod0=0,
                  n_rows=NTOK, tm=768, alpha=alpha, ctx=ctx if l == 0 else None)
        na, ml, gla, small, da, da_vt = _inproj(xs, mod, w_all, colscale, cos_tab, sin_tab, layer=l, tm=768)
        na_o = _neighbourhood(na, bias, layer=l, with_ctx=ctx_out)
        da_o = _diff_attention(da, da_vt, da_lambda, da_gsum, layer=l, lambda_init=lambda_init,
                               with_ctx=ctx_out)
        ml_o = _mlstm(ml, small, gate_b, layer=l)
        gla_o = _gla(gla, small, w2pad, ba, layer=l)
        n_rows, tm = (NTOK, 768) if ctx_out else (SEQ, 512)
        xs = _merge(xs, mod, na_o, ml_o, ml, gla_o, gla, da_o, norm_g, gmat, w_gates, w_branch_bf, w_o_bf,
                    ln_g3, ln_b3, layer=l, n_rows=n_rows, tm=tm, alpha=alpha, lambda_init=lambda_init)
        xs = _ffn(xs, mod, w_in_bf, w_out_bf, ln_g3, ln_b3, layer=l, sub=1, mod0=6,
                  n_rows=n_rows, tm=tm, alpha=alpha)
    return xs
```

```python
import functools
import math

import jax
import jax.numpy as jnp
import numpy as np
from jax import lax
from jax.experimental import pallas as pl
from jax.experimental.pallas import tpu as pltpu

F32 = jnp.float32
BF16 = jnp.bfloat16

D_MODEL = 1024
SEQ = 2048
CTX_LEN = 256
NTOK = SEQ + CTX_LEN
GRID_W = 64
GRID_ROWS = SEQ // GRID_W
N_MOD = 9
D_FF = 2816
FFN_HALF = 0.5
N_BRANCH = 4
BRANCH_W = 256
N_HEADS = 4
NA_KH = 8
NA_KW = 16
ML_F_OFF = 4
GLA_DK = 32
GLA_RANK = 16
GLA_TAU = 16.0
DA_DQK = 32
ROPE_BASE = 10000.0
LN_EPS = 1e-5
HEAD_EPS = 1e-6
NEG = -1e30
LOG2E = math.log2(math.e)
DA_SAFE_LOG2 = 100.0
DA_BOUND_SLACK = 1.1

V7X_VMEM_LIMIT = 56 * 1024 * 1024
FF_CHUNK = 256
BLK = 256
N_BLK = NTOK // BLK
N_LAT_BLK = SEQ // BLK
NA_BAND_ROWS = 12
NA_BAND = NA_BAND_ROWS * GRID_W
NA_QROWS = BLK // GRID_W

C_NA = 0
C_ML = 768
C_GLA = 1792
C_SMALL = 2560
C_DA = 2688
N_PROJ = 3968
SMALL_A_OFF = 16


def _dot(a, b):
    return jnp.dot(a, b, preferred_element_type=F32)


def _dot_nt(a, b):
    return lax.dot_general(a, b, (((1,), (1,)), ((), ())), preferred_element_type=F32)


def _dot_tn(a, b):
    return lax.dot_general(a, b, (((0,), (0,)), ((), ())), preferred_element_type=F32)


def _split_f32(x, n):
    parts = []
    r = x
    for i in range(n):
        p = r.astype(BF16).astype(F32)
        parts.append(p)
        if i + 1 < n:
            r = r - p
    return parts


def _split(x, n):
    return [p.astype(BF16) for p in _split_f32(x, n)]


def _dot_split(a, b, na, nb, dot=_dot):
    pa = _split(a, na) if na > 1 else [a.astype(BF16)]
    pb = _split(b, nb) if nb > 1 else [b.astype(BF16)]
    acc = None
    for i, x in enumerate(pa):
        for j, y in enumerate(pb):
            if i + j < max(na, nb):
                t = dot(x, y)
                acc = t if acc is None else acc + t
    return acc


def _layer_norm(y, g, b):
    mu = jnp.mean(y, axis=-1, keepdims=True)
    yc = y - mu
    var = jnp.mean(yc * yc, axis=-1, keepdims=True)
    return yc * lax.rsqrt(var + LN_EPS) * g + b


def _mod_rows(ref, b, n_batch, is_ctx):
    lat = ref[0, pl.ds(b, 1), :]
    ctx = ref[0, n_batch:n_batch + 1, :]
    return jnp.where(is_ctx, ctx, lat)


def _is_ctx_rows(t, tm):
    return (t * tm + lax.broadcasted_iota(jnp.int32, (tm, 1), 0)) >= SEQ


def _head_mask(width, head, n=1):
    lane = lax.broadcasted_iota(jnp.int32, (n, width), 1)
    per = width // N_HEADS
    return (lane >= head * per) & (lane < (head + 1) * per)


def _stack_heads(x):
    w = x.shape[-1]
    return jnp.concatenate([jnp.where(_head_mask(w, h), x, jnp.zeros_like(x)) for h in range(N_HEADS)], axis=0)


def _batch_group(n_batch):
    return 4 if n_batch % 4 == 0 else 2 if n_batch % 2 == 0 else 1


def _unstack_heads(x, m):
    w = x.shape[-1]
    out = jnp.zeros((m, w), x.dtype)
    for h in range(N_HEADS):
        out = jnp.where(_head_mask(w, h), x[h * m:(h + 1) * m], out)
    return out


def _ada_kernel(c_ref, w_ref, b_ref, o_ref):
    c = c_ref[...]
    s = c * jax.nn.sigmoid(c)
    o_ref[0] = _dot_split(s, w_ref[0], 2, 2) + b_ref[0]


def _ada(cc, w_ada, b_ada):
    depth = w_ada.shape[0]
    rows = cc.shape[0]
    return pl.pallas_call(
        _ada_kernel,
        grid=(depth, N_MOD),
        in_specs=[
            pl.BlockSpec((rows, D_MODEL), lambda l, k: (0, 0)),
            pl.BlockSpec((1, D_MODEL, D_MODEL), lambda l, k: (l, 0, k)),
            pl.BlockSpec((1, 1, D_MODEL), lambda l, k: (l, 0, k)),
        ],
        out_specs=pl.BlockSpec((1, rows, D_MODEL), lambda l, k: (l, 0, k)),
        out_shape=jax.ShapeDtypeStruct((depth, rows, N_MOD * D_MODEL), F32),
        compiler_params=pltpu.CompilerParams(
            dimension_semantics=("arbitrary", "arbitrary"), vmem_limit_bytes=V7X_VMEM_LIMIT),
        name="ada_mod",
    )(cc, w_ada, b_ada.reshape(depth, 1, N_MOD * D_MODEL))


def _ffn_kernel(*refs, n_batch, tm, alpha, n_x):
    x_refs = refs[:n_x]
    sh_ref, sc_ref, gt_ref, w_in_ref, w_out_ref, lng_ref, lnb_ref, o_ref, g_scr = refs[n_x:]
    b = pl.program_id(0)
    t = pl.program_id(1)
    is_ctx = _is_ctx_rows(t, tm)
    if n_x == 1:
        x = x_refs[0][0]
    else:
        parts = [r[0] for r in x_refs[:-1]]
        parts[-1] = jnp.where(t == pl.num_programs(1) - 1, x_refs[-1][0], parts[-1])
        x = jnp.concatenate(parts, axis=0)
    h = (x * (1.0 + _mod_rows(sc_ref, b, n_batch, is_ctx)) + _mod_rows(sh_ref, b, n_batch, is_ctx)).astype(BF16)
    for j in range(D_FF // FF_CHUNK):
        lo = j * FF_CHUNK
        a = _dot(h, w_in_ref[:, lo:lo + FF_CHUNK])
        v = _dot(h, w_in_ref[:, D_FF + lo:D_FF + lo + FF_CHUNK])
        g_scr[:, lo:lo + FF_CHUNK] = (a * jax.nn.sigmoid(a) * v).astype(BF16)
    y = _dot(g_scr[...], w_out_ref[...])
    y = alpha * x + FFN_HALF * _mod_rows(gt_ref, b, n_batch, is_ctx) * y
    o_ref[0] = _layer_norm(y, lng_ref[0], lnb_ref[0])


def _const_spec(shape, index_map):
    return pl.BlockSpec(shape, index_map, pipeline_mode=pl.Buffered(1))


def _mod_spec(rows, layer, k):
    return pl.BlockSpec((1, rows, D_MODEL), lambda b, t: (layer, 0, k))


def _ffn(xs, mod, w_in, w_out, ln_g, ln_b, *, layer, sub, mod0, n_rows, tm, alpha, ctx=None):
    n_batch = xs.shape[0]
    rows = mod.shape[1]
    ln_idx = layer * 3 + (0 if sub == 0 else 2)
    if ctx is None:
        x_args = [xs]
        x_specs = [pl.BlockSpec((1, tm, D_MODEL), lambda b, t: (b, t, 0))]
    else:
        per = tm // BLK
        x_args = [xs] * per + [ctx]
        x_specs = [pl.BlockSpec((1, BLK, D_MODEL), lambda b, t, j=j: (b, jnp.minimum(t * per + j, N_LAT_BLK - 1), 0))
                   for j in range(per)]
        x_specs.append(pl.BlockSpec((1, CTX_LEN, D_MODEL), lambda b, t: (b, 0, 0)))
    kern = functools.partial(_ffn_kernel, n_batch=n_batch, tm=tm, alpha=alpha, n_x=len(x_args))
    return pl.pallas_call(
        kern,
        grid=(n_batch, n_rows // tm),
        in_specs=x_specs + [
            _mod_spec(rows, layer, mod0), _mod_spec(rows, layer, mod0 + 1), _mod_spec(rows, layer, mod0 + 2),
            _const_spec((None, None, D_MODEL, 2 * D_FF), lambda b, t: (layer, sub, 0, 0)),
            _const_spec((None, None, D_FF, D_MODEL), lambda b, t: (layer, sub, 0, 0)),
            pl.BlockSpec((1, 1, D_MODEL), lambda b, t: (ln_idx, 0, 0)),
            pl.BlockSpec((1, 1, D_MODEL), lambda b, t: (ln_idx, 0, 0)),
        ],
        out_specs=pl.BlockSpec((1, tm, D_MODEL), lambda b, t: (b, t, 0)),
        out_shape=jax.ShapeDtypeStruct((n_batch, n_rows, D_MODEL), F32),
        scratch_shapes=[pltpu.VMEM((tm, D_FF), BF16)],
        compiler_params=pltpu.CompilerParams(
            dimension_semantics=("arbitrary", "arbitrary"), vmem_limit_bytes=V7X_VMEM_LIMIT),
        name=f"ffn_l{layer}_s{sub}",
    )(*x_args, mod, mod, mod, w_in, w_out, ln_g, ln_b)


def _inproj_kernel(x_ref, sh_ref, sc_ref, w_ref, cs_ref, cos_ref, sin_ref,
                   na_ref, ml_ref, gla_ref, sm_ref, da_ref, dvt_ref, *, n_batch, tm):
    b = pl.program_id(0)
    t = pl.program_id(1)
    is_ctx = _is_ctx_rows(t, tm)
    x = x_ref[0]
    h = (x * (1.0 + _mod_rows(sc_ref, b, n_batch, is_ctx)) + _mod_rows(sh_ref, b, n_batch, is_ctx)).astype(BF16)
    na_ref[0] = (_dot(h, w_ref[:, C_NA:C_ML]) * cs_ref[:, C_NA:C_ML]).astype(BF16)
    ml_ref[0] = _dot(h, w_ref[:, C_ML:C_GLA]) * cs_ref[:, C_ML:C_GLA]
    gla_ref[0] = _dot(h, w_ref[:, C_GLA:C_SMALL]) * cs_ref[:, C_GLA:C_SMALL]
    sm_ref[0] = _dot(h, w_ref[:, C_SMALL:C_DA])
    pd = _dot(h, w_ref[:, C_DA:N_PROJ])
    cos = cos_ref[...]
    sin = sin_ref[...]
    w = BRANCH_W
    da_ref[0, :, 0:w] = ((pd[:, 0:w] * cos + pd[:, w:2 * w] * sin) * (DA_DQK ** -0.5 * LOG2E)).astype(BF16)
    da_ref[0, :, w:2 * w] = (pd[:, 2 * w:3 * w] * cos + pd[:, 3 * w:4 * w] * sin).astype(BF16)
    dvt_ref[0] = pd[:, 4 * w:5 * w].T.astype(BF16)


def _inproj(xs, mod, w_all, colscale, cos_tab, sin_tab, *, layer, tm):
    n_batch = xs.shape[0]
    rows = mod.shape[1]
    kern = functools.partial(_inproj_kernel, n_batch=n_batch, tm=tm)

    def out(width, dtype):
        return (pl.BlockSpec((1, tm, width), lambda b, t: (b, t, 0)),
                jax.ShapeDtypeStruct((n_batch, NTOK, width), dtype))

    outs = [out(768, BF16), out(1024, F32), out(768, F32), out(128, F32), out(512, BF16),
            (pl.BlockSpec((1, BRANCH_W, tm), lambda b, t: (b, 0, t)),
             jax.ShapeDtypeStruct((n_batch, BRANCH_W, NTOK), BF16))]
    return pl.pallas_call(
        kern,
        grid=(n_batch, NTOK // tm),
        in_specs=[
            pl.BlockSpec((1, tm, D_MODEL), lambda b, t: (b, t, 0)),
            _mod_spec(rows, layer, 3), _mod_spec(rows, layer, 4),
            _const_spec((D_MODEL, N_PROJ), lambda b, t: (0, 0)),
            pl.BlockSpec((1, C_SMALL), lambda b, t: (0, 0)),
            pl.BlockSpec((tm, BRANCH_W), lambda b, t: (t, 0)),
            pl.BlockSpec((tm, BRANCH_W), lambda b, t: (t, 0)),
        ],
        out_specs=[o[0] for o in outs],
        out_shape=[o[1] for o in outs],
        compiler_params=pltpu.CompilerParams(
            dimension_semantics=("arbitrary", "arbitrary"), vmem_limit_bytes=V7X_VMEM_LIMIT),
        name=f"inproj_l{layer}",
    )(xs, mod, mod, w_all, colscale, cos_tab, sin_tab)


def _softmax_pv(parts, vals):
    m = None
    for s in parts:
        mi = jnp.max(s, axis=-1, keepdims=True)
        m = mi if m is None else jnp.maximum(m, mi)
    l = None
    o = None
    for s, v in zip(parts, vals):
        p = jnp.exp2(s - m)
        li = jnp.sum(p, axis=-1, keepdims=True)
        oi = _dot(p.astype(BF16), v)
        l = li if l is None else l + li
        o = oi if o is None else o + oi
    return o / l


def _na_kernel(q_ref, k_ref, v_ref, bias_ref, o_ref, *, with_ctx, group):
    i = pl.program_id(0)

    @pl.when(i < N_LAT_BLK)
    def _():
        start = pl.multiple_of(jnp.clip(i - 1, 0, N_LAT_BLK - NA_BAND // BLK) * BLK, BLK)
        bias = bias_ref[...].reshape(N_HEADS * BLK, NA_BAND)
        for e in range(group):
            qs = _stack_heads(q_ref[e])
            kb = k_ref[e, pl.ds(start, NA_BAND), :]
            vb = v_ref[e, pl.ds(start, NA_BAND), :]
            s_loc = _dot_nt(qs, kb) + bias
            s_ctx = _dot_nt(qs, k_ref[e, SEQ:NTOK, :])
            o = _softmax_pv([s_loc, s_ctx], [vb, v_ref[e, SEQ:NTOK, :]])
            o_ref[e] = _unstack_heads(o, BLK).astype(BF16)

    if with_ctx:
        @pl.when(i == N_LAT_BLK)
        def _():
            for e in range(group):
                qs = _stack_heads(q_ref[e])
                vc = v_ref[e, SEQ:NTOK, :]
                o = _softmax_pv([_dot_nt(qs, k_ref[e, SEQ:NTOK, :])], [vc])
                o_ref[e] = _unstack_heads(o, BLK).astype(BF16)


def _na_pattern(i):
    return jnp.minimum(i, 1) + (i >= N_LAT_BLK - 2).astype(jnp.int32) + (i >= N_LAT_BLK - 1).astype(jnp.int32)


def _neighbourhood(na, bias, *, layer, with_ctx):
    n_batch = na.shape[0]
    n_blk = N_BLK if with_ctx else N_LAT_BLK
    group = _batch_group(n_batch)
    return pl.pallas_call(
        functools.partial(_na_kernel, with_ctx=with_ctx, group=group),
        grid=(n_blk, n_batch // group),
        in_specs=[
            pl.BlockSpec((group, BLK, BRANCH_W), lambda i, b: (b, i, 0)),
            pl.BlockSpec((group, NTOK, BRANCH_W), lambda i, b: (b, 0, 1)),
            pl.BlockSpec((group, NTOK, BRANCH_W), lambda i, b: (b, 0, 2)),
            pl.BlockSpec((N_HEADS, None, BLK, NA_BAND),
                         lambda i, b: (0, _na_pattern(jnp.minimum(i, N_LAT_BLK - 1)), 0, 0)),
        ],
        out_specs=pl.BlockSpec((group, BLK, BRANCH_W), lambda i, b: (b, i, 0)),
        out_shape=jax.ShapeDtypeStruct((n_batch, n_blk * BLK, BRANCH_W), BF16),
        compiler_params=pltpu.CompilerParams(
            dimension_semantics=("arbitrary", "arbitrary"), vmem_limit_bytes=V7X_VMEM_LIMIT),
        name=f"na_l{layer}",
    )(na, na, na, bias)


def _na_bias_tables(rpb):
    blocks = np.array([0, 1, N_LAT_BLK - 2, N_LAT_BLK - 1])
    qr = blocks[:, None] * NA_QROWS + np.arange(NA_QROWS)[None, :]
    r0 = np.clip(qr - NA_KH // 2, 0, GRID_ROWS - NA_KH)
    band0 = np.clip(blocks - 1, 0, N_LAT_BLK - NA_BAND // BLK) * NA_QROWS
    kr = band0[:, None] + np.arange(NA_BAND_ROWS)[None, :]
    row_ok = (kr[:, None, :] >= r0[:, :, None]) & (kr[:, None, :] < r0[:, :, None] + NA_KH)
    dr = np.clip(kr[:, None, :] - qr[:, :, None] + NA_KH - 1, 0, 2 * NA_KH - 2)
    col = np.arange(GRID_W)
    c0 = np.clip(col - NA_KW // 2, 0, GRID_W - NA_KW)
    col_ok = (col[None, :] >= c0[:, None]) & (col[None, :] < c0[:, None] + NA_KW)
    dc = np.clip(col[None, :] - col[:, None] + NA_KW - 1, 0, 2 * NA_KW - 2)
    ok = row_ok[:, :, None, :, None] & col_ok[None, None, :, None, :]
    oh_c = jnp.asarray(dc[:, :, None] == np.arange(2 * NA_KW - 1), dtype=F32)
    oh_r = jnp.asarray(dr[..., None] == np.arange(2 * NA_KH - 1), dtype=F32)
    by_col = jnp.einsum("hrc,xyc->hrxy", rpb, oh_c, precision=lax.Precision.HIGHEST)
    vals = jnp.einsum("pqkr,hrxy->hpqxky", oh_r, by_col, precision=lax.Precision.HIGHEST)
    tab = jnp.where(ok[None], vals * LOG2E, NEG)
    return tab.reshape(N_HEADS, 4, BLK, NA_BAND)


def _da_kernel(q_ref, k_ref, vt_ref, qall_ref, lam_ref, g_ref, o_ref, safe_scr, *, lambda_init, with_ctx):
    i = pl.program_id(1)
    lp = lam_ref[...]
    lam = (jnp.exp(jnp.sum(lp[0:1] * lp[1:2], axis=-1, keepdims=True))
           - jnp.exp(jnp.sum(lp[2:3] * lp[3:4], axis=-1, keepdims=True)) + lambda_init)
    q = q_ref[0]
    lane = lax.broadcasted_iota(jnp.int32, (1, BRANCH_W), 1)

    @pl.when(i == 0)
    def _():
        def max_sq_norm(ref):
            f = ref[0].astype(F32)
            return jnp.max(_dot((f * f).astype(BF16), g_ref[...]), axis=0, keepdims=True)

        bound2 = jnp.max(max_sq_norm(qall_ref) * max_sq_norm(k_ref)) * DA_BOUND_SLACK
        safe_scr[0] = (bound2 <= DA_SAFE_LOG2 ** 2).astype(jnp.int32)

    safe = safe_scr[0] == 1

    def attend(k, vt, stabilise):
        dv = BRANCH_W // N_HEADS
        outs = []
        for h in range(N_HEADS):
            lo = h * 2 * DA_DQK
            q1 = jnp.where((lane >= lo) & (lane < lo + DA_DQK), q, jnp.zeros_like(q))
            q2 = jnp.where((lane >= lo + DA_DQK) & (lane < lo + 2 * DA_DQK), q, jnp.zeros_like(q))
            st = _dot_nt(k, jnp.concatenate([q1, q2], axis=0))
            if stabilise:
                st = st - jnp.max(st, axis=0, keepdims=True)
            pt = jnp.exp2(st)
            r = 1.0 / jnp.sum(pt, axis=0, keepdims=True)
            ot = _dot(vt[h * dv:(h + 1) * dv, :], pt.astype(BF16))
            outs.append(ot[:, :BLK] * r[:, :BLK] - ot[:, BLK:] * (lam * r[:, BLK:]))
        return jnp.concatenate(outs, axis=0).T

    def both(k_fn, vt_fn):
        @pl.when(safe)
        def _():
            o_ref[0] = attend(k_fn(), vt_fn(), False)

        @pl.when(jnp.logical_not(safe))
        def _():
            o_ref[0] = attend(k_fn(), vt_fn(), True)

    @pl.when(i < N_LAT_BLK)
    def _():
        both(lambda: k_ref[0], lambda: vt_ref[0])

    if with_ctx:
        @pl.when(i == N_LAT_BLK)
        def _():
            both(lambda: k_ref[0, SEQ:NTOK, :], lambda: vt_ref[0, :, SEQ:NTOK])


def _diff_attention(da, da_vt, lam_p, gsum, *, layer, lambda_init, with_ctx):
    n_batch = da.shape[0]
    n_blk = N_BLK if with_ctx else N_LAT_BLK
    return pl.pallas_call(
        functools.partial(_da_kernel, lambda_init=lambda_init, with_ctx=with_ctx),
        grid=(n_batch, n_blk),
        in_specs=[
            pl.BlockSpec((1, BLK, BRANCH_W), lambda b, i: (b, i, 0)),
            pl.BlockSpec((1, NTOK, BRANCH_W), lambda b, i: (b, 0, 1)),
            pl.BlockSpec((1, BRANCH_W, NTOK), lambda b, i: (b, 0, 0)),
            pl.BlockSpec((1, NTOK, BRANCH_W), lambda b, i: (b, 0, 0)),
            pl.BlockSpec((None, 4, DA_DQK), lambda b, i: (layer, 0, 0)),
            pl.BlockSpec((BRANCH_W, BRANCH_W), lambda b, i: (0, 0)),
        ],
        out_specs=pl.BlockSpec((1, BLK, BRANCH_W), lambda b, i: (b, i, 0)),
        out_shape=jax.ShapeDtypeStruct((n_batch, n_blk * BLK, BRANCH_W), F32),
        scratch_shapes=[pltpu.SMEM((1,), jnp.int32)],
        compiler_params=pltpu.CompilerParams(
            dimension_semantics=("arbitrary", "arbitrary"), vmem_limit_bytes=V7X_VMEM_LIMIT),
        name=f"da_l{layer}",
    )(da, da, da_vt, da, lam_p, gsum)


def _fwd_chunk(g):
    return jnp.where(g == 0, N_LAT_BLK, g - 1)


def _bwd_chunk(g):
    return jnp.where(g == 0, N_LAT_BLK, N_LAT_BLK - g)


def _tri_mask(reverse):
    r = lax.broadcasted_iota(jnp.int32, (BLK, BLK), 0)
    c = lax.broadcasted_iota(jnp.int32, (BLK, BLK), 1)
    return (c >= r) if reverse else (c <= r)


def _cumsum_rows(x, mask):
    tri = jnp.where(mask, 1.0, 0.0).astype(BF16)
    return _dot_split(tri, x, 1, 3)


def _cummax_rows(x, reverse):
    n = x.shape[0]
    row = lax.broadcasted_iota(jnp.int32, (n, 1), 0)
    sh = 1
    while sh < n:
        if reverse:
            moved, ok = pltpu.roll(x, n - sh, axis=0), row < n - sh
        else:
            moved, ok = pltpu.roll(x, sh, axis=0), row >= sh
        x = jnp.maximum(x, jnp.where(ok, moved, NEG))
        sh *= 2
    return x


def _mlstm_chunk(e, d, q_ref, k_ref, v_ref, s_ref, gb_ref, ex_ref, o_ref, c_scr, n_scr, m_scr):
    reverse = d == 1
    mask = _tri_mask(reverse)
    end = 0 if reverse else BLK - 1
    st = 2 * e + d
    q = q_ref[e]
    k = k_ref[e]
    v = v_ref[e]
    qb = q.astype(BF16)
    kb = k.astype(BF16)
    vb = v.astype(BF16)
    il = 2 * N_HEADS * d
    lane = lax.broadcasted_iota(jnp.int32, (1, 128), 1)
    sel = (lane >= il) & (lane < il + N_HEADS)
    is_f = (lane >= il + ML_F_OFF) & (lane < il + ML_F_OFF + N_HEADS)
    gs = s_ref[e] + gb_ref[...]
    gl = jnp.where(is_f, jax.nn.log_sigmoid(gs), gs)
    bc = _cumsum_rows(gl, mask)
    b_al = pltpu.roll(bc, 128 - ML_F_OFF, axis=1)
    b_al = jnp.where(sel, b_al, 0.0)
    i_g = jnp.where(sel, gl, 0.0)
    u = i_g - b_al
    m_prev = m_scr[st, 0:1, :]
    c = jnp.maximum(_cummax_rows(u, reverse), m_prev)
    c3 = _split_f32(c * LOG2E, 3)
    u3 = _split_f32(u * LOG2E, 3)
    pc = c3[0] + pltpu.roll(c3[1], 16, axis=1) + pltpu.roll(c3[2], 32, axis=1)
    pu = pltpu.roll(u3[0], 48, axis=1) + pltpu.roll(u3[1], 64, axis=1) + pltpu.roll(u3[2], 80, axis=1)
    y = jnp.where(lane < 48, 1.0, pu).astype(BF16)
    xs = []
    for h in range(N_HEADS):
        l = il + h
        neg_c = (lane == l) | (lane == l + 16) | (lane == l + 32)
        ones = (lane == l + 48) | (lane == l + 64) | (lane == l + 80)
        xs.append(jnp.where(neg_c, -pc, jnp.where(ones, 1.0, 0.0)))
    e_all = _dot_nt(jnp.concatenate(xs, axis=0).astype(BF16), y)
    mask4 = jnp.concatenate([mask] * N_HEADS, axis=0)
    w_all = jnp.exp2(jnp.where(mask4, e_all, NEG))
    qk = _dot_nt(_stack_heads(qb), kb) * w_all
    num = _unstack_heads(_dot(qk.astype(BF16), vb), BLK)
    wi_p = jnp.exp(jnp.where(sel, m_prev - c, 0.0))
    em_p = jnp.exp(-(b_al + c))
    den = jnp.zeros((BLK, BRANCH_W), F32)
    w_inter = jnp.zeros((BLK, BRANCH_W), F32)
    exp_neg_m = jnp.zeros((BLK, BRANCH_W), F32)
    for h in range(N_HEADS):
        hm = _head_mask(BRANCH_W, h)
        den = jnp.where(hm, jnp.sum(qk[h * BLK:(h + 1) * BLK], axis=-1, keepdims=True), den)
        w_inter = jnp.where(hm, wi_p[:, il + h:il + h + 1], w_inter)
        exp_neg_m = jnp.where(hm, em_p[:, il + h:il + h + 1], exp_neg_m)
    ex = ex_ref[d]
    q_c = _dot(qb, c_scr[st].astype(BF16))
    q_n = _dot(qb, n_scr[st].astype(BF16))
    o_ref[e] = (num + w_inter * q_c) / jnp.maximum(jnp.abs(den + w_inter * q_n), exp_neg_m)
    b_end = b_al[end:end + 1, :]
    k_log = b_end - b_al + i_g
    m_new = jnp.maximum(b_end + m_prev, jnp.max(k_log, axis=0, keepdims=True))
    wk = jnp.where(sel, jnp.exp(k_log - m_new), 0.0)
    decay = jnp.where(sel, jnp.exp(b_end + m_prev - m_new), 0.0)
    kw = (k * _dot_split(wk, ex, 2, 1)).astype(BF16)
    decay_l = _dot_split(jnp.broadcast_to(decay, (8, 128)), ex, 3, 1)[0:1]
    r_head = lax.broadcasted_iota(jnp.int32, (BRANCH_W, BRANCH_W), 0) // (BRANCH_W // N_HEADS)
    c_head = lax.broadcasted_iota(jnp.int32, (BRANCH_W, BRANCH_W), 1) // (BRANCH_W // N_HEADS)
    diag = r_head == c_head
    c_scr[st] = decay_l * c_scr[st] + jnp.where(diag, _dot_tn(kw, vb), 0.0)
    n_sum = _dot_tn(kw, jnp.ones((BLK, BRANCH_W), BF16))
    n_scr[st] = decay_l * n_scr[st] + jnp.where(diag, n_sum, 0.0)
    m_scr[st, 0:1, :] = jnp.where(sel, m_new, 0.0)


def _mlstm_kernel(qf, kf, vf, sf, qb, kb, vb, sb, gb_ref, ex_ref, o_ref_f, o_ref_b, c_scr, n_scr, m_scr, *, group):
    @pl.when(pl.program_id(1) == 0)
    def _():
        c_scr[...] = jnp.zeros_like(c_scr)
        n_scr[...] = jnp.zeros_like(n_scr)
        m_scr[...] = jnp.zeros_like(m_scr)

    for e in range(group):
        _mlstm_chunk(e, 0, qf, kf, vf, sf, gb_ref, ex_ref, o_ref_f, c_scr, n_scr, m_scr)
        _mlstm_chunk(e, 1, qb, kb, vb, sb, gb_ref, ex_ref, o_ref_b, c_scr, n_scr, m_scr)


def _scan_specs(group, width_blocks, chunk_fn):
    return [pl.BlockSpec((group, BLK, BRANCH_W), lambda b, g, j=j: (b, chunk_fn(g), j)) for j in width_blocks]


def _mlstm_expand_matrix():
    ex = np.zeros((2, 128, BRANCH_W), np.float32)
    per = BRANCH_W // N_HEADS
    for d in range(2):
        for h in range(N_HEADS):
            ex[d, 2 * N_HEADS * d + h, h * per:(h + 1) * per] = 1.0
    return jnp.asarray(ex, dtype=BF16)


def _mlstm(ml, small, gate_b, *, layer):
    n_batch = ml.shape[0]
    group = 2 if n_batch % 2 == 0 else 1
    small_spec = lambda fn: pl.BlockSpec((group, BLK, 128), lambda b, g: (b, fn(g), 0))
    out_spec = lambda fn: pl.BlockSpec((group, BLK, BRANCH_W), lambda b, g: (b, fn(g), 0))
    return pl.pallas_call(
        functools.partial(_mlstm_kernel, group=group),
        grid=(n_batch // group, N_BLK),
        in_specs=(_scan_specs(group, (0, 1, 2), _fwd_chunk) + [small_spec(_fwd_chunk)]
                  + _scan_specs(group, (0, 1, 2), _bwd_chunk) + [small_spec(_bwd_chunk)]
                  + [pl.BlockSpec((1, 128), lambda b, g: (0, 0)),
                     pl.BlockSpec((2, 128, BRANCH_W), lambda b, g: (0, 0, 0))]),
        out_specs=[out_spec(_fwd_chunk), out_spec(_bwd_chunk)],
        out_shape=[jax.ShapeDtypeStruct((n_batch, NTOK, BRANCH_W), F32)] * 2,
        scratch_shapes=[pltpu.VMEM((2 * group, BRANCH_W, BRANCH_W), F32),
                        pltpu.VMEM((2 * group, BRANCH_W, BRANCH_W), F32),
                        pltpu.VMEM((2 * group, 8, 128), F32)],
        compiler_params=pltpu.CompilerParams(
            dimension_semantics=("arbitrary", "arbitrary"), vmem_limit_bytes=V7X_VMEM_LIMIT),
        name=f"mlstm_l{layer}",
    )(ml, ml, ml, small, ml, ml, ml, small, gate_b, _mlstm_expand_matrix())


def _gla_chunk(e, d, qk_ref, v_ref, s_ref, w2_ref, ba_ref, o_ref, st_scr):
    reverse = d == 1
    mask = _tri_mask(reverse)
    end = 0 if reverse else BLK - 1
    half = BLK // 2
    slot = 2 * e + d
    q = qk_ref[e, :, 0:128]
    k = qk_ref[e, :, 128:256]
    vb = v_ref[e].astype(BF16)
    x = _dot_split(s_ref[e], w2_ref[d], 2, 2) + ba_ref[d]
    la = jax.nn.log_sigmoid(x) * (1.0 / GLA_TAU)
    bc = _cumsum_rows(la, mask)
    ref_row = bc[half:half + 1, :]
    qe = (q * jnp.exp(bc - ref_row)).astype(BF16)
    ke = (k * jnp.exp(ref_row - bc)).astype(BF16)
    att = _dot_nt(_stack_heads(qe), ke)
    mask4 = jnp.concatenate([mask] * N_HEADS, axis=0)
    o_all = _dot(jnp.where(mask4, att, 0.0).astype(BF16), vb)
    st = st_scr[slot]
    inter = _dot_nt((q * jnp.exp(bc)).astype(BF16), st.astype(BF16))
    o_ref[e] = _unstack_heads(o_all, BLK) + inter
    b_end = bc[end:end + 1, :]
    kend = (k * jnp.exp(b_end - bc)).astype(BF16)
    r_head = lax.broadcasted_iota(jnp.int32, (BRANCH_W, 128), 0) // (BRANCH_W // N_HEADS)
    c_head = lax.broadcasted_iota(jnp.int32, (BRANCH_W, 128), 1) // GLA_DK
    st_scr[slot] = jnp.exp(b_end) * st + jnp.where(r_head == c_head, _dot_tn(vb, kend), 0.0)


def _gla_kernel(qkf, vf, sf, qkb, vb, sb, w2_ref, ba_ref, o_ref_f, o_ref_b, st_scr, *, group):
    @pl.when(pl.program_id(1) == 0)
    def _():
        st_scr[...] = jnp.zeros_like(st_scr)

    for e in range(group):
        _gla_chunk(e, 0, qkf, vf, sf, w2_ref, ba_ref, o_ref_f, st_scr)
        _gla_chunk(e, 1, qkb, vb, sb, w2_ref, ba_ref, o_ref_b, st_scr)


def _gla(gla, small, w2pad, ba, *, layer):
    n_batch = gla.shape[0]
    group = _batch_group(n_batch)
    small_spec = lambda fn: pl.BlockSpec((group, BLK, 128), lambda b, g: (b, fn(g), 0))
    out_spec = lambda fn: pl.BlockSpec((group, BLK, BRANCH_W), lambda b, g: (b, fn(g), 0))
    return pl.pallas_call(
        functools.partial(_gla_kernel, group=group),
        grid=(n_batch // group, N_BLK),
        in_specs=(_scan_specs(group, (0, 1), _fwd_chunk) + [small_spec(_fwd_chunk)]
                  + _scan_specs(group, (0, 1), _bwd_chunk) + [small_spec(_bwd_chunk)]
                  + [pl.BlockSpec((2, 128, 128), lambda b, g: (0, 0, 0)),
                     pl.BlockSpec((2, 1, 128), lambda b, g: (0, 0, 0))]),
        out_specs=[out_spec(_fwd_chunk), out_spec(_bwd_chunk)],
        out_shape=[jax.ShapeDtypeStruct((n_batch, NTOK, BRANCH_W), F32)] * 2,
        scratch_shapes=[pltpu.VMEM((2 * group, BRANCH_W, 128), F32)],
        compiler_params=pltpu.CompilerParams(
            dimension_semantics=("arbitrary", "arbitrary"), vmem_limit_bytes=V7X_VMEM_LIMIT),
        name=f"gla_l{layer}",
    )(gla, gla, small, gla, gla, small, w2pad, ba)


def _group_mean(x, gmat):
    return _dot_split(x, gmat, 2, 1)


def _merge_kernel(x_ref, sh_ref, sc_ref, gt_ref, na_ref, mlf_ref, mlb_ref, mlo_ref, glf_ref, glb_ref, glg_ref,
                  da_ref, ng_ref, gm_ref, wg_ref, wb_ref, wo_ref, lng_ref, lnb_ref, o_ref,
                  *, n_batch, tm, alpha, lambda_init):
    b = pl.program_id(0)
    t = pl.program_id(1)
    is_ctx = _is_ctx_rows(t, tm)
    x = x_ref[0]
    h = (x * (1.0 + _mod_rows(sc_ref, b, n_batch, is_ctx)) + _mod_rows(sh_ref, b, n_batch, is_ctx)).astype(BF16)
    gmat = gm_ref[...]

    def rms_norm(y, gain):
        return y * lax.rsqrt(_group_mean(y * y, gmat) + HEAD_EPS) * gain

    h_ml = mlf_ref[0] + mlb_ref[0]
    y_ml = jax.nn.sigmoid(mlo_ref[0]) * rms_norm(h_ml - _group_mean(h_ml, gmat), ng_ref[0:1, :])
    g_gl = glg_ref[0]
    y_gla = g_gl * jax.nn.sigmoid(g_gl) * rms_norm(glf_ref[0] + glb_ref[0], ng_ref[1:2, :])
    y_da = (1.0 - lambda_init) * rms_norm(da_ref[0], ng_ref[2:3, :])
    ys = [na_ref[0], y_ml.astype(BF16), y_gla.astype(BF16), y_da.astype(BF16)]
    acc = None
    for j in range(N_BRANCH):
        gate = jax.nn.sigmoid(_dot(h, wg_ref[:, j * D_MODEL:(j + 1) * D_MODEL]))
        term = gate * _dot(ys[j], wb_ref[j])
        acc = term if acc is None else acc + term
    y = _dot(acc.astype(BF16), wo_ref[...])
    y = alpha * x + _mod_rows(gt_ref, b, n_batch, is_ctx) * y
    o_ref[0] = _layer_norm(y, lng_ref[0], lnb_ref[0])


def _merge(xs, mod, na_o, ml_o, ml, gla_o, gla, da_o, norm_g, gmat, w_gates, w_branch, w_out, ln_g, ln_b,
           *, layer, n_rows, tm, alpha, lambda_init):
    n_batch = xs.shape[0]
    rows = mod.shape[1]
    ln_idx = layer * 3 + 1
    kern = functools.partial(_merge_kernel, n_batch=n_batch, tm=tm, alpha=alpha, lambda_init=lambda_init)
    tok = lambda j: pl.BlockSpec((1, tm, BRANCH_W), lambda b, t: (b, t, j))
    return pl.pallas_call(
        kern,
        grid=(n_batch, n_rows // tm),
        in_specs=[
            pl.BlockSpec((1, tm, D_MODEL), lambda b, t: (b, t, 0)),
            _mod_spec(rows, layer, 3), _mod_spec(rows, layer, 4), _mod_spec(rows, layer, 5),
            tok(0), tok(0), tok(0), tok(3), tok(0), tok(0), tok(2), tok(0),
            pl.BlockSpec((8, BRANCH_W), lambda b, t: (0, 0)),
            pl.BlockSpec((BRANCH_W, BRANCH_W), lambda b, t: (0, 0)),
            _const_spec((D_MODEL, N_BRANCH * D_MODEL), lambda b, t: (0, 0)),
            _const_spec((None, N_BRANCH, BRANCH_W, D_MODEL), lambda b, t: (layer, 0, 0, 0)),
            _const_spec((None, D_MODEL, D_MODEL), lambda b, t: (layer, 0, 0)),
            pl.BlockSpec((1, 1, D_MODEL), lambda b, t: (ln_idx, 0, 0)),
            pl.BlockSpec((1, 1, D_MODEL), lambda b, t: (ln_idx, 0, 0)),
        ],
        out_specs=pl.BlockSpec((1, tm, D_MODEL), lambda b, t: (b, t, 0)),
        out_shape=jax.ShapeDtypeStruct((n_batch, n_rows, D_MODEL), F32),
        compiler_params=pltpu.CompilerParams(
            dimension_semantics=("arbitrary", "arbitrary"), vmem_limit_bytes=V7X_VMEM_LIMIT),
        name=f"merge_l{layer}",
    )(xs, mod, mod, mod, na_o, ml_o[0], ml_o[1], ml, gla_o[0], gla_o[1], gla, da_o, norm_g, gmat,
      w_gates, w_branch, w_out, ln_g, ln_b)


def _rope_tables():
    t = jnp.arange(SEQ)
    n_f = DA_DQK // 4
    inv = ROPE_BASE ** (-jnp.arange(n_f, dtype=F32) / n_f)

    def cs(pos):
        ang = pos.astype(F32)[:, None] * inv
        return jnp.cos(ang), jnp.sin(ang)

    (cr, sr), (cc, sc) = cs(t // GRID_W), cs(t % GRID_W)
    cos32 = jnp.concatenate([cr, cr, cc, cc], axis=-1)
    sin32 = jnp.concatenate([-sr, sr, -sc, sc], axis=-1)
    reps = BRANCH_W // DA_DQK
    cos = jnp.concatenate([jnp.tile(cos32, (1, reps)), jnp.ones((CTX_LEN, BRANCH_W), F32)], axis=0)
    sin = jnp.concatenate([jnp.tile(sin32, (1, reps)), jnp.zeros((CTX_LEN, BRANCH_W), F32)], axis=0)
    return cos, sin


def _swap_perm():
    n_f = DA_DQK // 4
    base = np.concatenate([np.arange(n_f, 2 * n_f), np.arange(0, n_f),
                           np.arange(3 * n_f, 4 * n_f), np.arange(2 * n_f, 3 * n_f)])
    return np.concatenate([g * DA_DQK + base for g in range(BRANCH_W // DA_DQK)])


def _prep_mix_weights(w):
    na, ml, ml_if = w[:, 0:768], w[:, 768:1792], w[:, 1792:1808]
    gla, gla_a = w[:, 1808:2576], w[:, 2576:2608]
    dq, dk, dv = w[:, 2608:2864], w[:, 2864:3120], w[:, 3120:3376]
    gates = w[:, 3376:]
    perm = _swap_perm()
    pad = jnp.zeros((D_MODEL, 128 - ml_if.shape[1] - gla_a.shape[1]), w.dtype)
    w_all = jnp.concatenate([na, ml, gla, ml_if, gla_a, pad, dq, dq[:, perm], dk, dk[:, perm], dv], axis=1)
    return w_all.astype(BF16), gates.astype(BF16)


def _col_scale():
    cs = np.ones((1, C_SMALL), np.float32)
    cs[:, C_NA:C_NA + 256] = 64 ** -0.5 * LOG2E
    cs[:, C_ML:C_ML + 256] = 64 ** -0.5
    cs[:, C_GLA:C_GLA + 128] = GLA_DK ** -0.5
    return jnp.asarray(cs)


def _group_mean_matrix():
    head = np.arange(BRANCH_W) // (BRANCH_W // N_HEADS)
    return jnp.asarray((head[:, None] == head[None, :]) / (BRANCH_W // N_HEADS), dtype=BF16)


def _da_group_sum_matrix():
    grp = np.arange(BRANCH_W) // DA_DQK
    return jnp.asarray(grp[:, None] == grp[None, :], dtype=BF16)


def kernel(x, c, ctx, c_ctx, w_ada, b_ada, ln_g, ln_b, ffn_w_in, ffn_w_out, w_mix_in, na_rpb, ml_gate_b,
           ml_norm_g, gla_w_a2, gla_b_a, gla_norm_g, da_lambda, da_norm_g, w_branch, w_out):
    n_batch = x.shape[0]
    depth = w_ada.shape[0]
    alpha = (2 * depth) ** 0.25
    mod_rows = -(-(n_batch + 1) // 8) * 8
    cc = jnp.concatenate([c, c_ctx[None, :], jnp.zeros((mod_rows - n_batch - 1, D_MODEL), F32)], axis=0)
    mod = _ada(cc, w_ada, b_ada)

    xs = x
    w_in_bf = ffn_w_in.astype(BF16)
    w_out_bf = ffn_w_out.astype(BF16)
    w_branch_bf = w_branch.astype(BF16)
    w_o_bf = w_out.astype(BF16)
    ln_g3 = ln_g.reshape(depth * 3, 1, D_MODEL)
    ln_b3 = ln_b.reshape(depth * 3, 1, D_MODEL)
    cos_tab, sin_tab = _rope_tables()
    colscale = _col_scale()
    gmat = _group_mean_matrix()
    da_gsum = _da_group_sum_matrix()

    for l in range(depth):
        ctx_out = l < depth - 1
        lambda_init = 0.8 - 0.6 * math.exp(-0.3 * l)
        w_all, w_gates = _prep_mix_weights(w_mix_in[l])
        bias = _na_bias_tables(na_rpb[l])
        gate_b = jnp.concatenate([ml_gate_b[l], jnp.zeros((128 - ml_gate_b.shape[1],), F32)])[None, :]
        w2pad = jnp.zeros((2, 128, 128), F32)
        for d in range(2):
            lo = SMALL_A_OFF + d * GLA_RANK
            w2pad = w2pad.at[d, lo:lo + GLA_RANK, :].set(gla_w_a2[l, d])
        ba = gla_b_a[l][:, None, :]
        norm_g = jnp.concatenate([ml_norm_g[l][None], gla_norm_g[l][None], da_norm_g[l][None],
                                  jnp.zeros((5, BRANCH_W), F32)], axis=0)

        xs = _ffn(xs, mod, w_in_bf, w_out_bf, ln_g3, ln_b3, layer=l, sub=0, mod0=0,
                  n_rows=NTOK, tm=768, alpha=alpha, ctx=ctx if l == 0 else None)
        na, ml, gla, small, da, da_vt = _inproj(xs, mod, w_all, colscale, cos_tab, sin_tab, layer=l, tm=768)
        na_o = _neighbourhood(na, bias, layer=l, with_ctx=ctx_out)
        da_o = _diff_attention(da, da_vt, da_lambda, da_gsum, layer=l, lambda_init=lambda_init,
                               with_ctx=ctx_out)
        ml_o = _mlstm(ml, small, gate_b, layer=l)
        gla_o = _gla(gla, small, w2pad, ba, layer=l)
        n_rows, tm = (NTOK, 768) if ctx_out else (SEQ, 512)
        xs = _merge(xs, mod, na_o, ml_o, ml, gla_o, gla, da_o, norm_g, gmat, w_gates, w_branch_bf, w_o_bf,
                    ln_g3, ln_b3, layer=l, n_rows=n_rows, tm=tm, alpha=alpha, lambda_init=lambda_init)
        xs = _ffn(xs, mod, w_in_bf, w_out_bf, ln_g3, ln_b3, layer=l, sub=1, mod0=6,
                  n_rows=n_rows, tm=tm, alpha=alpha)
    return xs
```

```python
import functools
import math

import jax
import jax.numpy as jnp
import numpy as np
from jax import lax
from jax.experimental import pallas as pl
from jax.experimental.pallas import tpu as pltpu

F32 = jnp.float32
BF16 = jnp.bfloat16

D_MODEL = 1024
SEQ = 2048
CTX_LEN = 256
NTOK = SEQ + CTX_LEN
GRID_W = 64
GRID_ROWS = SEQ // GRID_W
N_MOD = 9
D_FF = 2816
FFN_HALF = 0.5
N_BRANCH = 4
BRANCH_W = 256
N_HEADS = 4
NA_KH = 8
NA_KW = 16
ML_F_OFF = 4
GLA_DK = 32
GLA_RANK = 16
GLA_TAU = 16.0
DA_DQK = 32
ROPE_BASE = 10000.0
LN_EPS = 1e-5
HEAD_EPS = 1e-6
NEG = -1e30
LOG2E = math.log2(math.e)
DA_SAFE_LOG2 = 100.0
DA_BOUND_SLACK = 1.1

V7X_VMEM_LIMIT = 56 * 1024 * 1024
FF_CHUNK = 256
BLK = 256
N_BLK = NTOK // BLK
N_LAT_BLK = SEQ // BLK
NA_BAND_ROWS = 12
NA_BAND = NA_BAND_ROWS * GRID_W
NA_QROWS = BLK // GRID_W

C_NA = 0
C_ML = 768
C_GLA = 1792
C_SMALL = 2560
C_DA = 2688
N_PROJ = 3968
SMALL_A_OFF = 16


def _dot(a, b):
    return jnp.dot(a, b, preferred_element_type=F32)


def _dot_nt(a, b):
    return lax.dot_general(a, b, (((1,), (1,)), ((), ())), preferred_element_type=F32)


def _dot_tn(a, b):
    return lax.dot_general(a, b, (((0,), (0,)), ((), ())), preferred_element_type=F32)


def _split_f32(x, n):
    parts = []
    r = x
    for i in range(n):
        p = r.astype(BF16).astype(F32)
        parts.append(p)
        if i + 1 < n:
            r = r - p
    return parts


def _split(x, n):
    return [p.astype(BF16) for p in _split_f32(x, n)]


def _dot_split(a, b, na, nb, dot=_dot):
    pa = _split(a, na) if na > 1 else [a.astype(BF16)]
    pb = _split(b, nb) if nb > 1 else [b.astype(BF16)]
    acc = None
    for i, x in enumerate(pa):
        for j, y in enumerate(pb):
            if i + j < max(na, nb):
                t = dot(x, y)
                acc = t if acc is None else acc + t
    return acc


def _layer_norm(y, g, b):
    mu = jnp.mean(y, axis=-1, keepdims=True)
    yc = y - mu
    var = jnp.mean(yc * yc, axis=-1, keepdims=True)
    return yc * lax.rsqrt(var + LN_EPS) * g + b


def _mod_rows(ref, b, n_batch, is_ctx):
    lat = ref[0, pl.ds(b, 1), :]
    ctx = ref[0, n_batch:n_batch + 1, :]
    return jnp.where(is_ctx, ctx, lat)


def _is_ctx_rows(t, tm):
    return (t * tm + lax.broadcasted_iota(jnp.int32, (tm, 1), 0)) >= SEQ


def _head_mask(width, head, n=1):
    lane = lax.broadcasted_iota(jnp.int32, (n, width), 1)
    per = width // N_HEADS
    return (lane >= head * per) & (lane < (head + 1) * per)


def _stack_heads(x):
    w = x.shape[-1]
    return jnp.concatenate([jnp.where(_head_mask(w, h), x, jnp.zeros_like(x)) for h in range(N_HEADS)], axis=0)


def _batch_group(n_batch):
    return 4 if n_batch % 4 == 0 else 2 if n_batch % 2 == 0 else 1


def _unstack_heads(x, m):
    w = x.shape[-1]
    out = jnp.zeros((m, w), x.dtype)
    for h in range(N_HEADS):
        out = jnp.where(_head_mask(w, h), x[h * m:(h + 1) * m], out)
    return out


def _ada_kernel(c_ref, w_ref, b_ref, o_ref):
    c = c_ref[...]
    s = c * jax.nn.sigmoid(c)
    o_ref[0] = _dot_split(s, w_ref[0], 2, 2) + b_ref[0]


def _ada(cc, w_ada, b_ada):
    depth = w_ada.shape[0]
    rows = cc.shape[0]
    return pl.pallas_call(
        _ada_kernel,
        grid=(depth, N_MOD),
        in_specs=[
            pl.BlockSpec((rows, D_MODEL), lambda l, k: (0, 0)),
            pl.BlockSpec((1, D_MODEL, D_MODEL), lambda l, k: (l, 0, k)),
            pl.BlockSpec((1, 1, D_MODEL), lambda l, k: (l, 0, k)),
        ],
        out_specs=pl.BlockSpec((1, rows, D_MODEL), lambda l, k: (l, 0, k)),
        out_shape=jax.ShapeDtypeStruct((depth, rows, N_MOD * D_MODEL), F32),
        compiler_params=pltpu.CompilerParams(
            dimension_semantics=("arbitrary", "arbitrary"), vmem_limit_bytes=V7X_VMEM_LIMIT),
        name="ada_mod",
    )(cc, w_ada, b_ada.reshape(depth, 1, N_MOD * D_MODEL))


def _ffn_kernel(*refs, n_batch, tm, alpha, n_x):
    x_refs = refs[:n_x]
    sh_ref, sc_ref, gt_ref, w_in_ref, w_out_ref, lng_ref, lnb_ref, o_ref, g_scr = refs[n_x:]
    b = pl.program_id(0)
    t = pl.program_id(1)
    is_ctx = _is_ctx_rows(t, tm)
    if n_x == 1:
        x = x_refs[0][0]
    else:
        parts = [r[0] for r in x_refs[:-1]]
        parts[-1] = jnp.where(t == pl.num_programs(1) - 1, x_refs[-1][0], parts[-1])
        x = jnp.concatenate(parts, axis=0)
    h = (x * (1.0 + _mod_rows(sc_ref, b, n_batch, is_ctx)) + _mod_rows(sh_ref, b, n_batch, is_ctx)).astype(BF16)
    for j in range(D_FF // FF_CHUNK):
        lo = j * FF_CHUNK
        a = _dot(h, w_in_ref[:, lo:lo + FF_CHUNK])
        v = _dot(h, w_in_ref[:, D_FF + lo:D_FF + lo + FF_CHUNK])
        g_scr[:, lo:lo + FF_CHUNK] = (a * jax.nn.sigmoid(a) * v).astype(BF16)
    y = _dot(g_scr[...], w_out_ref[...])
    y = alpha * x + FFN_HALF * _mod_rows(gt_ref, b, n_batch, is_ctx) * y
    o_ref[0] = _layer_norm(y, lng_ref[0], lnb_ref[0])


def _const_spec(shape, index_map):
    return pl.BlockSpec(shape, index_map, pipeline_mode=pl.Buffered(1))


def _mod_spec(rows, layer, k):
    return pl.BlockSpec((1, rows, D_MODEL), lambda b, t: (layer, 0, k))


def _ffn(xs, mod, w_in, w_out, ln_g, ln_b, *, layer, sub, mod0, n_rows, tm, alpha, ctx=None):
    n_batch = xs.shape[0]
    rows = mod.shape[1]
    ln_idx = layer * 3 + (0 if sub == 0 else 2)
    if ctx is None:
        x_args = [xs]
        x_specs = [pl.BlockSpec((1, tm, D_MODEL), lambda b, t: (b, t, 0))]
    else:
        per = tm // BLK
        x_args = [xs] * per + [ctx]
        x_specs = [pl.BlockSpec((1, BLK, D_MODEL), lambda b, t, j=j: (b, jnp.minimum(t * per + j, N_LAT_BLK - 1), 0))
                   for j in range(per)]
        x_specs.append(pl.BlockSpec((1, CTX_LEN, D_MODEL), lambda b, t: (b, 0, 0)))
    kern = functools.partial(_ffn_kernel, n_batch=n_batch, tm=tm, alpha=alpha, n_x=len(x_args))
    return pl.pallas_call(
        kern,
        grid=(n_batch, n_rows // tm),
        in_specs=x_specs + [
            _mod_spec(rows, layer, mod0), _mod_spec(rows, layer, mod0 + 1), _mod_spec(rows, layer, mod0 + 2),
            _const_spec((None, None, D_MODEL, 2 * D_FF), lambda b, t: (layer, sub, 0, 0)),
            _const_spec((None, None, D_FF, D_MODEL), lambda b, t: (layer, sub, 0, 0)),
            pl.BlockSpec((1, 1, D_MODEL), lambda b, t: (ln_idx, 0, 0)),
            pl.BlockSpec((1, 1, D_MODEL), lambda b, t: (ln_idx, 0, 0)),
        ],
        out_specs=pl.BlockSpec((1, tm, D_MODEL), lambda b, t: (b, t, 0)),
        out_shape=jax.ShapeDtypeStruct((n_batch, n_rows, D_MODEL), F32),
        scratch_shapes=[pltpu.VMEM((tm, D_FF), BF16)],
        compiler_params=pltpu.CompilerParams(
            dimension_semantics=("arbitrary", "arbitrary"), vmem_limit_bytes=V7X_VMEM_LIMIT),
        name=f"ffn_l{layer}_s{sub}",
    )(*x_args, mod, mod, mod, w_in, w_out, ln_g, ln_b)


def _inproj_kernel(x_ref, sh_ref, sc_ref, w_ref, cs_ref, cos_ref, sin_ref,
                   na_ref, nvt_ref, ml_ref, gla_ref, sm_ref, da_ref, dvt_ref, *, n_batch, tm):
    b = pl.program_id(0)
    t = pl.program_id(1)
    is_ctx = _is_ctx_rows(t, tm)
    x = x_ref[0]
    h = (x * (1.0 + _mod_rows(sc_ref, b, n_batch, is_ctx)) + _mod_rows(sh_ref, b, n_batch, is_ctx)).astype(BF16)
    p_na = _dot(h, w_ref[:, C_NA:C_ML]) * cs_ref[:, C_NA:C_ML]
    na_ref[0] = p_na[:, 0:2 * BRANCH_W].astype(BF16)
    nvt_ref[0] = p_na[:, 2 * BRANCH_W:3 * BRANCH_W].T.astype(BF16)
    ml_ref[0] = _dot(h, w_ref[:, C_ML:C_GLA]) * cs_ref[:, C_ML:C_GLA]
    gla_ref[0] = _dot(h, w_ref[:, C_GLA:C_SMALL]) * cs_ref[:, C_GLA:C_SMALL]
    sm_ref[0] = _dot(h, w_ref[:, C_SMALL:C_DA])
    pd = _dot(h, w_ref[:, C_DA:N_PROJ])
    cos = cos_ref[...]
    sin = sin_ref[...]
    w = BRANCH_W
    da_ref[0, :, 0:w] = ((pd[:, 0:w] * cos + pd[:, w:2 * w] * sin) * (DA_DQK ** -0.5 * LOG2E)).astype(BF16)
    da_ref[0, :, w:2 * w] = (pd[:, 2 * w:3 * w] * cos + pd[:, 3 * w:4 * w] * sin).astype(BF16)
    dvt_ref[0] = pd[:, 4 * w:5 * w].T.astype(BF16)


def _inproj(xs, mod, w_all, colscale, cos_tab, sin_tab, *, layer, tm):
    n_batch = xs.shape[0]
    rows = mod.shape[1]
    kern = functools.partial(_inproj_kernel, n_batch=n_batch, tm=tm)

    def out(width, dtype):
        return (pl.BlockSpec((1, tm, width), lambda b, t: (b, t, 0)),
                jax.ShapeDtypeStruct((n_batch, NTOK, width), dtype))

    transposed = (pl.BlockSpec((1, BRANCH_W, tm), lambda b, t: (b, 0, t)),
                  jax.ShapeDtypeStruct((n_batch, BRANCH_W, NTOK), BF16))
    outs = [out(512, BF16), transposed, out(1024, F32), out(768, F32), out(128, F32), out(512, BF16), transposed]
    return pl.pallas_call(
        kern,
        grid=(n_batch, NTOK // tm),
        in_specs=[
            pl.BlockSpec((1, tm, D_MODEL), lambda b, t: (b, t, 0)),
            _mod_spec(rows, layer, 3), _mod_spec(rows, layer, 4),
            _const_spec((D_MODEL, N_PROJ), lambda b, t: (0, 0)),
            pl.BlockSpec((1, C_SMALL), lambda b, t: (0, 0)),
            pl.BlockSpec((tm, BRANCH_W), lambda b, t: (t, 0)),
            pl.BlockSpec((tm, BRANCH_W), lambda b, t: (t, 0)),
        ],
        out_specs=[o[0] for o in outs],
        out_shape=[o[1] for o in outs],
        compiler_params=pltpu.CompilerParams(
            dimension_semantics=("arbitrary", "arbitrary"), vmem_limit_bytes=V7X_VMEM_LIMIT),
        name=f"inproj_l{layer}",
    )(xs, mod, mod, w_all, colscale, cos_tab, sin_tab)


def _softmax_pv_t(parts, vals_t):
    dv = BRANCH_W // N_HEADS
    m = None
    for s in parts:
        mi = jnp.max(s, axis=0, keepdims=True)
        m = mi if m is None else jnp.maximum(m, mi)
    ps = [jnp.exp2(s - m) for s in parts]
    l = sum(jnp.sum(p, axis=0, keepdims=True) for p in ps)
    r = 1.0 / l
    outs = []
    for h in range(N_HEADS):
        qs_h = slice(h * BLK, (h + 1) * BLK)
        o = sum(_dot(vt[h * dv:(h + 1) * dv, :], p[:, qs_h].astype(BF16)) for p, vt in zip(ps, vals_t))
        outs.append(o * r[:, qs_h])
    return jnp.concatenate(outs, axis=0).T


def _na_kernel(q_ref, k_ref, vt_ref, bias_ref, o_ref, *, with_ctx, group):
    i = pl.program_id(0)

    @pl.when(i < N_LAT_BLK)
    def _():
        start = pl.multiple_of(jnp.clip(i - 1, 0, N_LAT_BLK - NA_BAND // BLK) * BLK, BLK)
        bias = bias_ref[...]
        for e in range(group):
            qs = _stack_heads(q_ref[e])
            s_loc = _dot_nt(k_ref[e, pl.ds(start, NA_BAND), :], qs) + bias
            s_ctx = _dot_nt(k_ref[e, SEQ:NTOK, :], qs)
            vts = [vt_ref[e, :, pl.ds(start, NA_BAND)], vt_ref[e, :, SEQ:NTOK]]
            o_ref[e] = _softmax_pv_t([s_loc, s_ctx], vts).astype(BF16)

    if with_ctx:
        @pl.when(i == N_LAT_BLK)
        def _():
            for e in range(group):
                qs = _stack_heads(q_ref[e])
                s_ctx = _dot_nt(k_ref[e, SEQ:NTOK, :], qs)
                o_ref[e] = _softmax_pv_t([s_ctx], [vt_ref[e, :, SEQ:NTOK]]).astype(BF16)


def _na_pattern(i):
    return jnp.minimum(i, 1) + (i >= N_LAT_BLK - 2).astype(jnp.int32) + (i >= N_LAT_BLK - 1).astype(jnp.int32)


def _neighbourhood(na, na_vt, bias, *, layer, with_ctx):
    n_batch = na.shape[0]
    n_blk = N_BLK if with_ctx else N_LAT_BLK
    group = _batch_group(n_batch)
    return pl.pallas_call(
        functools.partial(_na_kernel, with_ctx=with_ctx, group=group),
        grid=(n_blk, n_batch // group),
        in_specs=[
            pl.BlockSpec((group, BLK, BRANCH_W), lambda i, b: (b, i, 0)),
            pl.BlockSpec((group, NTOK, BRANCH_W), lambda i, b: (b, 0, 1)),
            pl.BlockSpec((group, BRANCH_W, NTOK), lambda i, b: (b, 0, 0)),
            pl.BlockSpec((None, NA_BAND, N_HEADS * BLK),
                         lambda i, b: (_na_pattern(jnp.minimum(i, N_LAT_BLK - 1)), 0, 0)),
        ],
        out_specs=pl.BlockSpec((group, BLK, BRANCH_W), lambda i, b: (b, i, 0)),
        out_shape=jax.ShapeDtypeStruct((n_batch, n_blk * BLK, BRANCH_W), BF16),
        compiler_params=pltpu.CompilerParams(
            dimension_semantics=("arbitrary", "arbitrary"), vmem_limit_bytes=V7X_VMEM_LIMIT),
        name=f"na_l{layer}",
    )(na, na, na_vt, bias)


def _na_bias_tables(rpb):
    blocks = np.array([0, 1, N_LAT_BLK - 2, N_LAT_BLK - 1])
    qr = blocks[:, None] * NA_QROWS + np.arange(NA_QROWS)[None, :]
    r0 = np.clip(qr - NA_KH // 2, 0, GRID_ROWS - NA_KH)
    band0 = np.clip(blocks - 1, 0, N_LAT_BLK - NA_BAND // BLK) * NA_QROWS
    kr = band0[:, None] + np.arange(NA_BAND_ROWS)[None, :]
    row_ok = (kr[:, None, :] >= r0[:, :, None]) & (kr[:, None, :] < r0[:, :, None] + NA_KH)
    dr = np.clip(kr[:, None, :] - qr[:, :, None] + NA_KH - 1, 0, 2 * NA_KH - 2)
    col = np.arange(GRID_W)
    c0 = np.clip(col - NA_KW // 2, 0, GRID_W - NA_KW)
    col_ok = (col[None, :] >= c0[:, None]) & (col[None, :] < c0[:, None] + NA_KW)
    dc = np.clip(col[None, :] - col[:, None] + NA_KW - 1, 0, 2 * NA_KW - 2)
    ok = row_ok[:, :, None, :, None] & col_ok[None, None, :, None, :]
    oh_c = jnp.asarray(dc[:, :, None] == np.arange(2 * NA_KW - 1), dtype=F32)
    oh_r = jnp.asarray(dr[..., None] == np.arange(2 * NA_KH - 1), dtype=F32)
    by_col = jnp.einsum("hrc,xyc->hrxy", rpb, oh_c, precision=lax.Precision.HIGHEST)
    vals = jnp.einsum("pqkr,hrxy->pkyhqx", oh_r, by_col, precision=lax.Precision.HIGHEST)
    ok_t = ok.transpose(0, 3, 4, 1, 2)[:, :, :, None, :, :]
    tab = jnp.where(ok_t, vals * LOG2E, NEG)
    return tab.reshape(4, NA_BAND, N_HEADS * BLK)


def _da_kernel(q_ref, k_ref, vt_ref, qall_ref, lam_ref, g_ref, o_ref, safe_scr, *, lambda_init, with_ctx):
    i = pl.program_id(1)
    lp = lam_ref[...]
    lam = (jnp.exp(jnp.sum(lp[0:1] * lp[1:2], axis=-1, keepdims=True))
           - jnp.exp(jnp.sum(lp[2:3] * lp[3:4], axis=-1, keepdims=True)) + lambda_init)
    q = q_ref[0]
    lane = lax.broadcasted_iota(jnp.int32, (1, BRANCH_W), 1)

    @pl.when(i == 0)
    def _():
        def max_sq_norm(ref):
            f = ref[0].astype(F32)
            return jnp.max(_dot((f * f).astype(BF16), g_ref[...]), axis=0, keepdims=True)

        bound2 = jnp.max(max_sq_norm(qall_ref) * max_sq_norm(k_ref)) * DA_BOUND_SLACK
        safe_scr[0] = (bound2 <= DA_SAFE_LOG2 ** 2).astype(jnp.int32)

    safe = safe_scr[0] == 1

    def attend(k, vt, stabilise):
        dv = BRANCH_W // N_HEADS
        outs = []
        for h in range(N_HEADS):
            lo = h * 2 * DA_DQK
            q1 = jnp.where((lane >= lo) & (lane < lo + DA_DQK), q, jnp.zeros_like(q))
            q2 = jnp.where((lane >= lo + DA_DQK) & (lane < lo + 2 * DA_DQK), q, jnp.zeros_like(q))
            st = _dot_nt(k, jnp.concatenate([q1, q2], axis=0))
            if stabilise:
                st = st - jnp.max(st, axis=0, keepdims=True)
            pt = jnp.exp2(st)
            r = 1.0 / jnp.sum(pt, axis=0, keepdims=True)
            ot = _dot(vt[h * dv:(h + 1) * dv, :], pt.astype(BF16))
            outs.append(ot[:, :BLK] * r[:, :BLK] - ot[:, BLK:] * (lam * r[:, BLK:]))
        return jnp.concatenate(outs, axis=0).T

    def both(k_fn, vt_fn):
        @pl.when(safe)
        def _():
            o_ref[0] = attend(k_fn(), vt_fn(), False)

        @pl.when(jnp.logical_not(safe))
        def _():
            o_ref[0] = attend(k_fn(), vt_fn(), True)

    @pl.when(i < N_LAT_BLK)
    def _():
        both(lambda: k_ref[0], lambda: vt_ref[0])

    if with_ctx:
        @pl.when(i == N_LAT_BLK)
        def _():
            both(lambda: k_ref[0, SEQ:NTOK, :], lambda: vt_ref[0, :, SEQ:NTOK])


def _diff_attention(da, da_vt, lam_p, gsum, *, layer, lambda_init, with_ctx):
    n_batch = da.shape[0]
    n_blk = N_BLK if with_ctx else N_LAT_BLK
    return pl.pallas_call(
        functools.partial(_da_kernel, lambda_init=lambda_init, with_ctx=with_ctx),
        grid=(n_batch, n_blk),
        in_specs=[
            pl.BlockSpec((1, BLK, BRANCH_W), lambda b, i: (b, i, 0)),
            pl.BlockSpec((1, NTOK, BRANCH_W), lambda b, i: (b, 0, 1)),
            pl.BlockSpec((1, BRANCH_W, NTOK), lambda b, i: (b, 0, 0)),
            pl.BlockSpec((1, NTOK, BRANCH_W), lambda b, i: (b, 0, 0)),
            pl.BlockSpec((None, 4, DA_DQK), lambda b, i: (layer, 0, 0)),
            pl.BlockSpec((BRANCH_W, BRANCH_W), lambda b, i: (0, 0)),
        ],
        out_specs=pl.BlockSpec((1, BLK, BRANCH_W), lambda b, i: (b, i, 0)),
        out_shape=jax.ShapeDtypeStruct((n_batch, n_blk * BLK, BRANCH_W), F32),
        scratch_shapes=[pltpu.SMEM((1,), jnp.int32)],
        compiler_params=pltpu.CompilerParams(
            dimension_semantics=("arbitrary", "arbitrary"), vmem_limit_bytes=V7X_VMEM_LIMIT),
        name=f"da_l{layer}",
    )(da, da, da_vt, da, lam_p, gsum)


def _fwd_chunk(g):
    return jnp.where(g == 0, N_LAT_BLK, g - 1)


def _bwd_chunk(g):
    return jnp.where(g == 0, N_LAT_BLK, N_LAT_BLK - g)


def _tri_mask(reverse):
    r = lax.broadcasted_iota(jnp.int32, (BLK, BLK), 0)
    c = lax.broadcasted_iota(jnp.int32, (BLK, BLK), 1)
    return (c >= r) if reverse else (c <= r)


def _cumsum_rows(x, mask):
    tri = jnp.where(mask, 1.0, 0.0).astype(BF16)
    return _dot_split(tri, x, 1, 3)


def _cummax_rows(x, reverse):
    n = x.shape[0]
    row = lax.broadcasted_iota(jnp.int32, (n, 1), 0)
    sh = 1
    while sh < n:
        if reverse:
            moved, ok = pltpu.roll(x, n - sh, axis=0), row < n - sh
        else:
            moved, ok = pltpu.roll(x, sh, axis=0), row >= sh
        x = jnp.maximum(x, jnp.where(ok, moved, NEG))
        sh *= 2
    return x


def _mlstm_chunk(e, d, q_ref, k_ref, v_ref, s_ref, gb_ref, ex_ref, o_ref, c_scr, n_scr, m_scr):
    reverse = d == 1
    mask = _tri_mask(reverse)
    end = 0 if reverse else BLK - 1
    st = 2 * e + d
    q = q_ref[e]
    k = k_ref[e]
    v = v_ref[e]
    qb = q.astype(BF16)
    kb = k.astype(BF16)
    vb = v.astype(BF16)
    il = 2 * N_HEADS * d
    lane = lax.broadcasted_iota(jnp.int32, (1, 128), 1)
    sel = (lane >= il) & (lane < il + N_HEADS)
    is_f = (lane >= il + ML_F_OFF) & (lane < il + ML_F_OFF + N_HEADS)
    gs = s_ref[e] + gb_ref[...]
    gl = jnp.where(is_f, jax.nn.log_sigmoid(gs), gs)
    bc = _cumsum_rows(gl, mask)
    b_al = pltpu.roll(bc, 128 - ML_F_OFF, axis=1)
    b_al = jnp.where(sel, b_al, 0.0)
    i_g = jnp.where(sel, gl, 0.0)
    u = i_g - b_al
    m_prev = m_scr[st, 0:1, :]
    c = jnp.maximum(_cummax_rows(u, reverse), m_prev)
    c3 = _split_f32(c * LOG2E, 3)
    u3 = _split_f32(u * LOG2E, 3)
    pc = c3[0] + pltpu.roll(c3[1], 16, axis=1) + pltpu.roll(c3[2], 32, axis=1)
    pu = pltpu.roll(u3[0], 48, axis=1) + pltpu.roll(u3[1], 64, axis=1) + pltpu.roll(u3[2], 80, axis=1)
    y = jnp.where(lane < 48, 1.0, pu).astype(BF16)
    xs = []
    for h in range(N_HEADS):
        l = il + h
        neg_c = (lane == l) | (lane == l + 16) | (lane == l + 32)
        ones = (lane == l + 48) | (lane == l + 64) | (lane == l + 80)
        xs.append(jnp.where(neg_c, -pc, jnp.where(ones, 1.0, 0.0)))
    e_all = _dot_nt(jnp.concatenate(xs, axis=0).astype(BF16), y)
    mask4 = jnp.concatenate([mask] * N_HEADS, axis=0)
    w_all = jnp.exp2(jnp.where(mask4, e_all, NEG))
    qk = _dot_nt(_stack_heads(qb), kb) * w_all
    num = _unstack_heads(_dot(qk.astype(BF16), vb), BLK)
    wi_p = jnp.exp(jnp.where(sel, m_prev - c, 0.0))
    em_p = jnp.exp(-(b_al + c))
    den = jnp.zeros((BLK, BRANCH_W), F32)
    w_inter = jnp.zeros((BLK, BRANCH_W), F32)
    exp_neg_m = jnp.zeros((BLK, BRANCH_W), F32)
    for h in range(N_HEADS):
        hm = _head_mask(BRANCH_W, h)
        den = jnp.where(hm, jnp.sum(qk[h * BLK:(h + 1) * BLK], axis=-1, keepdims=True), den)
        w_inter = jnp.where(hm, wi_p[:, il + h:il + h + 1], w_inter)
        exp_neg_m = jnp.where(hm, em_p[:, il + h:il + h + 1], exp_neg_m)
    ex = ex_ref[d]
    q_c = _dot(qb, c_scr[st].astype(BF16))
    q_n = _dot(qb, n_scr[st].astype(BF16))
    o_ref[e] = (num + w_inter * q_c) / jnp.maximum(jnp.abs(den + w_inter * q_n), exp_neg_m)
    b_end = b_al[end:end + 1, :]
    k_log = b_end - b_al + i_g
    m_new = jnp.maximum(b_end + m_prev, jnp.max(k_log, axis=0, keepdims=True))
    wk = jnp.where(sel, jnp.exp(k_log - m_new), 0.0)
    decay = jnp.where(sel, jnp.exp(b_end + m_prev - m_new), 0.0)
    kw = (k * _dot_split(wk, ex, 2, 1)).astype(BF16)
    decay_l = _dot_split(jnp.broadcast_to(decay, (8, 128)), ex, 3, 1)[0:1]
    r_head = lax.broadcasted_iota(jnp.int32, (BRANCH_W, BRANCH_W), 0) // (BRANCH_W // N_HEADS)
    c_head = lax.broadcasted_iota(jnp.int32, (BRANCH_W, BRANCH_W), 1) // (BRANCH_W // N_HEADS)
    diag = r_head == c_head
    c_scr[st] = decay_l * c_scr[st] + jnp.where(diag, _dot_tn(kw, vb), 0.0)
    n_sum = _dot_tn(kw, jnp.ones((BLK, BRANCH_W), BF16))
    n_scr[st] = decay_l * n_scr[st] + jnp.where(diag, n_sum, 0.0)
    m_scr[st, 0:1, :] = jnp.where(sel, m_new, 0.0)


def _mlstm_kernel(qf, kf, vf, sf, qb, kb, vb, sb, gb_ref, ex_ref, o_ref_f, o_ref_b, c_scr, n_scr, m_scr, *, group):
    @pl.when(pl.program_id(1) == 0)
    def _():
        c_scr[...] = jnp.zeros_like(c_scr)
        n_scr[...] = jnp.zeros_like(n_scr)
        m_scr[...] = jnp.zeros_like(m_scr)

    for e in range(group):
        _mlstm_chunk(e, 0, qf, kf, vf, sf, gb_ref, ex_ref, o_ref_f, c_scr, n_scr, m_scr)
        _mlstm_chunk(e, 1, qb, kb, vb, sb, gb_ref, ex_ref, o_ref_b, c_scr, n_scr, m_scr)


def _scan_specs(group, width_blocks, chunk_fn):
    return [pl.BlockSpec((group, BLK, BRANCH_W), lambda b, g, j=j: (b, chunk_fn(g), j)) for j in width_blocks]


def _mlstm_expand_matrix():
    ex = np.zeros((2, 128, BRANCH_W), np.float32)
    per = BRANCH_W // N_HEADS
    for d in range(2):
        for h in range(N_HEADS):
            ex[d, 2 * N_HEADS * d + h, h * per:(h + 1) * per] = 1.0
    return jnp.asarray(ex, dtype=BF16)


def _mlstm(ml, small, gate_b, *, layer):
    n_batch = ml.shape[0]
    group = 2 if n_batch % 2 == 0 else 1
    small_spec = lambda fn: pl.BlockSpec((group, BLK, 128), lambda b, g: (b, fn(g), 0))
    out_spec = lambda fn: pl.BlockSpec((group, BLK, BRANCH_W), lambda b, g: (b, fn(g), 0))
    return pl.pallas_call(
        functools.partial(_mlstm_kernel, group=group),
        grid=(n_batch // group, N_BLK),
        in_specs=(_scan_specs(group, (0, 1, 2), _fwd_chunk) + [small_spec(_fwd_chunk)]
                  + _scan_specs(group, (0, 1, 2), _bwd_chunk) + [small_spec(_bwd_chunk)]
                  + [pl.BlockSpec((1, 128), lambda b, g: (0, 0)),
                     pl.BlockSpec((2, 128, BRANCH_W), lambda b, g: (0, 0, 0))]),
        out_specs=[out_spec(_fwd_chunk), out_spec(_bwd_chunk)],
        out_shape=[jax.ShapeDtypeStruct((n_batch, NTOK, BRANCH_W), F32)] * 2,
        scratch_shapes=[pltpu.VMEM((2 * group, BRANCH_W, BRANCH_W), F32),
                        pltpu.VMEM((2 * group, BRANCH_W, BRANCH_W), F32),
                        pltpu.VMEM((2 * group, 8, 128), F32)],
        compiler_params=pltpu.CompilerParams(
            dimension_semantics=("arbitrary", "arbitrary"), vmem_limit_bytes=V7X_VMEM_LIMIT),
        name=f"mlstm_l{layer}",
    )(ml, ml, ml, small, ml, ml, ml, small, gate_b, _mlstm_expand_matrix())


def _gla_chunk(e, d, qk_ref, v_ref, s_ref, w2_ref, ba_ref, o_ref, st_scr):
    reverse = d == 1
    mask = _tri_mask(reverse)
    end = 0 if reverse else BLK - 1
    half = BLK // 2
    slot = 2 * e + d
    q = qk_ref[e, :, 0:128]
    k = qk_ref[e, :, 128:256]
    vb = v_ref[e].astype(BF16)
    x = _dot_split(s_ref[e], w2_ref[d], 2, 2) + ba_ref[d]
    la = jax.nn.log_sigmoid(x) * (1.0 / GLA_TAU)
    bc = _cumsum_rows(la, mask)
    ref_row = bc[half:half + 1, :]
    qe = (q * jnp.exp(bc - ref_row)).astype(BF16)
    ke = (k * jnp.exp(ref_row - bc)).astype(BF16)
    att = _dot_nt(_stack_heads(qe), ke)
    mask4 = jnp.concatenate([mask] * N_HEADS, axis=0)
    o_all = _dot(jnp.where(mask4, att, 0.0).astype(BF16), vb)
    st = st_scr[slot]
    inter = _dot_nt((q * jnp.exp(bc)).astype(BF16), st.astype(BF16))
    o_ref[e] = _unstack_heads(o_all, BLK) + inter
    b_end = bc[end:end + 1, :]
    kend = (k * jnp.exp(b_end - bc)).astype(BF16)
    r_head = lax.broadcasted_iota(jnp.int32, (BRANCH_W, 128), 0) // (BRANCH_W // N_HEADS)
    c_head = lax.broadcasted_iota(jnp.int32, (BRANCH_W, 128), 1) // GLA_DK
    st_scr[slot] = jnp.exp(b_end) * st + jnp.where(r_head == c_head, _dot_tn(vb, kend), 0.0)


def _gla_kernel(qkf, vf, sf, qkb, vb, sb, w2_ref, ba_ref, o_ref_f, o_ref_b, st_scr, *, group):
    @pl.when(pl.program_id(1) == 0)
    def _():
        st_scr[...] = jnp.zeros_like(st_scr)

    for e in range(group):
        _gla_chunk(e, 0, qkf, vf, sf, w2_ref, ba_ref, o_ref_f, st_scr)
        _gla_chunk(e, 1, qkb, vb, sb, w2_ref, ba_ref, o_ref_b, st_scr)


def _gla(gla, small, w2pad, ba, *, layer):
    n_batch = gla.shape[0]
    group = _batch_group(n_batch)
    small_spec = lambda fn: pl.BlockSpec((group, BLK, 128), lambda b, g: (b, fn(g), 0))
    out_spec = lambda fn: pl.BlockSpec((group, BLK, BRANCH_W), lambda b, g: (b, fn(g), 0))
    return pl.pallas_call(
        functools.partial(_gla_kernel, group=group),
        grid=(n_batch // group, N_BLK),
        in_specs=(_scan_specs(group, (0, 1), _fwd_chunk) + [small_spec(_fwd_chunk)]
                  + _scan_specs(group, (0, 1), _bwd_chunk) + [small_spec(_bwd_chunk)]
                  + [pl.BlockSpec((2, 128, 128), lambda b, g: (0, 0, 0)),
                     pl.BlockSpec((2, 1, 128), lambda b, g: (0, 0, 0))]),
        out_specs=[out_spec(_fwd_chunk), out_spec(_bwd_chunk)],
        out_shape=[jax.ShapeDtypeStruct((n_batch, NTOK, BRANCH_W), F32)] * 2,
        scratch_shapes=[pltpu.VMEM((2 * group, BRANCH_W, 128), F32)],
        compiler_params=pltpu.CompilerParams(
            dimension_semantics=("arbitrary", "arbitrary"), vmem_limit_bytes=V7X_VMEM_LIMIT),
        name=f"gla_l{layer}",
    )(gla, gla, small, gla, gla, small, w2pad, ba)


def _group_mean(x, gmat):
    return _dot_split(x, gmat, 2, 1)


def _merge_kernel(x_ref, sh_ref, sc_ref, gt_ref, na_ref, mlf_ref, mlb_ref, mlo_ref, glf_ref, glb_ref, glg_ref,
                  da_ref, ng_ref, gm_ref, wg_ref, wb_ref, wo_ref, lng_ref, lnb_ref, o_ref,
                  *, n_batch, tm, alpha, lambda_init):
    b = pl.program_id(0)
    t = pl.program_id(1)
    is_ctx = _is_ctx_rows(t, tm)
    x = x_ref[0]
    h = (x * (1.0 + _mod_rows(sc_ref, b, n_batch, is_ctx)) + _mod_rows(sh_ref, b, n_batch, is_ctx)).astype(BF16)
    gmat = gm_ref[...]

    def rms_norm(y, gain):
        return y * lax.rsqrt(_group_mean(y * y, gmat) + HEAD_EPS) * gain

    h_ml = mlf_ref[0] + mlb_ref[0]
    y_ml = jax.nn.sigmoid(mlo_ref[0]) * rms_norm(h_ml - _group_mean(h_ml, gmat), ng_ref[0:1, :])
    g_gl = glg_ref[0]
    y_gla = g_gl * jax.nn.sigmoid(g_gl) * rms_norm(glf_ref[0] + glb_ref[0], ng_ref[1:2, :])
    y_da = (1.0 - lambda_init) * rms_norm(da_ref[0], ng_ref[2:3, :])
    ys = [na_ref[0], y_ml.astype(BF16), y_gla.astype(BF16), y_da.astype(BF16)]
    acc = None
    for j in range(N_BRANCH):
        gate = jax.nn.sigmoid(_dot(h, wg_ref[:, j * D_MODEL:(j + 1) * D_MODEL]))
        term = gate * _dot(ys[j], wb_ref[j])
        acc = term if acc is None else acc + term
    y = _dot(acc.astype(BF16), wo_ref[...])
    y = alpha * x + _mod_rows(gt_ref, b, n_batch, is_ctx) * y
    o_ref[0] = _layer_norm(y, lng_ref[0], lnb_ref[0])


def _merge(xs, mod, na_o, ml_o, ml, gla_o, gla, da_o, norm_g, gmat, w_gates, w_branch, w_out, ln_g, ln_b,
           *, layer, n_rows, tm, alpha, lambda_init):
    n_batch = xs.shape[0]
    rows = mod.shape[1]
    ln_idx = layer * 3 + 1
    kern = functools.partial(_merge_kernel, n_batch=n_batch, tm=tm, alpha=alpha, lambda_init=lambda_init)
    tok = lambda j: pl.BlockSpec((1, tm, BRANCH_W), lambda b, t: (b, t, j))
    return pl.pallas_call(
        kern,
        grid=(n_batch, n_rows // tm),
        in_specs=[
            pl.BlockSpec((1, tm, D_MODEL), lambda b, t: (b, t, 0)),
            _mod_spec(rows, layer, 3), _mod_spec(rows, layer, 4), _mod_spec(rows, layer, 5),
            tok(0), tok(0), tok(0), tok(3), tok(0), tok(0), tok(2), tok(0),
            pl.BlockSpec((8, BRANCH_W), lambda b, t: (0, 0)),
            pl.BlockSpec((BRANCH_W, BRANCH_W), lambda b, t: (0, 0)),
            _const_spec((D_MODEL, N_BRANCH * D_MODEL), lambda b, t: (0, 0)),
            _const_spec((None, N_BRANCH, BRANCH_W, D_MODEL), lambda b, t: (layer, 0, 0, 0)),
            _const_spec((None, D_MODEL, D_MODEL), lambda b, t: (layer, 0, 0)),
            pl.BlockSpec((1, 1, D_MODEL), lambda b, t: (ln_idx, 0, 0)),
            pl.BlockSpec((1, 1, D_MODEL), lambda b, t: (ln_idx, 0, 0)),
        ],
        out_specs=pl.BlockSpec((1, tm, D_MODEL), lambda b, t: (b, t, 0)),
        out_shape=jax.ShapeDtypeStruct((n_batch, n_rows, D_MODEL), F32),
        compiler_params=pltpu.CompilerParams(
            dimension_semantics=("arbitrary", "arbitrary"), vmem_limit_bytes=V7X_VMEM_LIMIT),
        name=f"merge_l{layer}",
    )(xs, mod, mod, mod, na_o, ml_o[0], ml_o[1], ml, gla_o[0], gla_o[1], gla, da_o, norm_g, gmat,
      w_gates, w_branch, w_out, ln_g, ln_b)


def _rope_tables():
    t = jnp.arange(SEQ)
    n_f = DA_DQK // 4
    inv = ROPE_BASE ** (-jnp.arange(n_f, dtype=F32) / n_f)

    def cs(pos):
        ang = pos.astype(F32)[:, None] * inv
        return jnp.cos(ang), jnp.sin(ang)

    (cr, sr), (cc, sc) = cs(t // GRID_W), cs(t % GRID_W)
    cos32 = jnp.concatenate([cr, cr, cc, cc], axis=-1)
    sin32 = jnp.concatenate([-sr, sr, -sc, sc], axis=-1)
    reps = BRANCH_W // DA_DQK
    cos = jnp.concatenate([jnp.tile(cos32, (1, reps)), jnp.ones((CTX_LEN, BRANCH_W), F32)], axis=0)
    sin = jnp.concatenate([jnp.tile(sin32, (1, reps)), jnp.zeros((CTX_LEN, BRANCH_W), F32)], axis=0)
    return cos, sin


def _swap_perm():
    n_f = DA_DQK // 4
    base = np.concatenate([np.arange(n_f, 2 * n_f), np.arange(0, n_f),
                           np.arange(3 * n_f, 4 * n_f), np.arange(2 * n_f, 3 * n_f)])
    return np.concatenate([g * DA_DQK + base for g in range(BRANCH_W // DA_DQK)])


def _prep_mix_weights(w):
    na, ml, ml_if = w[:, 0:768], w[:, 768:1792], w[:, 1792:1808]
    gla, gla_a = w[:, 1808:2576], w[:, 2576:2608]
    dq, dk, dv = w[:, 2608:2864], w[:, 2864:3120], w[:, 3120:3376]
    gates = w[:, 3376:]
    perm = _swap_perm()
    pad = jnp.zeros((D_MODEL, 128 - ml_if.shape[1] - gla_a.shape[1]), w.dtype)
    w_all = jnp.concatenate([na, ml, gla, ml_if, gla_a, pad, dq, dq[:, perm], dk, dk[:, perm], dv], axis=1)
    return w_all.astype(BF16), gates.astype(BF16)


def _col_scale():
    cs = np.ones((1, C_SMALL), np.float32)
    cs[:, C_NA:C_NA + 256] = 64 ** -0.5 * LOG2E
    cs[:, C_ML:C_ML + 256] = 64 ** -0.5
    cs[:, C_GLA:C_GLA + 128] = GLA_DK ** -0.5
    return jnp.asarray(cs)


def _group_mean_matrix():
    head = np.arange(BRANCH_W) // (BRANCH_W // N_HEADS)
    return jnp.asarray((head[:, None] == head[None, :]) / (BRANCH_W // N_HEADS), dtype=BF16)


def _da_group_sum_matrix():
    grp = np.arange(BRANCH_W) // DA_DQK
    return jnp.asarray(grp[:, None] == grp[None, :], dtype=BF16)


def kernel(x, c, ctx, c_ctx, w_ada, b_ada, ln_g, ln_b, ffn_w_in, ffn_w_out, w_mix_in, na_rpb, ml_gate_b,
           ml_norm_g, gla_w_a2, gla_b_a, gla_norm_g, da_lambda, da_norm_g, w_branch, w_out):
    n_batch = x.shape[0]
    depth = w_ada.shape[0]
    alpha = (2 * depth) ** 0.25
    mod_rows = -(-(n_batch + 1) // 8) * 8
    cc = jnp.concatenate([c, c_ctx[None, :], jnp.zeros((mod_rows - n_batch - 1, D_MODEL), F32)], axis=0)
    mod = _ada(cc, w_ada, b_ada)

    xs = x
    w_in_bf = ffn_w_in.astype(BF16)
    w_out_bf = ffn_w_out.astype(BF16)
    w_branch_bf = w_branch.astype(BF16)
    w_o_bf = w_out.astype(BF16)
    ln_g3 = ln_g.reshape(depth * 3, 1, D_MODEL)
    ln_b3 = ln_b.reshape(depth * 3, 1, D_MODEL)
    cos_tab, sin_tab = _rope_tables()
    colscale = _col_scale()
    gmat = _group_mean_matrix()
    da_gsum = _da_group_sum_matrix()

    for l in range(depth):
        ctx_out = l < depth - 1
        lambda_init = 0.8 - 0.6 * math.exp(-0.3 * l)
        w_all, w_gates = _prep_mix_weights(w_mix_in[l])
        bias = _na_bias_tables(na_rpb[l])
        gate_b = jnp.concatenate([ml_gate_b[l], jnp.zeros((128 - ml_gate_b.shape[1],), F32)])[None, :]
        w2pad = jnp.zeros((2, 128, 128), F32)
        for d in range(2):
            lo = SMALL_A_OFF + d * GLA_RANK
            w2pad = w2pad.at[d, lo:lo + GLA_RANK, :].set(gla_w_a2[l, d])
        ba = gla_b_a[l][:, None, :]
        norm_g = jnp.concatenate([ml_norm_g[l][None], gla_norm_g[l][None], da_norm_g[l][None],
                                  jnp.zeros((5, BRANCH_W), F32)], axis=0)

        xs = _ffn(xs, mod, w_in_bf, w_out_bf, ln_g3, ln_b3, layer=l, sub=0, mod0=0,
                  n_rows=NTOK, tm=768, alpha=alpha, ctx=ctx if l == 0 else None)
        na, na_vt, ml, gla, small, da, da_vt = _inproj(xs, mod, w_all, colscale, cos_tab, sin_tab,
                                                       layer=l, tm=768)
        na_o = _neighbourhood(na, na_vt, bias, layer=l, with_ctx=ctx_out)
        da_o = _diff_attention(da, da_vt, da_lambda, da_gsum, layer=l, lambda_init=lambda_init,
                               with_ctx=ctx_out)
        ml_o = _mlstm(ml, small, gate_b, layer=l)
        gla_o = _gla(gla, small, w2pad, ba, layer=l)
        n_rows, tm = (NTOK, 768) if ctx_out else (SEQ, 512)
        xs = _merge(xs, mod, na_o, ml_o, ml, gla_o, gla, da_o, norm_g, gmat, w_gates, w_branch_bf, w_o_bf,
                    ln_g3, ln_b3, layer=l, n_rows=n_rows, tm=tm, alpha=alpha, lambda_init=lambda_init)
        xs = _ffn(xs, mod, w_in_bf, w_out_bf, ln_g3, ln_b3, layer=l, sub=1, mod0=6,
                  n_rows=n_rows, tm=tm, alpha=alpha)
    return xs
```

```python
import functools
import math

import jax
import jax.numpy as jnp
import numpy as np
from jax import lax
from jax.experimental import pallas as pl
from jax.experimental.pallas import tpu as pltpu

F32 = jnp.float32
BF16 = jnp.bfloat16

D_MODEL = 1024
SEQ = 2048
CTX_LEN = 256
NTOK = SEQ + CTX_LEN
GRID_W = 64
GRID_ROWS = SEQ // GRID_W
N_MOD = 9
D_FF = 2816
FFN_HALF = 0.5
N_BRANCH = 4
BRANCH_W = 256
N_HEADS = 4
NA_KH = 8
NA_KW = 16
ML_F_OFF = 4
GLA_DK = 32
GLA_RANK = 16
GLA_TAU = 16.0
DA_DQK = 32
ROPE_BASE = 10000.0
LN_EPS = 1e-5
HEAD_EPS = 1e-6
NEG = -1e30
LOG2E = math.log2(math.e)
DA_SAFE_LOG2 = 100.0
DA_BOUND_SLACK = 1.1

V7X_VMEM_LIMIT = 56 * 1024 * 1024
FF_CHUNK = 256
BLK = 256
N_BLK = NTOK // BLK
N_LAT_BLK = SEQ // BLK
NA_BAND_ROWS = 12
NA_BAND = NA_BAND_ROWS * GRID_W
NA_QROWS = BLK // GRID_W

C_NA = 0
C_ML = 768
C_GLA = 1792
C_SMALL = 2560
C_DA = 2688
N_PROJ = 3968
SMALL_A_OFF = 16


def _dot(a, b):
    return jnp.dot(a, b, preferred_element_type=F32)


def _dot_nt(a, b):
    return lax.dot_general(a, b, (((1,), (1,)), ((), ())), preferred_element_type=F32)


def _dot_tn(a, b):
    return lax.dot_general(a, b, (((0,), (0,)), ((), ())), preferred_element_type=F32)


def _split_f32(x, n):
    parts = []
    r = x
    for i in range(n):
        p = r.astype(BF16).astype(F32)
        parts.append(p)
        if i + 1 < n:
            r = r - p
    return parts


def _split(x, n):
    return [p.astype(BF16) for p in _split_f32(x, n)]


def _dot_split(a, b, na, nb, dot=_dot):
    pa = _split(a, na) if na > 1 else [a.astype(BF16)]
    pb = _split(b, nb) if nb > 1 else [b.astype(BF16)]
    acc = None
    for i, x in enumerate(pa):
        for j, y in enumerate(pb):
            if i + j < max(na, nb):
                t = dot(x, y)
                acc = t if acc is None else acc + t
    return acc


def _layer_norm(y, g, b):
    mu = jnp.mean(y, axis=-1, keepdims=True)
    yc = y - mu
    var = jnp.mean(yc * yc, axis=-1, keepdims=True)
    return yc * lax.rsqrt(var + LN_EPS) * g + b


def _mod_rows(ref, b, n_batch, is_ctx):
    lat = ref[0, pl.ds(b, 1), :]
    ctx = ref[0, n_batch:n_batch + 1, :]
    return jnp.where(is_ctx, ctx, lat)


def _is_ctx_rows(t, tm):
    return (t * tm + lax.broadcasted_iota(jnp.int32, (tm, 1), 0)) >= SEQ


def _head_mask(width, head, n=1):
    lane = lax.broadcasted_iota(jnp.int32, (n, width), 1)
    per = width // N_HEADS
    return (lane >= head * per) & (lane < (head + 1) * per)


def _stack_heads(x):
    w = x.shape[-1]
    return jnp.concatenate([jnp.where(_head_mask(w, h), x, jnp.zeros_like(x)) for h in range(N_HEADS)], axis=0)


def _batch_group(n_batch):
    return 4 if n_batch % 4 == 0 else 2 if n_batch % 2 == 0 else 1


def _unstack_heads(x, m):
    w = x.shape[-1]
    out = jnp.zeros((m, w), x.dtype)
    for h in range(N_HEADS):
        out = jnp.where(_head_mask(w, h), x[h * m:(h + 1) * m], out)
    return out


def _ada_kernel(c_ref, w_ref, b_ref, o_ref):
    c = c_ref[...]
    s = c * jax.nn.sigmoid(c)
    o_ref[0] = _dot_split(s, w_ref[0], 2, 2) + b_ref[0]


def _ada(cc, w_ada, b_ada):
    depth = w_ada.shape[0]
    rows = cc.shape[0]
    return pl.pallas_call(
        _ada_kernel,
        grid=(depth, N_MOD),
        in_specs=[
            pl.BlockSpec((rows, D_MODEL), lambda l, k: (0, 0)),
            pl.BlockSpec((1, D_MODEL, D_MODEL), lambda l, k: (l, 0, k)),
            pl.BlockSpec((1, 1, D_MODEL), lambda l, k: (l, 0, k)),
        ],
        out_specs=pl.BlockSpec((1, rows, D_MODEL), lambda l, k: (l, 0, k)),
        out_shape=jax.ShapeDtypeStruct((depth, rows, N_MOD * D_MODEL), F32),
        compiler_params=pltpu.CompilerParams(
            dimension_semantics=("arbitrary", "arbitrary"), vmem_limit_bytes=V7X_VMEM_LIMIT),
        name="ada_mod",
    )(cc, w_ada, b_ada.reshape(depth, 1, N_MOD * D_MODEL))


def _ffn_kernel(*refs, n_batch, tm, alpha, n_x):
    x_refs = refs[:n_x]
    sh_ref, sc_ref, gt_ref, w_in_ref, w_out_ref, lng_ref, lnb_ref, o_ref, g_scr = refs[n_x:]
    b = pl.program_id(0)
    t = pl.program_id(1)
    is_ctx = _is_ctx_rows(t, tm)
    if n_x == 1:
        x = x_refs[0][0]
    else:
        parts = [r[0] for r in x_refs[:-1]]
        parts[-1] = jnp.where(t == pl.num_programs(1) - 1, x_refs[-1][0], parts[-1])
        x = jnp.concatenate(parts, axis=0)
    h = (x * (1.0 + _mod_rows(sc_ref, b, n_batch, is_ctx)) + _mod_rows(sh_ref, b, n_batch, is_ctx)).astype(BF16)
    for j in range(D_FF // FF_CHUNK):
        lo = j * FF_CHUNK
        a = _dot(h, w_in_ref[:, lo:lo + FF_CHUNK])
        v = _dot(h, w_in_ref[:, D_FF + lo:D_FF + lo + FF_CHUNK])
        g_scr[:, lo:lo + FF_CHUNK] = (a * jax.nn.sigmoid(a) * v).astype(BF16)
    y = _dot(g_scr[...], w_out_ref[...])
    y = alpha * x + FFN_HALF * _mod_rows(gt_ref, b, n_batch, is_ctx) * y
    o_ref[0] = _layer_norm(y, lng_ref[0], lnb_ref[0])


def _const_spec(shape, index_map):
    return pl.BlockSpec(shape, index_map, pipeline_mode=pl.Buffered(1))


def _mod_spec(rows, layer, k):
    return pl.BlockSpec((1, rows, D_MODEL), lambda b, t: (layer, 0, k))


def _ffn(xs, mod, w_in, w_out, ln_g, ln_b, *, layer, sub, mod0, n_rows, tm, alpha, ctx=None):
    n_batch = xs.shape[0]
    rows = mod.shape[1]
    ln_idx = layer * 3 + (0 if sub == 0 else 2)
    if ctx is None:
        x_args = [xs]
        x_specs = [pl.BlockSpec((1, tm, D_MODEL), lambda b, t: (b, t, 0))]
    else:
        per = tm // BLK
        x_args = [xs] * per + [ctx]
        x_specs = [pl.BlockSpec((1, BLK, D_MODEL), lambda b, t, j=j: (b, jnp.minimum(t * per + j, N_LAT_BLK - 1), 0))
                   for j in range(per)]
        x_specs.append(pl.BlockSpec((1, CTX_LEN, D_MODEL), lambda b, t: (b, 0, 0)))
    kern = functools.partial(_ffn_kernel, n_batch=n_batch, tm=tm, alpha=alpha, n_x=len(x_args))
    return pl.pallas_call(
        kern,
        grid=(n_batch, n_rows // tm),
        in_specs=x_specs + [
            _mod_spec(rows, layer, mod0), _mod_spec(rows, layer, mod0 + 1), _mod_spec(rows, layer, mod0 + 2),
            _const_spec((None, None, D_MODEL, 2 * D_FF), lambda b, t: (layer, sub, 0, 0)),
            _const_spec((None, None, D_FF, D_MODEL), lambda b, t: (layer, sub, 0, 0)),
            pl.BlockSpec((1, 1, D_MODEL), lambda b, t: (ln_idx, 0, 0)),
            pl.BlockSpec((1, 1, D_MODEL), lambda b, t: (ln_idx, 0, 0)),
        ],
        out_specs=pl.BlockSpec((1, tm, D_MODEL), lambda b, t: (b, t, 0)),
        out_shape=jax.ShapeDtypeStruct((n_batch, n_rows, D_MODEL), F32),
        scratch_shapes=[pltpu.VMEM((tm, D_FF), BF16)],
        compiler_params=pltpu.CompilerParams(
            dimension_semantics=("arbitrary", "arbitrary"), vmem_limit_bytes=V7X_VMEM_LIMIT),
        name=f"ffn_l{layer}_s{sub}",
    )(*x_args, mod, mod, mod, w_in, w_out, ln_g, ln_b)


def _inproj_kernel(x_ref, sh_ref, sc_ref, w_ref, cs_ref, cos_ref, sin_ref,
                   na_ref, nvt_ref, ml_ref, gla_ref, sm_ref, da_ref, dvt_ref, *, n_batch, tm):
    b = pl.program_id(0)
    t = pl.program_id(1)
    is_ctx = _is_ctx_rows(t, tm)
    x = x_ref[0]
    h = (x * (1.0 + _mod_rows(sc_ref, b, n_batch, is_ctx)) + _mod_rows(sh_ref, b, n_batch, is_ctx)).astype(BF16)
    p_na = _dot(h, w_ref[:, C_NA:C_ML]) * cs_ref[:, C_NA:C_ML]
    na_ref[0] = p_na[:, 0:2 * BRANCH_W].astype(BF16)
    nvt_ref[0] = p_na[:, 2 * BRANCH_W:3 * BRANCH_W].T.astype(BF16)
    ml_ref[0] = _dot(h, w_ref[:, C_ML:C_GLA]) * cs_ref[:, C_ML:C_GLA]
    gla_ref[0] = _dot(h, w_ref[:, C_GLA:C_SMALL]) * cs_ref[:, C_GLA:C_SMALL]
    sm_ref[0] = _dot(h, w_ref[:, C_SMALL:C_DA])
    pd = _dot(h, w_ref[:, C_DA:N_PROJ])
    cos = cos_ref[...]
    sin = sin_ref[...]
    w = BRANCH_W
    da_ref[0, :, 0:w] = ((pd[:, 0:w] * cos + pd[:, w:2 * w] * sin) * (DA_DQK ** -0.5 * LOG2E)).astype(BF16)
    da_ref[0, :, w:2 * w] = (pd[:, 2 * w:3 * w] * cos + pd[:, 3 * w:4 * w] * sin).astype(BF16)
    dvt_ref[0] = pd[:, 4 * w:5 * w].T.astype(BF16)


def _inproj(xs, mod, w_all, colscale, cos_tab, sin_tab, *, layer, tm):
    n_batch = xs.shape[0]
    rows = mod.shape[1]
    kern = functools.partial(_inproj_kernel, n_batch=n_batch, tm=tm)

    def out(width, dtype):
        return (pl.BlockSpec((1, tm, width), lambda b, t: (b, t, 0)),
                jax.ShapeDtypeStruct((n_batch, NTOK, width), dtype))

    transposed = (pl.BlockSpec((1, BRANCH_W, tm), lambda b, t: (b, 0, t)),
                  jax.ShapeDtypeStruct((n_batch, BRANCH_W, NTOK), BF16))
    outs = [out(512, BF16), transposed, out(1024, F32), out(768, F32), out(128, F32), out(512, BF16), transposed]
    return pl.pallas_call(
        kern,
        grid=(n_batch, NTOK // tm),
        in_specs=[
            pl.BlockSpec((1, tm, D_MODEL), lambda b, t: (b, t, 0)),
            _mod_spec(rows, layer, 3), _mod_spec(rows, layer, 4),
            _const_spec((D_MODEL, N_PROJ), lambda b, t: (0, 0)),
            pl.BlockSpec((1, C_SMALL), lambda b, t: (0, 0)),
            pl.BlockSpec((tm, BRANCH_W), lambda b, t: (t, 0)),
            pl.BlockSpec((tm, BRANCH_W), lambda b, t: (t, 0)),
        ],
        out_specs=[o[0] for o in outs],
        out_shape=[o[1] for o in outs],
        compiler_params=pltpu.CompilerParams(
            dimension_semantics=("arbitrary", "arbitrary"), vmem_limit_bytes=V7X_VMEM_LIMIT),
        name=f"inproj_l{layer}",
    )(xs, mod, mod, w_all, colscale, cos_tab, sin_tab)


def _softmax_pv_t(parts, vals_t):
    dv = BRANCH_W // N_HEADS
    m = None
    for s in parts:
        mi = jnp.max(s, axis=0, keepdims=True)
        m = mi if m is None else jnp.maximum(m, mi)
    ps = [jnp.exp2(s - m) for s in parts]
    l = sum(jnp.sum(p, axis=0, keepdims=True) for p in ps)
    r = 1.0 / l
    outs = []
    for h in range(N_HEADS):
        qs_h = slice(h * BLK, (h + 1) * BLK)
        o = sum(_dot(vt[h * dv:(h + 1) * dv, :], p[:, qs_h].astype(BF16)) for p, vt in zip(ps, vals_t))
        outs.append(o * r[:, qs_h])
    return jnp.concatenate(outs, axis=0).T


def _na_kernel(q_ref, k_ref, vt_ref, bias_ref, o_ref, *, with_ctx, group):
    i = pl.program_id(0)

    @pl.when(i < N_LAT_BLK)
    def _():
        start = pl.multiple_of(jnp.clip(i - 1, 0, N_LAT_BLK - NA_BAND // BLK) * BLK, BLK)
        bias = bias_ref[...]
        for e in range(group):
            qs = _stack_heads(q_ref[e])
            s_loc = _dot_nt(k_ref[e, pl.ds(start, NA_BAND), :], qs) + bias
            s_ctx = _dot_nt(k_ref[e, SEQ:NTOK, :], qs)
            vts = [vt_ref[e, :, pl.ds(start, NA_BAND)], vt_ref[e, :, SEQ:NTOK]]
            o_ref[e] = _softmax_pv_t([s_loc, s_ctx], vts).astype(BF16)

    if with_ctx:
        @pl.when(i == N_LAT_BLK)
        def _():
            for e in range(group):
                qs = _stack_heads(q_ref[e])
                s_ctx = _dot_nt(k_ref[e, SEQ:NTOK, :], qs)
                o_ref[e] = _softmax_pv_t([s_ctx], [vt_ref[e, :, SEQ:NTOK]]).astype(BF16)


def _na_pattern(i):
    return jnp.minimum(i, 1) + (i >= N_LAT_BLK - 2).astype(jnp.int32) + (i >= N_LAT_BLK - 1).astype(jnp.int32)


def _neighbourhood(na, na_vt, bias, *, layer, with_ctx):
    n_batch = na.shape[0]
    n_blk = N_BLK if with_ctx else N_LAT_BLK
    group = _batch_group(n_batch)
    return pl.pallas_call(
        functools.partial(_na_kernel, with_ctx=with_ctx, group=group),
        grid=(n_blk, n_batch // group),
        in_specs=[
            pl.BlockSpec((group, BLK, BRANCH_W), lambda i, b: (b, i, 0)),
            pl.BlockSpec((group, NTOK, BRANCH_W), lambda i, b: (b, 0, 1)),
            pl.BlockSpec((group, BRANCH_W, NTOK), lambda i, b: (b, 0, 0)),
            pl.BlockSpec((None, NA_BAND, N_HEADS * BLK),
                         lambda i, b: (_na_pattern(jnp.minimum(i, N_LAT_BLK - 1)), 0, 0)),
        ],
        out_specs=pl.BlockSpec((group, BLK, BRANCH_W), lambda i, b: (b, i, 0)),
        out_shape=jax.ShapeDtypeStruct((n_batch, n_blk * BLK, BRANCH_W), BF16),
        compiler_params=pltpu.CompilerParams(
            dimension_semantics=("arbitrary", "arbitrary"), vmem_limit_bytes=V7X_VMEM_LIMIT),
        name=f"na_l{layer}",
    )(na, na, na_vt, bias)


def _na_bias_tables(rpb):
    blocks = np.array([0, 1, N_LAT_BLK - 2, N_LAT_BLK - 1])
    qr = blocks[:, None] * NA_QROWS + np.arange(NA_QROWS)[None, :]
    r0 = np.clip(qr - NA_KH // 2, 0, GRID_ROWS - NA_KH)
    band0 = np.clip(blocks - 1, 0, N_LAT_BLK - NA_BAND // BLK) * NA_QROWS
    kr = band0[:, None] + np.arange(NA_BAND_ROWS)[None, :]
    row_ok = (kr[:, None, :] >= r0[:, :, None]) & (kr[:, None, :] < r0[:, :, None] + NA_KH)
    dr = np.clip(kr[:, None, :] - qr[:, :, None] + NA_KH - 1, 0, 2 * NA_KH - 2)
    col = np.arange(GRID_W)
    c0 = np.clip(col - NA_KW // 2, 0, GRID_W - NA_KW)
    col_ok = (col[None, :] >= c0[:, None]) & (col[None, :] < c0[:, None] + NA_KW)
    dc = np.clip(col[None, :] - col[:, None] + NA_KW - 1, 0, 2 * NA_KW - 2)
    ok = row_ok[:, :, None, :, None] & col_ok[None, None, :, None, :]
    oh_c = jnp.asarray(dc[:, :, None] == np.arange(2 * NA_KW - 1), dtype=F32)
    oh_r = jnp.asarray(dr[..., None] == np.arange(2 * NA_KH - 1), dtype=F32)
    by_col = jnp.einsum("hrc,xyc->hrxy", rpb, oh_c, precision=lax.Precision.HIGHEST)
    vals = jnp.einsum("pqkr,hrxy->pkyhqx", oh_r, by_col, precision=lax.Precision.HIGHEST)
    ok_t = ok.transpose(0, 3, 4, 1, 2)[:, :, :, None, :, :]
    tab = jnp.where(ok_t, vals * LOG2E, NEG)
    return tab.reshape(4, NA_BAND, N_HEADS * BLK)


def _da_kernel(q_ref, k_ref, vt_ref, qall_ref, lam_ref, g_ref, o_ref, safe_scr, *, lambda_init, with_ctx,
               group):
    i = pl.program_id(1)
    lp = lam_ref[...]
    lam = (jnp.exp(jnp.sum(lp[0:1] * lp[1:2], axis=-1, keepdims=True))
           - jnp.exp(jnp.sum(lp[2:3] * lp[3:4], axis=-1, keepdims=True)) + lambda_init)
    lane = lax.broadcasted_iota(jnp.int32, (1, BRANCH_W), 1)

    @pl.when(i == 0)
    def _():
        def max_sq_norm(ref, e):
            f = ref[e].astype(F32)
            return jnp.max(_dot((f * f).astype(BF16), g_ref[...]), axis=0, keepdims=True)

        bound2 = functools.reduce(
            jnp.maximum, [jnp.max(max_sq_norm(qall_ref, e) * max_sq_norm(k_ref, e)) for e in range(group)])
        safe_scr[0] = (bound2 * DA_BOUND_SLACK <= DA_SAFE_LOG2 ** 2).astype(jnp.int32)

    safe = safe_scr[0] == 1

    def attend(q, k, vt, stabilise):
        dv = BRANCH_W // N_HEADS
        outs = []
        for h in range(N_HEADS):
            lo = h * 2 * DA_DQK
            q1 = jnp.where((lane >= lo) & (lane < lo + DA_DQK), q, jnp.zeros_like(q))
            q2 = jnp.where((lane >= lo + DA_DQK) & (lane < lo + 2 * DA_DQK), q, jnp.zeros_like(q))
            st = _dot_nt(k, jnp.concatenate([q1, q2], axis=0))
            if stabilise:
                st = st - jnp.max(st, axis=0, keepdims=True)
            pt = jnp.exp2(st)
            r = 1.0 / jnp.sum(pt, axis=0, keepdims=True)
            ot = _dot(vt[h * dv:(h + 1) * dv, :], pt.astype(BF16))
            outs.append(ot[:, :BLK] * r[:, :BLK] - ot[:, BLK:] * (lam * r[:, BLK:]))
        return jnp.concatenate(outs, axis=0).T

    def both(k_fn, vt_fn):
        for stabilise in (False, True):
            @pl.when(safe != stabilise)
            def _():
                for e in range(group):
                    o_ref[e] = attend(q_ref[e], k_fn(e), vt_fn(e), stabilise)

    @pl.when(i < N_LAT_BLK)
    def _():
        both(lambda e: k_ref[e], lambda e: vt_ref[e])

    if with_ctx:
        @pl.when(i == N_LAT_BLK)
        def _():
            both(lambda e: k_ref[e, SEQ:NTOK, :], lambda e: vt_ref[e, :, SEQ:NTOK])


def _diff_attention(da, da_vt, lam_p, gsum, *, layer, lambda_init, with_ctx):
    n_batch = da.shape[0]
    n_blk = N_BLK if with_ctx else N_LAT_BLK
    group = 2 if n_batch % 2 == 0 else 1
    return pl.pallas_call(
        functools.partial(_da_kernel, lambda_init=lambda_init, with_ctx=with_ctx, group=group),
        grid=(n_batch // group, n_blk),
        in_specs=[
            pl.BlockSpec((group, BLK, BRANCH_W), lambda b, i: (b, i, 0)),
            pl.BlockSpec((group, NTOK, BRANCH_W), lambda b, i: (b, 0, 1)),
            pl.BlockSpec((group, BRANCH_W, NTOK), lambda b, i: (b, 0, 0)),
            pl.BlockSpec((group, NTOK, BRANCH_W), lambda b, i: (b, 0, 0)),
            pl.BlockSpec((None, 4, DA_DQK), lambda b, i: (layer, 0, 0)),
            pl.BlockSpec((BRANCH_W, BRANCH_W), lambda b, i: (0, 0)),
        ],
        out_specs=pl.BlockSpec((group, BLK, BRANCH_W), lambda b, i: (b, i, 0)),
        out_shape=jax.ShapeDtypeStruct((n_batch, n_blk * BLK, BRANCH_W), F32),
        scratch_shapes=[pltpu.SMEM((1,), jnp.int32)],
        compiler_params=pltpu.CompilerParams(
            dimension_semantics=("arbitrary", "arbitrary"), vmem_limit_bytes=V7X_VMEM_LIMIT),
        name=f"da_l{layer}",
    )(da, da, da_vt, da, lam_p, gsum)


def _fwd_chunk(g):
    return jnp.where(g == 0, N_LAT_BLK, g - 1)


def _bwd_chunk(g):
    return jnp.where(g == 0, N_LAT_BLK, N_LAT_BLK - g)


def _tri_mask(reverse):
    r = lax.broadcasted_iota(jnp.int32, (BLK, BLK), 0)
    c = lax.broadcasted_iota(jnp.int32, (BLK, BLK), 1)
    return (c >= r) if reverse else (c <= r)


def _cumsum_rows(x, mask):
    tri = jnp.where(mask, 1.0, 0.0).astype(BF16)
    return _dot_split(tri, x, 1, 3)


def _cummax_rows(x, reverse):
    n = x.shape[0]
    row = lax.broadcasted_iota(jnp.int32, (n, 1), 0)
    sh = 1
    while sh < n:
        if reverse:
            moved, ok = pltpu.roll(x, n - sh, axis=0), row < n - sh
        else:
            moved, ok = pltpu.roll(x, sh, axis=0), row >= sh
        x = jnp.maximum(x, jnp.where(ok, moved, NEG))
        sh *= 2
    return x


def _mlstm_chunk(e, d, q_ref, k_ref, v_ref, s_ref, gb_ref, ex_ref, o_ref, c_scr, n_scr, m_scr):
    reverse = d == 1
    mask = _tri_mask(reverse)
    end = 0 if reverse else BLK - 1
    st = 2 * e + d
    q = q_ref[e]
    k = k_ref[e]
    v = v_ref[e]
    qb = q.astype(BF16)
    kb = k.astype(BF16)
    vb = v.astype(BF16)
    il = 2 * N_HEADS * d
    lane = lax.broadcasted_iota(jnp.int32, (1, 128), 1)
    sel = (lane >= il) & (lane < il + N_HEADS)
    is_f = (lane >= il + ML_F_OFF) & (lane < il + ML_F_OFF + N_HEADS)
    gs = s_ref[e] + gb_ref[...]
    gl = jnp.where(is_f, jax.nn.log_sigmoid(gs), gs)
    bc = _cumsum_rows(gl, mask)
    b_al = pltpu.roll(bc, 128 - ML_F_OFF, axis=1)
    b_al = jnp.where(sel, b_al, 0.0)
    i_g = jnp.where(sel, gl, 0.0)
    u = i_g - b_al
    m_prev = m_scr[st, 0:1, :]
    c = jnp.maximum(_cummax_rows(u, reverse), m_prev)
    c3 = _split_f32(c * LOG2E, 3)
    u3 = _split_f32(u * LOG2E, 3)
    pc = c3[0] + pltpu.roll(c3[1], 16, axis=1) + pltpu.roll(c3[2], 32, axis=1)
    pu = pltpu.roll(u3[0], 48, axis=1) + pltpu.roll(u3[1], 64, axis=1) + pltpu.roll(u3[2], 80, axis=1)
    y = jnp.where(lane < 48, 1.0, pu).astype(BF16)
    xs = []
    for h in range(N_HEADS):
        l = il + h
        neg_c = (lane == l) | (lane == l + 16) | (lane == l + 32)
        ones = (lane == l + 48) | (lane == l + 64) | (lane == l + 80)
        xs.append(jnp.where(neg_c, -pc, jnp.where(ones, 1.0, 0.0)))
    e_all = _dot_nt(jnp.concatenate(xs, axis=0).astype(BF16), y)
    mask4 = jnp.concatenate([mask] * N_HEADS, axis=0)
    w_all = jnp.exp2(jnp.where(mask4, e_all, NEG))
    qk = _dot_nt(_stack_heads(qb), kb) * w_all
    num = _unstack_heads(_dot(qk.astype(BF16), vb), BLK)
    wi_p = jnp.exp(jnp.where(sel, m_prev - c, 0.0))
    em_p = jnp.exp(-(b_al + c))
    den = jnp.zeros((BLK, BRANCH_W), F32)
    w_inter = jnp.zeros((BLK, BRANCH_W), F32)
    exp_neg_m = jnp.zeros((BLK, BRANCH_W), F32)
    for h in range(N_HEADS):
        hm = _head_mask(BRANCH_W, h)
        den = jnp.where(hm, jnp.sum(qk[h * BLK:(h + 1) * BLK], axis=-1, keepdims=True), den)
        w_inter = jnp.where(hm, wi_p[:, il + h:il + h + 1], w_inter)
        exp_neg_m = jnp.where(hm, em_p[:, il + h:il + h + 1], exp_neg_m)
    ex = ex_ref[d]
    q_c = _dot(qb, c_scr[st].astype(BF16))
    q_n = _dot(qb, n_scr[st].astype(BF16))
    o_ref[e] = (num + w_inter * q_c) / jnp.maximum(jnp.abs(den + w_inter * q_n), exp_neg_m)
    b_end = b_al[end:end + 1, :]
    k_log = b_end - b_al + i_g
    m_new = jnp.maximum(b_end + m_prev, jnp.max(k_log, axis=0, keepdims=True))
    wk = jnp.where(sel, jnp.exp(k_log - m_new), 0.0)
    decay = jnp.where(sel, jnp.exp(b_end + m_prev - m_new), 0.0)
    kw = (k * _dot_split(wk, ex, 2, 1)).astype(BF16)
    decay_l = _dot_split(jnp.broadcast_to(decay, (8, 128)), ex, 3, 1)[0:1]
    r_head = lax.broadcasted_iota(jnp.int32, (BRANCH_W, BRANCH_W), 0) // (BRANCH_W // N_HEADS)
    c_head = lax.broadcasted_iota(jnp.int32, (BRANCH_W, BRANCH_W), 1) // (BRANCH_W // N_HEADS)
    diag = r_head == c_head
    c_scr[st] = decay_l * c_scr[st] + jnp.where(diag, _dot_tn(kw, vb), 0.0)
    n_sum = _dot_tn(kw, jnp.ones((BLK, BRANCH_W), BF16))
    n_scr[st] = decay_l * n_scr[st] + jnp.where(diag, n_sum, 0.0)
    m_scr[st, 0:1, :] = jnp.where(sel, m_new, 0.0)


def _mlstm_kernel(qf, kf, vf, sf, qb, kb, vb, sb, gb_ref, ex_ref, o_ref_f, o_ref_b, c_scr, n_scr, m_scr, *, group):
    @pl.when(pl.program_id(1) == 0)
    def _():
        c_scr[...] = jnp.zeros_like(c_scr)
        n_scr[...] = jnp.zeros_like(n_scr)
        m_scr[...] = jnp.zeros_like(m_scr)

    for e in range(group):
        _mlstm_chunk(e, 0, qf, kf, vf, sf, gb_ref, ex_ref, o_ref_f, c_scr, n_scr, m_scr)
        _mlstm_chunk(e, 1, qb, kb, vb, sb, gb_ref, ex_ref, o_ref_b, c_scr, n_scr, m_scr)


def _scan_specs(group, width_blocks, chunk_fn):
    return [pl.BlockSpec((group, BLK, BRANCH_W), lambda b, g, j=j: (b, chunk_fn(g), j)) for j in width_blocks]


def _mlstm_expand_matrix():
    ex = np.zeros((2, 128, BRANCH_W), np.float32)
    per = BRANCH_W // N_HEADS
    for d in range(2):
        for h in range(N_HEADS):
            ex[d, 2 * N_HEADS * d + h, h * per:(h + 1) * per] = 1.0
    return jnp.asarray(ex, dtype=BF16)


def _mlstm(ml, small, gate_b, *, layer):
    n_batch = ml.shape[0]
    group = _batch_group(n_batch)
    small_spec = lambda fn: pl.BlockSpec((group, BLK, 128), lambda b, g: (b, fn(g), 0))
    out_spec = lambda fn: pl.BlockSpec((group, BLK, BRANCH_W), lambda b, g: (b, fn(g), 0))
    return pl.pallas_call(
        functools.partial(_mlstm_kernel, group=group),
        grid=(n_batch // group, N_BLK),
        in_specs=(_scan_specs(group, (0, 1, 2), _fwd_chunk) + [small_spec(_fwd_chunk)]
                  + _scan_specs(group, (0, 1, 2), _bwd_chunk) + [small_spec(_bwd_chunk)]
                  + [pl.BlockSpec((1, 128), lambda b, g: (0, 0)),
                     pl.BlockSpec((2, 128, BRANCH_W), lambda b, g: (0, 0, 0))]),
        out_specs=[out_spec(_fwd_chunk), out_spec(_bwd_chunk)],
        out_shape=[jax.ShapeDtypeStruct((n_batch, NTOK, BRANCH_W), F32)] * 2,
        scratch_shapes=[pltpu.VMEM((2 * group, BRANCH_W, BRANCH_W), F32),
                        pltpu.VMEM((2 * group, BRANCH_W, BRANCH_W), F32),
                        pltpu.VMEM((2 * group, 8, 128), F32)],
        compiler_params=pltpu.CompilerParams(
            dimension_semantics=("arbitrary", "arbitrary"), vmem_limit_bytes=V7X_VMEM_LIMIT),
        name=f"mlstm_l{layer}",
    )(ml, ml, ml, small, ml, ml, ml, small, gate_b, _mlstm_expand_matrix())


def _gla_chunk(e, d, qk_ref, v_ref, s_ref, w2_ref, ba_ref, o_ref, st_scr):
    reverse = d == 1
    mask = _tri_mask(reverse)
    end = 0 if reverse else BLK - 1
    half = BLK // 2
    slot = 2 * e + d
    q = qk_ref[e, :, 0:128]
    k = qk_ref[e, :, 128:256]
    vb = v_ref[e].astype(BF16)
    x = _dot_split(s_ref[e], w2_ref[d], 2, 2) + ba_ref[d]
    la = jax.nn.log_sigmoid(x) * (1.0 / GLA_TAU)
    bc = _cumsum_rows(la, mask)
    ref_row = bc[half:half + 1, :]
    qe = (q * jnp.exp(bc - ref_row)).astype(BF16)
    ke = (k * jnp.exp(ref_row - bc)).astype(BF16)
    att = _dot_nt(_stack_heads(qe), ke)
    mask4 = jnp.concatenate([mask] * N_HEADS, axis=0)
    o_all = _dot(jnp.where(mask4, att, 0.0).astype(BF16), vb)
    st = st_scr[slot]
    inter = _dot_nt((q * jnp.exp(bc)).astype(BF16), st.astype(BF16))
    o_ref[e] = _unstack_heads(o_all, BLK) + inter
    b_end = bc[end:end + 1, :]
    kend = (k * jnp.exp(b_end - bc)).astype(BF16)
    r_head = lax.broadcasted_iota(jnp.int32, (BRANCH_W, 128), 0) // (BRANCH_W // N_HEADS)
    c_head = lax.broadcasted_iota(jnp.int32, (BRANCH_W, 128), 1) // GLA_DK
    st_scr[slot] = jnp.exp(b_end) * st + jnp.where(r_head == c_head, _dot_tn(vb, kend), 0.0)


def _gla_kernel(qkf, vf, sf, qkb, vb, sb, w2_ref, ba_ref, o_ref_f, o_ref_b, st_scr, *, group):
    @pl.when(pl.program_id(1) == 0)
    def _():
        st_scr[...] = jnp.zeros_like(st_scr)

    for e in range(group):
        _gla_chunk(e, 0, qkf, vf, sf, w2_ref, ba_ref, o_ref_f, st_scr)
        _gla_chunk(e, 1, qkb, vb, sb, w2_ref, ba_ref, o_ref_b, st_scr)


def _gla(gla, small, w2pad, ba, *, layer):
    n_batch = gla.shape[0]
    group = _batch_group(n_batch)
    small_spec = lambda fn: pl.BlockSpec((group, BLK, 128), lambda b, g: (b, fn(g), 0))
    out_spec = lambda fn: pl.BlockSpec((group, BLK, BRANCH_W), lambda b, g: (b, fn(g), 0))
    return pl.pallas_call(
        functools.partial(_gla_kernel, group=group),
        grid=(n_batch // group, N_BLK),
        in_specs=(_scan_specs(group, (0, 1), _fwd_chunk) + [small_spec(_fwd_chunk)]
                  + _scan_specs(group, (0, 1), _bwd_chunk) + [small_spec(_bwd_chunk)]
                  + [pl.BlockSpec((2, 128, 128), lambda b, g: (0, 0, 0)),
                     pl.BlockSpec((2, 1, 128), lambda b, g: (0, 0, 0))]),
        out_specs=[out_spec(_fwd_chunk), out_spec(_bwd_chunk)],
        out_shape=[jax.ShapeDtypeStruct((n_batch, NTOK, BRANCH_W), F32)] * 2,
        scratch_shapes=[pltpu.VMEM((2 * group, BRANCH_W, 128), F32)],
        compiler_params=pltpu.CompilerParams(
            dimension_semantics=("arbitrary", "arbitrary"), vmem_limit_bytes=V7X_VMEM_LIMIT),
        name=f"gla_l{layer}",
    )(gla, gla, small, gla, gla, small, w2pad, ba)


def _group_mean(x, gmat):
    return _dot_split(x, gmat, 2, 1)


def _merge_kernel(x_ref, sh_ref, sc_ref, gt_ref, na_ref, mlf_ref, mlb_ref, mlo_ref, glf_ref, glb_ref, glg_ref,
                  da_ref, ng_ref, gm_ref, wg_ref, wb_ref, wo_ref, lng_ref, lnb_ref, o_ref,
                  *, n_batch, tm, alpha, lambda_init):
    b = pl.program_id(0)
    t = pl.program_id(1)
    is_ctx = _is_ctx_rows(t, tm)
    x = x_ref[0]
    h = (x * (1.0 + _mod_rows(sc_ref, b, n_batch, is_ctx)) + _mod_rows(sh_ref, b, n_batch, is_ctx)).astype(BF16)
    gmat = gm_ref[...]

    def rms_norm(y, gain):
        return y * lax.rsqrt(_group_mean(y * y, gmat) + HEAD_EPS) * gain

    h_ml = mlf_ref[0] + mlb_ref[0]
    y_ml = jax.nn.sigmoid(mlo_ref[0]) * rms_norm(h_ml - _group_mean(h_ml, gmat), ng_ref[0:1, :])
    g_gl = glg_ref[0]
    y_gla = g_gl * jax.nn.sigmoid(g_gl) * rms_norm(glf_ref[0] + glb_ref[0], ng_ref[1:2, :])
    y_da = (1.0 - lambda_init) * rms_norm(da_ref[0], ng_ref[2:3, :])
    ys = [na_ref[0], y_ml.astype(BF16), y_gla.astype(BF16), y_da.astype(BF16)]
    acc = None
    for j in range(N_BRANCH):
        gate = jax.nn.sigmoid(_dot(h, wg_ref[:, j * D_MODEL:(j + 1) * D_MODEL]))
        term = gate * _dot(ys[j], wb_ref[j])
        acc = term if acc is None else acc + term
    y = _dot(acc.astype(BF16), wo_ref[...])
    y = alpha * x + _mod_rows(gt_ref, b, n_batch, is_ctx) * y
    o_ref[0] = _layer_norm(y, lng_ref[0], lnb_ref[0])


def _merge(xs, mod, na_o, ml_o, ml, gla_o, gla, da_o, norm_g, gmat, w_gates, w_branch, w_out, ln_g, ln_b,
           *, layer, n_rows, tm, alpha, lambda_init):
    n_batch = xs.shape[0]
    rows = mod.shape[1]
    ln_idx = layer * 3 + 1
    kern = functools.partial(_merge_kernel, n_batch=n_batch, tm=tm, alpha=alpha, lambda_init=lambda_init)
    tok = lambda j: pl.BlockSpec((1, tm, BRANCH_W), lambda b, t: (b, t, j))
    return pl.pallas_call(
        kern,
        grid=(n_batch, n_rows // tm),
        in_specs=[
            pl.BlockSpec((1, tm, D_MODEL), lambda b, t: (b, t, 0)),
            _mod_spec(rows, layer, 3), _mod_spec(rows, layer, 4), _mod_spec(rows, layer, 5),
            tok(0), tok(0), tok(0), tok(3), tok(0), tok(0), tok(2), tok(0),
            pl.BlockSpec((8, BRANCH_W), lambda b, t: (0, 0)),
            pl.BlockSpec((BRANCH_W, BRANCH_W), lambda b, t: (0, 0)),
            _const_spec((D_MODEL, N_BRANCH * D_MODEL), lambda b, t: (0, 0)),
            _const_spec((None, N_BRANCH, BRANCH_W, D_MODEL), lambda b, t: (layer, 0, 0, 0)),
            _const_spec((None, D_MODEL, D_MODEL), lambda b, t: (layer, 0, 0)),
            pl.BlockSpec((1, 1, D_MODEL), lambda b, t: (ln_idx, 0, 0)),
            pl.BlockSpec((1, 1, D_MODEL), lambda b, t: (ln_idx, 0, 0)),
        ],
        out_specs=pl.BlockSpec((1, tm, D_MODEL), lambda b, t: (b, t, 0)),
        out_shape=jax.ShapeDtypeStruct((n_batch, n_rows, D_MODEL), F32),
        compiler_params=pltpu.CompilerParams(
            dimension_semantics=("arbitrary", "arbitrary"), vmem_limit_bytes=V7X_VMEM_LIMIT),
        name=f"merge_l{layer}",
    )(xs, mod, mod, mod, na_o, ml_o[0], ml_o[1], ml, gla_o[0], gla_o[1], gla, da_o, norm_g, gmat,
      w_gates, w_branch, w_out, ln_g, ln_b)


def _rope_tables():
    t = jnp.arange(SEQ)
    n_f = DA_DQK // 4
    inv = ROPE_BASE ** (-jnp.arange(n_f, dtype=F32) / n_f)

    def cs(pos):
        ang = pos.astype(F32)[:, None] * inv
        return jnp.cos(ang), jnp.sin(ang)

    (cr, sr), (cc, sc) = cs(t // GRID_W), cs(t % GRID_W)
    cos32 = jnp.concatenate([cr, cr, cc, cc], axis=-1)
    sin32 = jnp.concatenate([-sr, sr, -sc, sc], axis=-1)
    reps = BRANCH_W // DA_DQK
    cos = jnp.concatenate([jnp.tile(cos32, (1, reps)), jnp.ones((CTX_LEN, BRANCH_W), F32)], axis=0)
    sin = jnp.concatenate([jnp.tile(sin32, (1, reps)), jnp.zeros((CTX_LEN, BRANCH_W), F32)], axis=0)
    return cos, sin


def _swap_perm():
    n_f = DA_DQK // 4
    base = np.concatenate([np.arange(n_f, 2 * n_f), np.arange(0, n_f),
                           np.arange(3 * n_f, 4 * n_f), np.arange(2 * n_f, 3 * n_f)])
    return np.concatenate([g * DA_DQK + base for g in range(BRANCH_W // DA_DQK)])


def _prep_mix_weights(w):
    na, ml, ml_if = w[:, 0:768], w[:, 768:1792], w[:, 1792:1808]
    gla, gla_a = w[:, 1808:2576], w[:, 2576:2608]
    dq, dk, dv = w[:, 2608:2864], w[:, 2864:3120], w[:, 3120:3376]
    gates = w[:, 3376:]
    perm = _swap_perm()
    pad = jnp.zeros((D_MODEL, 128 - ml_if.shape[1] - gla_a.shape[1]), w.dtype)
    w_all = jnp.concatenate([na, ml, gla, ml_if, gla_a, pad, dq, dq[:, perm], dk, dk[:, perm], dv], axis=1)
    return w_all.astype(BF16), gates.astype(BF16)


def _col_scale():
    cs = np.ones((1, C_SMALL), np.float32)
    cs[:, C_NA:C_NA + 256] = 64 ** -0.5 * LOG2E
    cs[:, C_ML:C_ML + 256] = 64 ** -0.5
    cs[:, C_GLA:C_GLA + 128] = GLA_DK ** -0.5
    return jnp.asarray(cs)


def _group_mean_matrix():
    head = np.arange(BRANCH_W) // (BRANCH_W // N_HEADS)
    return jnp.asarray((head[:, None] == head[None, :]) / (BRANCH_W // N_HEADS), dtype=BF16)


def _da_group_sum_matrix():
    grp = np.arange(BRANCH_W) // DA_DQK
    return jnp.asarray(grp[:, None] == grp[None, :], dtype=BF16)


def kernel(x, c, ctx, c_ctx, w_ada, b_ada, ln_g, ln_b, ffn_w_in, ffn_w_out, w_mix_in, na_rpb, ml_gate_b,
           ml_norm_g, gla_w_a2, gla_b_a, gla_norm_g, da_lambda, da_norm_g, w_branch, w_out):
    n_batch = x.shape[0]
    depth = w_ada.shape[0]
    alpha = (2 * depth) ** 0.25
    mod_rows = -(-(n_batch + 1) // 8) * 8
    cc = jnp.concatenate([c, c_ctx[None, :], jnp.zeros((mod_rows - n_batch - 1, D_MODEL), F32)], axis=0)
    mod = _ada(cc, w_ada, b_ada)

    xs = x
    w_in_bf = ffn_w_in.astype(BF16)
    w_out_bf = ffn_w_out.astype(BF16)
    w_branch_bf = w_branch.astype(BF16)
    w_o_bf = w_out.astype(BF16)
    ln_g3 = ln_g.reshape(depth * 3, 1, D_MODEL)
    ln_b3 = ln_b.reshape(depth * 3, 1, D_MODEL)
    cos_tab, sin_tab = _rope_tables()
    colscale = _col_scale()
    gmat = _group_mean_matrix()
    da_gsum = _da_group_sum_matrix()

    for l in range(depth):
        ctx_out = l < depth - 1
        lambda_init = 0.8 - 0.6 * math.exp(-0.3 * l)
        w_all, w_gates = _prep_mix_weights(w_mix_in[l])
        bias = _na_bias_tables(na_rpb[l])
        gate_b = jnp.concatenate([ml_gate_b[l], jnp.zeros((128 - ml_gate_b.shape[1],), F32)])[None, :]
        w2pad = jnp.zeros((2, 128, 128), F32)
        for d in range(2):
            lo = SMALL_A_OFF + d * GLA_RANK
            w2pad = w2pad.at[d, lo:lo + GLA_RANK, :].set(gla_w_a2[l, d])
        ba = gla_b_a[l][:, None, :]
        norm_g = jnp.concatenate([ml_norm_g[l][None], gla_norm_g[l][None], da_norm_g[l][None],
                                  jnp.zeros((5, BRANCH_W), F32)], axis=0)

        xs = _ffn(xs, mod, w_in_bf, w_out_bf, ln_g3, ln_b3, layer=l, sub=0, mod0=0,
                  n_rows=NTOK, tm=768, alpha=alpha, ctx=ctx if l == 0 else None)
        na, na_vt, ml, gla, small, da, da_vt = _inproj(xs, mod, w_all, colscale, cos_tab, sin_tab,
                                                       layer=l, tm=768)
        na_o = _neighbourhood(na, na_vt, bias, layer=l, with_ctx=ctx_out)
        da_o = _diff_attention(da, da_vt, da_lambda, da_gsum, layer=l, lambda_init=lambda_init,
                               with_ctx=ctx_out)
        ml_o = _mlstm(ml, small, gate_b, layer=l)
        gla_o = _gla(gla, small, w2pad, ba, layer=l)
        n_rows, tm = (NTOK, 768) if ctx_out else (SEQ, 512)
        xs = _merge(xs, mod, na_o, ml_o, ml, gla_o, gla, da_o, norm_g, gmat, w_gates, w_branch_bf, w_o_bf,
                    ln_g3, ln_b3, layer=l, n_rows=n_rows, tm=tm, alpha=alpha, lambda_init=lambda_init)
        xs = _ffn(xs, mod, w_in_bf, w_out_bf, ln_g3, ln_b3, layer=l, sub=1, mod0=6,
                  n_rows=n_rows, tm=tm, alpha=alpha)
    return xs
```

```python
import functools
import math

import jax
import jax.numpy as jnp
import numpy as np
from jax import lax
from jax.experimental import pallas as pl
from jax.experimental.pallas import tpu as pltpu

F32 = jnp.float32
BF16 = jnp.bfloat16

D_MODEL = 1024
SEQ = 2048
CTX_LEN = 256
NTOK = SEQ + CTX_LEN
GRID_W = 64
GRID_ROWS = SEQ // GRID_W
N_MOD = 9
D_FF = 2816
FFN_HALF = 0.5
N_BRANCH = 4
BRANCH_W = 256
N_HEADS = 4
NA_KH = 8
NA_KW = 16
ML_F_OFF = 4
GLA_DK = 32
GLA_RANK = 16
GLA_TAU = 16.0
DA_DQK = 32
ROPE_BASE = 10000.0
LN_EPS = 1e-5
HEAD_EPS = 1e-6
NEG = -1e30
LOG2E = math.log2(math.e)
DA_SAFE_LOG2 = 100.0
DA_BOUND_SLACK = 1.1

V7X_VMEM_LIMIT = 56 * 1024 * 1024
FF_CHUNK = 256
BLK = 256
TM_ALL = 3 * BLK
TM_LAT = 2 * BLK
N_BLK = NTOK // BLK
N_LAT_BLK = SEQ // BLK
NA_BAND_ROWS = 12
NA_BAND = NA_BAND_ROWS * GRID_W
NA_QROWS = BLK // GRID_W

C_NA = 0
C_ML = 768
C_GLA = 1792
C_SMALL = 2560
C_DA = 2688
N_PROJ = 3968
SMALL_A_OFF = 16


def _dot(a, b):
    return jnp.dot(a, b, preferred_element_type=F32)


def _dot_nt(a, b):
    return lax.dot_general(a, b, (((1,), (1,)), ((), ())), preferred_element_type=F32)


def _dot_tn(a, b):
    return lax.dot_general(a, b, (((0,), (0,)), ((), ())), preferred_element_type=F32)


def _split_f32(x, n):
    parts = []
    r = x
    for i in range(n):
        p = r.astype(BF16).astype(F32)
        parts.append(p)
        if i + 1 < n:
            r = r - p
    return parts


def _split(x, n):
    return [p.astype(BF16) for p in _split_f32(x, n)]


def _dot_split(a, b, na, nb, dot=_dot):
    pa = _split(a, na) if na > 1 else [a.astype(BF16)]
    pb = _split(b, nb) if nb > 1 else [b.astype(BF16)]
    acc = None
    for i, x in enumerate(pa):
        for j, y in enumerate(pb):
            if i + j < max(na, nb):
                t = dot(x, y)
                acc = t if acc is None else acc + t
    return acc


def _layer_norm(y, g, b):
    mu = jnp.mean(y, axis=-1, keepdims=True)
    yc = y - mu
    var = jnp.mean(yc * yc, axis=-1, keepdims=True)
    return yc * lax.rsqrt(var + LN_EPS) * g + b


def _mod_rows(ref, b, n_batch, is_ctx):
    lat = ref[0, pl.ds(b, 1), :]
    ctx = ref[0, n_batch:n_batch + 1, :]
    return jnp.where(is_ctx, ctx, lat)


def _is_ctx_rows(t, tm):
    return (t * tm + lax.broadcasted_iota(jnp.int32, (tm, 1), 0)) >= SEQ


def _head_mask(width, head, n=1):
    lane = lax.broadcasted_iota(jnp.int32, (n, width), 1)
    per = width // N_HEADS
    return (lane >= head * per) & (lane < (head + 1) * per)


def _stack_heads(x):
    w = x.shape[-1]
    return jnp.concatenate([jnp.where(_head_mask(w, h), x, jnp.zeros_like(x)) for h in range(N_HEADS)], axis=0)


def _batch_group(n_batch, want=4):
    return next(g for g in (4, 2, 1) if g <= want and n_batch % g == 0)


def _unstack_heads(x, m):
    w = x.shape[-1]
    out = jnp.zeros((m, w), x.dtype)
    for h in range(N_HEADS):
        out = jnp.where(_head_mask(w, h), x[h * m:(h + 1) * m], out)
    return out


def _ada_kernel(c_ref, w_ref, b_ref, o_ref):
    c = c_ref[...]
    s = c * jax.nn.sigmoid(c)
    o_ref[0] = _dot_split(s, w_ref[0], 2, 2) + b_ref[0]


def _ada(cc, w_ada, b_ada):
    depth = w_ada.shape[0]
    rows = cc.shape[0]
    return pl.pallas_call(
        _ada_kernel,
        grid=(depth, N_MOD),
        in_specs=[
            pl.BlockSpec((rows, D_MODEL), lambda l, k: (0, 0)),
            pl.BlockSpec((1, D_MODEL, D_MODEL), lambda l, k: (l, 0, k)),
            pl.BlockSpec((1, 1, D_MODEL), lambda l, k: (l, 0, k)),
        ],
        out_specs=pl.BlockSpec((1, rows, D_MODEL), lambda l, k: (l, 0, k)),
        out_shape=jax.ShapeDtypeStruct((depth, rows, N_MOD * D_MODEL), F32),
        compiler_params=pltpu.CompilerParams(
            dimension_semantics=("arbitrary", "arbitrary"), vmem_limit_bytes=V7X_VMEM_LIMIT),
        name="ada_mod",
    )(cc, w_ada, b_ada.reshape(depth, 1, N_MOD * D_MODEL))


def _ffn_kernel(*refs, n_batch, tm, alpha, n_x):
    x_refs = refs[:n_x]
    sh_ref, sc_ref, gt_ref, w_in_ref, w_out_ref, lng_ref, lnb_ref, o_ref, g_scr = refs[n_x:]
    b = pl.program_id(0)
    t = pl.program_id(1)
    is_ctx = _is_ctx_rows(t, tm)
    if n_x == 1:
        x = x_refs[0][0]
    else:
        parts = [r[0] for r in x_refs[:-1]]
        parts[-1] = jnp.where(t == pl.num_programs(1) - 1, x_refs[-1][0], parts[-1])
        x = jnp.concatenate(parts, axis=0)
    h = (x * (1.0 + _mod_rows(sc_ref, b, n_batch, is_ctx)) + _mod_rows(sh_ref, b, n_batch, is_ctx)).astype(BF16)
    for j in range(D_FF // FF_CHUNK):
        lo = j * FF_CHUNK
        a = _dot(h, w_in_ref[:, lo:lo + FF_CHUNK])
        v = _dot(h, w_in_ref[:, D_FF + lo:D_FF + lo + FF_CHUNK])
        g_scr[:, lo:lo + FF_CHUNK] = (a * jax.nn.sigmoid(a) * v).astype(BF16)
    y = _dot(g_scr[...], w_out_ref[...])
    y = alpha * x + FFN_HALF * _mod_rows(gt_ref, b, n_batch, is_ctx) * y
    o_ref[0] = _layer_norm(y, lng_ref[0], lnb_ref[0])


def _const_spec(shape, index_map):
    return pl.BlockSpec(shape, index_map, pipeline_mode=pl.Buffered(1))


def _mod_spec(rows, layer, k):
    return pl.BlockSpec((1, rows, D_MODEL), lambda b, t: (layer, 0, k))


def _ffn(xs, mod, w_in, w_out, ln_g, ln_b, *, layer, sub, mod0, n_rows, tm, alpha, ctx=None):
    n_batch = xs.shape[0]
    rows = mod.shape[1]
    ln_idx = layer * 3 + (0 if sub == 0 else 2)
    if ctx is None:
        x_args = [xs]
        x_specs = [pl.BlockSpec((1, tm, D_MODEL), lambda b, t: (b, t, 0))]
    else:
        per = tm // BLK
        x_args = [xs] * per + [ctx]
        x_specs = [pl.BlockSpec((1, BLK, D_MODEL), lambda b, t, j=j: (b, jnp.minimum(t * per + j, N_LAT_BLK - 1), 0))
                   for j in range(per)]
        x_specs.append(pl.BlockSpec((1, CTX_LEN, D_MODEL), lambda b, t: (b, 0, 0)))
    kern = functools.partial(_ffn_kernel, n_batch=n_batch, tm=tm, alpha=alpha, n_x=len(x_args))
    return pl.pallas_call(
        kern,
        grid=(n_batch, n_rows // tm),
        in_specs=x_specs + [
            _mod_spec(rows, layer, mod0), _mod_spec(rows, layer, mod0 + 1), _mod_spec(rows, layer, mod0 + 2),
            _const_spec((None, None, D_MODEL, 2 * D_FF), lambda b, t: (layer, sub, 0, 0)),
            _const_spec((None, None, D_FF, D_MODEL), lambda b, t: (layer, sub, 0, 0)),
            pl.BlockSpec((1, 1, D_MODEL), lambda b, t: (ln_idx, 0, 0)),
            pl.BlockSpec((1, 1, D_MODEL), lambda b, t: (ln_idx, 0, 0)),
        ],
        out_specs=pl.BlockSpec((1, tm, D_MODEL), lambda b, t: (b, t, 0)),
        out_shape=jax.ShapeDtypeStruct((n_batch, n_rows, D_MODEL), F32),
        scratch_shapes=[pltpu.VMEM((tm, D_FF), BF16)],
        compiler_params=pltpu.CompilerParams(
            dimension_semantics=("arbitrary", "arbitrary"), vmem_limit_bytes=V7X_VMEM_LIMIT),
        name=f"ffn_l{layer}_s{sub}",
    )(*x_args, mod, mod, mod, w_in, w_out, ln_g, ln_b)


def _inproj_kernel(x_ref, sh_ref, sc_ref, w_ref, cs_ref, cos_ref, sin_ref,
                   na_ref, nvt_ref, ml_ref, gla_ref, sm_ref, da_ref, dvt_ref, *, n_batch, tm):
    b = pl.program_id(0)
    t = pl.program_id(1)
    is_ctx = _is_ctx_rows(t, tm)
    x = x_ref[0]
    h = (x * (1.0 + _mod_rows(sc_ref, b, n_batch, is_ctx)) + _mod_rows(sh_ref, b, n_batch, is_ctx)).astype(BF16)
    p_na = _dot(h, w_ref[:, C_NA:C_ML]) * cs_ref[:, C_NA:C_ML]
    na_ref[0] = p_na[:, 0:2 * BRANCH_W].astype(BF16)
    nvt_ref[0] = p_na[:, 2 * BRANCH_W:3 * BRANCH_W].T.astype(BF16)
    ml_ref[0] = _dot(h, w_ref[:, C_ML:C_GLA]) * cs_ref[:, C_ML:C_GLA]
    gla_ref[0] = _dot(h, w_ref[:, C_GLA:C_SMALL]) * cs_ref[:, C_GLA:C_SMALL]
    sm_ref[0] = _dot(h, w_ref[:, C_SMALL:C_DA])
    pd = _dot(h, w_ref[:, C_DA:N_PROJ])
    cos = cos_ref[...]
    sin = sin_ref[...]
    w = BRANCH_W
    da_ref[0, :, 0:w] = ((pd[:, 0:w] * cos + pd[:, w:2 * w] * sin) * (DA_DQK ** -0.5 * LOG2E)).astype(BF16)
    da_ref[0, :, w:2 * w] = (pd[:, 2 * w:3 * w] * cos + pd[:, 3 * w:4 * w] * sin).astype(BF16)
    dvt_ref[0] = pd[:, 4 * w:5 * w].T.astype(BF16)


def _inproj(xs, mod, w_all, colscale, cos_tab, sin_tab, *, layer, tm):
    n_batch = xs.shape[0]
    rows = mod.shape[1]
    kern = functools.partial(_inproj_kernel, n_batch=n_batch, tm=tm)

    def out(width, dtype):
        return (pl.BlockSpec((1, tm, width), lambda b, t: (b, t, 0)),
                jax.ShapeDtypeStruct((n_batch, NTOK, width), dtype))

    transposed = (pl.BlockSpec((1, BRANCH_W, tm), lambda b, t: (b, 0, t)),
                  jax.ShapeDtypeStruct((n_batch, BRANCH_W, NTOK), BF16))
    outs = [out(512, BF16), transposed, out(1024, F32), out(768, F32), out(128, F32), out(512, BF16), transposed]
    return pl.pallas_call(
        kern,
        grid=(n_batch, NTOK // tm),
        in_specs=[
            pl.BlockSpec((1, tm, D_MODEL), lambda b, t: (b, t, 0)),
            _mod_spec(rows, layer, 3), _mod_spec(rows, layer, 4),
            _const_spec((D_MODEL, N_PROJ), lambda b, t: (0, 0)),
            pl.BlockSpec((1, C_SMALL), lambda b, t: (0, 0)),
            pl.BlockSpec((tm, BRANCH_W), lambda b, t: (t, 0)),
            pl.BlockSpec((tm, BRANCH_W), lambda b, t: (t, 0)),
        ],
        out_specs=[o[0] for o in outs],
        out_shape=[o[1] for o in outs],
        compiler_params=pltpu.CompilerParams(
            dimension_semantics=("arbitrary", "arbitrary"), vmem_limit_bytes=V7X_VMEM_LIMIT),
        name=f"inproj_l{layer}",
    )(xs, mod, mod, w_all, colscale, cos_tab, sin_tab)


def _softmax_pv_t(parts, vals_t):
    dv = BRANCH_W // N_HEADS
    m = None
    for s in parts:
        mi = jnp.max(s, axis=0, keepdims=True)
        m = mi if m is None else jnp.maximum(m, mi)
    ps = [jnp.exp2(s - m) for s in parts]
    l = sum(jnp.sum(p, axis=0, keepdims=True) for p in ps)
    r = 1.0 / l
    outs = []
    for h in range(N_HEADS):
        qs_h = slice(h * BLK, (h + 1) * BLK)
        o = sum(_dot(vt[h * dv:(h + 1) * dv, :], p[:, qs_h].astype(BF16)) for p, vt in zip(ps, vals_t))
        outs.append(o * r[:, qs_h])
    return jnp.concatenate(outs, axis=0).T


def _na_kernel(q_ref, k_ref, vt_ref, bias_ref, o_ref, *, with_ctx, group):
    i = pl.program_id(0)

    @pl.when(i < N_LAT_BLK)
    def _():
        start = pl.multiple_of(jnp.clip(i - 1, 0, N_LAT_BLK - NA_BAND // BLK) * BLK, BLK)
        bias = bias_ref[...]
        for e in range(group):
            qs = _stack_heads(q_ref[e])
            s_loc = _dot_nt(k_ref[e, pl.ds(start, NA_BAND), :], qs) + bias
            s_ctx = _dot_nt(k_ref[e, SEQ:NTOK, :], qs)
            vts = [vt_ref[e, :, pl.ds(start, NA_BAND)], vt_ref[e, :, SEQ:NTOK]]
            o_ref[e] = _softmax_pv_t([s_loc, s_ctx], vts).astype(BF16)

    if with_ctx:
        @pl.when(i == N_LAT_BLK)
        def _():
            for e in range(group):
                qs = _stack_heads(q_ref[e])
                s_ctx = _dot_nt(k_ref[e, SEQ:NTOK, :], qs)
                o_ref[e] = _softmax_pv_t([s_ctx], [vt_ref[e, :, SEQ:NTOK]]).astype(BF16)


def _na_pattern(i):
    return jnp.minimum(i, 1) + (i >= N_LAT_BLK - 2).astype(jnp.int32) + (i >= N_LAT_BLK - 1).astype(jnp.int32)


def _neighbourhood(na, na_vt, bias, *, layer, with_ctx):
    n_batch = na.shape[0]
    n_blk = N_BLK if with_ctx else N_LAT_BLK
    group = _batch_group(n_batch)
    return pl.pallas_call(
        functools.partial(_na_kernel, with_ctx=with_ctx, group=group),
        grid=(n_blk, n_batch // group),
        in_specs=[
            pl.BlockSpec((group, BLK, BRANCH_W), lambda i, b: (b, i, 0)),
            pl.BlockSpec((group, NTOK, BRANCH_W), lambda i, b: (b, 0, 1)),
            pl.BlockSpec((group, BRANCH_W, NTOK), lambda i, b: (b, 0, 0)),
            pl.BlockSpec((None, NA_BAND, N_HEADS * BLK),
                         lambda i, b: (_na_pattern(jnp.minimum(i, N_LAT_BLK - 1)), 0, 0)),
        ],
        out_specs=pl.BlockSpec((group, BLK, BRANCH_W), lambda i, b: (b, i, 0)),
        out_shape=jax.ShapeDtypeStruct((n_batch, n_blk * BLK, BRANCH_W), BF16),
        compiler_params=pltpu.CompilerParams(
            dimension_semantics=("arbitrary", "arbitrary"), vmem_limit_bytes=V7X_VMEM_LIMIT),
        name=f"na_l{layer}",
    )(na, na, na_vt, bias)


def _na_bias_tables(rpb):
    blocks = np.array([0, 1, N_LAT_BLK - 2, N_LAT_BLK - 1])
    qr = blocks[:, None] * NA_QROWS + np.arange(NA_QROWS)[None, :]
    r0 = np.clip(qr - NA_KH // 2, 0, GRID_ROWS - NA_KH)
    band0 = np.clip(blocks - 1, 0, N_LAT_BLK - NA_BAND // BLK) * NA_QROWS
    kr = band0[:, None] + np.arange(NA_BAND_ROWS)[None, :]
    row_ok = (kr[:, None, :] >= r0[:, :, None]) & (kr[:, None, :] < r0[:, :, None] + NA_KH)
    dr = np.clip(kr[:, None, :] - qr[:, :, None] + NA_KH - 1, 0, 2 * NA_KH - 2)
    col = np.arange(GRID_W)
    c0 = np.clip(col - NA_KW // 2, 0, GRID_W - NA_KW)
    col_ok = (col[None, :] >= c0[:, None]) & (col[None, :] < c0[:, None] + NA_KW)
    dc = np.clip(col[None, :] - col[:, None] + NA_KW - 1, 0, 2 * NA_KW - 2)
    ok = row_ok[:, :, None, :, None] & col_ok[None, None, :, None, :]
    oh_c = jnp.asarray(dc[:, :, None] == np.arange(2 * NA_KW - 1), dtype=F32)
    oh_r = jnp.asarray(dr[..., None] == np.arange(2 * NA_KH - 1), dtype=F32)
    by_col = jnp.einsum("hrc,xyc->hrxy", rpb, oh_c, precision=lax.Precision.HIGHEST)
    vals = jnp.einsum("pqkr,hrxy->pkyhqx", oh_r, by_col, precision=lax.Precision.HIGHEST)
    ok_t = ok.transpose(0, 3, 4, 1, 2)[:, :, :, None, :, :]
    tab = jnp.where(ok_t, vals * LOG2E, NEG)
    return tab.reshape(4, NA_BAND, N_HEADS * BLK)


def _da_kernel(q_ref, k_ref, vt_ref, qall_ref, lam_ref, g_ref, o_ref, safe_scr, *, lambda_init, with_ctx,
               group):
    i = pl.program_id(1)
    lp = lam_ref[...]
    lam = (jnp.exp(jnp.sum(lp[0:1] * lp[1:2], axis=-1, keepdims=True))
           - jnp.exp(jnp.sum(lp[2:3] * lp[3:4], axis=-1, keepdims=True)) + lambda_init)
    lane = lax.broadcasted_iota(jnp.int32, (1, BRANCH_W), 1)

    @pl.when(i == 0)
    def _():
        def max_sq_norm(ref, e):
            f = ref[e].astype(F32)
            return jnp.max(_dot((f * f).astype(BF16), g_ref[...]), axis=0, keepdims=True)

        bound2 = functools.reduce(
            jnp.maximum, [jnp.max(max_sq_norm(qall_ref, e) * max_sq_norm(k_ref, e)) for e in range(group)])
        safe_scr[0] = (bound2 * DA_BOUND_SLACK <= DA_SAFE_LOG2 ** 2).astype(jnp.int32)

    safe = safe_scr[0] == 1

    def attend(q, k, vt, stabilise):
        dv = BRANCH_W // N_HEADS
        outs = []
        for h in range(N_HEADS):
            lo = h * 2 * DA_DQK
            q1 = jnp.where((lane >= lo) & (lane < lo + DA_DQK), q, jnp.zeros_like(q))
            q2 = jnp.where((lane >= lo + DA_DQK) & (lane < lo + 2 * DA_DQK), q, jnp.zeros_like(q))
            st = _dot_nt(k, jnp.concatenate([q1, q2], axis=0))
            if stabilise:
                st = st - jnp.max(st, axis=0, keepdims=True)
            pt = jnp.exp2(st)
            r = 1.0 / jnp.sum(pt, axis=0, keepdims=True)
            ot = _dot(vt[h * dv:(h + 1) * dv, :], pt.astype(BF16))
            outs.append(ot[:, :BLK] * r[:, :BLK] - ot[:, BLK:] * (lam * r[:, BLK:]))
        return jnp.concatenate(outs, axis=0).T

    def both(k_fn, vt_fn):
        for stabilise in (False, True):
            @pl.when(safe != stabilise)
            def _():
                for e in range(group):
                    o_ref[e] = attend(q_ref[e], k_fn(e), vt_fn(e), stabilise)

    @pl.when(i < N_LAT_BLK)
    def _():
        both(lambda e: k_ref[e], lambda e: vt_ref[e])

    if with_ctx:
        @pl.when(i == N_LAT_BLK)
        def _():
            both(lambda e: k_ref[e, SEQ:NTOK, :], lambda e: vt_ref[e, :, SEQ:NTOK])


def _diff_attention(da, da_vt, lam_p, gsum, *, layer, lambda_init, with_ctx):
    n_batch = da.shape[0]
    n_blk = N_BLK if with_ctx else N_LAT_BLK
    group = _batch_group(n_batch)
    return pl.pallas_call(
        functools.partial(_da_kernel, lambda_init=lambda_init, with_ctx=with_ctx, group=group),
        grid=(n_batch // group, n_blk),
        in_specs=[
            pl.BlockSpec((group, BLK, BRANCH_W), lambda b, i: (b, i, 0)),
            pl.BlockSpec((group, NTOK, BRANCH_W), lambda b, i: (b, 0, 1)),
            pl.BlockSpec((group, BRANCH_W, NTOK), lambda b, i: (b, 0, 0)),
            pl.BlockSpec((group, NTOK, BRANCH_W), lambda b, i: (b, 0, 0)),
            pl.BlockSpec((None, 4, DA_DQK), lambda b, i: (layer, 0, 0)),
            pl.BlockSpec((BRANCH_W, BRANCH_W), lambda b, i: (0, 0)),
        ],
        out_specs=pl.BlockSpec((group, BLK, BRANCH_W), lambda b, i: (b, i, 0)),
        out_shape=jax.ShapeDtypeStruct((n_batch, n_blk * BLK, BRANCH_W), F32),
        scratch_shapes=[pltpu.SMEM((1,), jnp.int32)],
        compiler_params=pltpu.CompilerParams(
            dimension_semantics=("arbitrary", "arbitrary"), vmem_limit_bytes=V7X_VMEM_LIMIT),
        name=f"da_l{layer}",
    )(da, da, da_vt, da, lam_p, gsum)


def _fwd_chunk(g):
    return jnp.where(g == 0, N_LAT_BLK, g - 1)


def _bwd_chunk(g):
    return jnp.where(g == 0, N_LAT_BLK, N_LAT_BLK - g)


def _tri_mask(reverse):
    r = lax.broadcasted_iota(jnp.int32, (BLK, BLK), 0)
    c = lax.broadcasted_iota(jnp.int32, (BLK, BLK), 1)
    return (c >= r) if reverse else (c <= r)


def _cumsum_rows(x, mask):
    tri = jnp.where(mask, 1.0, 0.0).astype(BF16)
    return _dot_split(tri, x, 1, 3)


def _cummax_rows(x, reverse):
    n = x.shape[0]
    row = lax.broadcasted_iota(jnp.int32, (n, 1), 0)
    sh = 1
    while sh < n:
        if reverse:
            moved, ok = pltpu.roll(x, n - sh, axis=0), row < n - sh
        else:
            moved, ok = pltpu.roll(x, sh, axis=0), row >= sh
        x = jnp.maximum(x, jnp.where(ok, moved, NEG))
        sh *= 2
    return x


def _mlstm_chunk(e, d, q_ref, k_ref, v_ref, s_ref, gb_ref, ex_ref, o_ref, c_scr, n_scr, m_scr):
    reverse = d == 1
    mask = _tri_mask(reverse)
    end = 0 if reverse else BLK - 1
    st = 2 * e + d
    q = q_ref[e]
    k = k_ref[e]
    v = v_ref[e]
    qb = q.astype(BF16)
    kb = k.astype(BF16)
    vb = v.astype(BF16)
    il = 2 * N_HEADS * d
    lane = lax.broadcasted_iota(jnp.int32, (1, 128), 1)
    sel = (lane >= il) & (lane < il + N_HEADS)
    is_f = (lane >= il + ML_F_OFF) & (lane < il + ML_F_OFF + N_HEADS)
    gs = s_ref[e] + gb_ref[...]
    gl = jnp.where(is_f, jax.nn.log_sigmoid(gs), gs)
    bc = _cumsum_rows(gl, mask)
    b_al = pltpu.roll(bc, 128 - ML_F_OFF, axis=1)
    b_al = jnp.where(sel, b_al, 0.0)
    i_g = jnp.where(sel, gl, 0.0)
    u = i_g - b_al
    m_prev = m_scr[st, 0:1, :]
    c = jnp.maximum(_cummax_rows(u, reverse), m_prev)
    c3 = _split_f32(c * LOG2E, 3)
    u3 = _split_f32(u * LOG2E, 3)
    pc = c3[0] + pltpu.roll(c3[1], 16, axis=1) + pltpu.roll(c3[2], 32, axis=1)
    pu = pltpu.roll(u3[0], 48, axis=1) + pltpu.roll(u3[1], 64, axis=1) + pltpu.roll(u3[2], 80, axis=1)
    y = jnp.where(lane < 48, 1.0, pu).astype(BF16)
    xs = []
    for h in range(N_HEADS):
        l = il + h
        neg_c = (lane == l) | (lane == l + 16) | (lane == l + 32)
        ones = (lane == l + 48) | (lane == l + 64) | (lane == l + 80)
        xs.append(jnp.where(neg_c, -pc, jnp.where(ones, 1.0, 0.0)))
    e_all = _dot_nt(jnp.concatenate(xs, axis=0).astype(BF16), y)
    mask4 = jnp.concatenate([mask] * N_HEADS, axis=0)
    w_all = jnp.exp2(jnp.where(mask4, e_all, NEG))
    qk = _dot_nt(_stack_heads(qb), kb) * w_all
    num = _unstack_heads(_dot(qk.astype(BF16), vb), BLK)
    wi_p = jnp.exp(jnp.where(sel, m_prev - c, 0.0))
    em_p = jnp.exp(-(b_al + c))
    den = jnp.zeros((BLK, BRANCH_W), F32)
    w_inter = jnp.zeros((BLK, BRANCH_W), F32)
    exp_neg_m = jnp.zeros((BLK, BRANCH_W), F32)
    for h in range(N_HEADS):
        hm = _head_mask(BRANCH_W, h)
        den = jnp.where(hm, jnp.sum(qk[h * BLK:(h + 1) * BLK], axis=-1, keepdims=True), den)
        w_inter = jnp.where(hm, wi_p[:, il + h:il + h + 1], w_inter)
        exp_neg_m = jnp.where(hm, em_p[:, il + h:il + h + 1], exp_neg_m)
    ex = ex_ref[d]
    q_c = _dot(qb, c_scr[st].astype(BF16))
    q_n = _dot(qb, n_scr[st].astype(BF16))
    o_ref[e] = (num + w_inter * q_c) / jnp.maximum(jnp.abs(den + w_inter * q_n), exp_neg_m)
    b_end = b_al[end:end + 1, :]
    k_log = b_end - b_al + i_g
    m_new = jnp.maximum(b_end + m_prev, jnp.max(k_log, axis=0, keepdims=True))
    wk = jnp.where(sel, jnp.exp(k_log - m_new), 0.0)
    decay = jnp.where(sel, jnp.exp(b_end + m_prev - m_new), 0.0)
    kw = (k * _dot_split(wk, ex, 2, 1)).astype(BF16)
    decay_l = _dot_split(jnp.broadcast_to(decay, (8, 128)), ex, 3, 1)[0:1]
    r_head = lax.broadcasted_iota(jnp.int32, (BRANCH_W, BRANCH_W), 0) // (BRANCH_W // N_HEADS)
    c_head = lax.broadcasted_iota(jnp.int32, (BRANCH_W, BRANCH_W), 1) // (BRANCH_W // N_HEADS)
    diag = r_head == c_head
    c_scr[st] = decay_l * c_scr[st] + jnp.where(diag, _dot_tn(kw, vb), 0.0)
    n_sum = _dot_tn(kw, jnp.ones((BLK, BRANCH_W), BF16))
    n_scr[st] = decay_l * n_scr[st] + jnp.where(diag, n_sum, 0.0)
    m_scr[st, 0:1, :] = jnp.where(sel, m_new, 0.0)


def _mlstm_kernel(qf, kf, vf, sf, qb, kb, vb, sb, gb_ref, ex_ref, o_ref_f, o_ref_b, c_scr, n_scr, m_scr, *, group):
    @pl.when(pl.program_id(1) == 0)
    def _():
        c_scr[...] = jnp.zeros_like(c_scr)
        n_scr[...] = jnp.zeros_like(n_scr)
        m_scr[...] = jnp.zeros_like(m_scr)

    for e in range(group):
        _mlstm_chunk(e, 0, qf, kf, vf, sf, gb_ref, ex_ref, o_ref_f, c_scr, n_scr, m_scr)
        _mlstm_chunk(e, 1, qb, kb, vb, sb, gb_ref, ex_ref, o_ref_b, c_scr, n_scr, m_scr)


def _scan_specs(group, width_blocks, chunk_fn):
    return [pl.BlockSpec((group, BLK, BRANCH_W), lambda b, g, j=j: (b, chunk_fn(g), j)) for j in width_blocks]


def _mlstm_expand_matrix():
    ex = np.zeros((2, 128, BRANCH_W), np.float32)
    per = BRANCH_W // N_HEADS
    for d in range(2):
        for h in range(N_HEADS):
            ex[d, 2 * N_HEADS * d + h, h * per:(h + 1) * per] = 1.0
    return jnp.asarray(ex, dtype=BF16)


def _mlstm(ml, small, gate_b, *, layer):
    n_batch = ml.shape[0]
    group = _batch_group(n_batch, want=2)
    small_spec = lambda fn: pl.BlockSpec((group, BLK, 128), lambda b, g: (b, fn(g), 0))
    out_spec = lambda fn: pl.BlockSpec((group, BLK, BRANCH_W), lambda b, g: (b, fn(g), 0))
    return pl.pallas_call(
        functools.partial(_mlstm_kernel, group=group),
        grid=(n_batch // group, N_BLK),
        in_specs=(_scan_specs(group, (0, 1, 2), _fwd_chunk) + [small_spec(_fwd_chunk)]
                  + _scan_specs(group, (0, 1, 2), _bwd_chunk) + [small_spec(_bwd_chunk)]
                  + [pl.BlockSpec((1, 128), lambda b, g: (0, 0)),
                     pl.BlockSpec((2, 128, BRANCH_W), lambda b, g: (0, 0, 0))]),
        out_specs=[out_spec(_fwd_chunk), out_spec(_bwd_chunk)],
        out_shape=[jax.ShapeDtypeStruct((n_batch, NTOK, BRANCH_W), F32)] * 2,
        scratch_shapes=[pltpu.VMEM((2 * group, BRANCH_W, BRANCH_W), F32),
                        pltpu.VMEM((2 * group, BRANCH_W, BRANCH_W), F32),
                        pltpu.VMEM((2 * group, 8, 128), F32)],
        compiler_params=pltpu.CompilerParams(
            dimension_semantics=("arbitrary", "arbitrary"), vmem_limit_bytes=V7X_VMEM_LIMIT),
        name=f"mlstm_l{layer}",
    )(ml, ml, ml, small, ml, ml, ml, small, gate_b, _mlstm_expand_matrix())


def _gla_chunk(e, d, qk_ref, v_ref, s_ref, w2_ref, ba_ref, o_ref, st_scr):
    reverse = d == 1
    mask = _tri_mask(reverse)
    end = 0 if reverse else BLK - 1
    half = BLK // 2
    slot = 2 * e + d
    q = qk_ref[e, :, 0:128]
    k = qk_ref[e, :, 128:256]
    vb = v_ref[e].astype(BF16)
    x = _dot_split(s_ref[e], w2_ref[d], 2, 2) + ba_ref[d]
    la = jax.nn.log_sigmoid(x) * (1.0 / GLA_TAU)
    bc = _cumsum_rows(la, mask)
    ref_row = bc[half:half + 1, :]
    qe = (q * jnp.exp(bc - ref_row)).astype(BF16)
    ke = (k * jnp.exp(ref_row - bc)).astype(BF16)
    att = _dot_nt(_stack_heads(qe), ke)
    mask4 = jnp.concatenate([mask] * N_HEADS, axis=0)
    o_all = _dot(jnp.where(mask4, att, 0.0).astype(BF16), vb)
    st = st_scr[slot]
    inter = _dot_nt((q * jnp.exp(bc)).astype(BF16), st.astype(BF16))
    o_ref[e] = _unstack_heads(o_all, BLK) + inter
    b_end = bc[end:end + 1, :]
    kend = (k * jnp.exp(b_end - bc)).astype(BF16)
    r_head = lax.broadcasted_iota(jnp.int32, (BRANCH_W, 128), 0) // (BRANCH_W // N_HEADS)
    c_head = lax.broadcasted_iota(jnp.int32, (BRANCH_W, 128), 1) // GLA_DK
    st_scr[slot] = jnp.exp(b_end) * st + jnp.where(r_head == c_head, _dot_tn(vb, kend), 0.0)


def _gla_kernel(qkf, vf, sf, qkb, vb, sb, w2_ref, ba_ref, o_ref_f, o_ref_b, st_scr, *, group):
    @pl.when(pl.program_id(1) == 0)
    def _():
        st_scr[...] = jnp.zeros_like(st_scr)

    for e in range(group):
        _gla_chunk(e, 0, qkf, vf, sf, w2_ref, ba_ref, o_ref_f, st_scr)
        _gla_chunk(e, 1, qkb, vb, sb, w2_ref, ba_ref, o_ref_b, st_scr)


def _gla(gla, small, w2pad, ba, *, layer):
    n_batch = gla.shape[0]
    group = _batch_group(n_batch)
    small_spec = lambda fn: pl.BlockSpec((group, BLK, 128), lambda b, g: (b, fn(g), 0))
    out_spec = lambda fn: pl.BlockSpec((group, BLK, BRANCH_W), lambda b, g: (b, fn(g), 0))
    return pl.pallas_call(
        functools.partial(_gla_kernel, group=group),
        grid=(n_batch // group, N_BLK),
        in_specs=(_scan_specs(group, (0, 1), _fwd_chunk) + [small_spec(_fwd_chunk)]
                  + _scan_specs(group, (0, 1), _bwd_chunk) + [small_spec(_bwd_chunk)]
                  + [pl.BlockSpec((2, 128, 128), lambda b, g: (0, 0, 0)),
                     pl.BlockSpec((2, 1, 128), lambda b, g: (0, 0, 0))]),
        out_specs=[out_spec(_fwd_chunk), out_spec(_bwd_chunk)],
        out_shape=[jax.ShapeDtypeStruct((n_batch, NTOK, BRANCH_W), F32)] * 2,
        scratch_shapes=[pltpu.VMEM((2 * group, BRANCH_W, 128), F32)],
        compiler_params=pltpu.CompilerParams(
            dimension_semantics=("arbitrary", "arbitrary"), vmem_limit_bytes=V7X_VMEM_LIMIT),
        name=f"gla_l{layer}",
    )(gla, gla, small, gla, gla, small, w2pad, ba)


def _group_mean(x, gmat):
    return _dot_split(x, gmat, 2, 1)


def _merge_kernel(x_ref, sh_ref, sc_ref, gt_ref, na_ref, mlf_ref, mlb_ref, mlo_ref, glf_ref, glb_ref, glg_ref,
                  da_ref, ng_ref, gm_ref, wg_ref, wb_ref, wo_ref, lng_ref, lnb_ref, o_ref,
                  *, n_batch, tm, alpha, lambda_init):
    b = pl.program_id(0)
    t = pl.program_id(1)
    is_ctx = _is_ctx_rows(t, tm)
    x = x_ref[0]
    h = (x * (1.0 + _mod_rows(sc_ref, b, n_batch, is_ctx)) + _mod_rows(sh_ref, b, n_batch, is_ctx)).astype(BF16)
    gmat = gm_ref[...]

    def rms_norm(y, gain):
        return y * lax.rsqrt(_group_mean(y * y, gmat) + HEAD_EPS) * gain

    h_ml = mlf_ref[0] + mlb_ref[0]
    y_ml = jax.nn.sigmoid(mlo_ref[0]) * rms_norm(h_ml - _group_mean(h_ml, gmat), ng_ref[0:1, :])
    g_gl = glg_ref[0]
    y_gla = g_gl * jax.nn.sigmoid(g_gl) * rms_norm(glf_ref[0] + glb_ref[0], ng_ref[1:2, :])
    y_da = (1.0 - lambda_init) * rms_norm(da_ref[0], ng_ref[2:3, :])
    ys = [na_ref[0], y_ml.astype(BF16), y_gla.astype(BF16), y_da.astype(BF16)]
    acc = None
    for j in range(N_BRANCH):
        gate = jax.nn.sigmoid(_dot(h, wg_ref[:, j * D_MODEL:(j + 1) * D_MODEL]))
        term = gate * _dot(ys[j], wb_ref[j])
        acc = term if acc is None else acc + term
    y = _dot(acc.astype(BF16), wo_ref[...])
    y = alpha * x + _mod_rows(gt_ref, b, n_batch, is_ctx) * y
    o_ref[0] = _layer_norm(y, lng_ref[0], lnb_ref[0])


def _merge(xs, mod, na_o, ml_o, ml, gla_o, gla, da_o, norm_g, gmat, w_gates, w_branch, w_out, ln_g, ln_b,
           *, layer, n_rows, tm, alpha, lambda_init):
    n_batch = xs.shape[0]
    rows = mod.shape[1]
    ln_idx = layer * 3 + 1
    kern = functools.partial(_merge_kernel, n_batch=n_batch, tm=tm, alpha=alpha, lambda_init=lambda_init)
    tok = lambda j: pl.BlockSpec((1, tm, BRANCH_W), lambda b, t: (b, t, j))
    return pl.pallas_call(
        kern,
        grid=(n_batch, n_rows // tm),
        in_specs=[
            pl.BlockSpec((1, tm, D_MODEL), lambda b, t: (b, t, 0)),
            _mod_spec(rows, layer, 3), _mod_spec(rows, layer, 4), _mod_spec(rows, layer, 5),
            tok(0), tok(0), tok(0), tok(3), tok(0), tok(0), tok(2), tok(0),
            pl.BlockSpec((8, BRANCH_W), lambda b, t: (0, 0)),
            pl.BlockSpec((BRANCH_W, BRANCH_W), lambda b, t: (0, 0)),
            _const_spec((D_MODEL, N_BRANCH * D_MODEL), lambda b, t: (0, 0)),
            _const_spec((None, N_BRANCH, BRANCH_W, D_MODEL), lambda b, t: (layer, 0, 0, 0)),
            _const_spec((None, D_MODEL, D_MODEL), lambda b, t: (layer, 0, 0)),
            pl.BlockSpec((1, 1, D_MODEL), lambda b, t: (ln_idx, 0, 0)),
            pl.BlockSpec((1, 1, D_MODEL), lambda b, t: (ln_idx, 0, 0)),
        ],
        out_specs=pl.BlockSpec((1, tm, D_MODEL), lambda b, t: (b, t, 0)),
        out_shape=jax.ShapeDtypeStruct((n_batch, n_rows, D_MODEL), F32),
        compiler_params=pltpu.CompilerParams(
            dimension_semantics=("arbitrary", "arbitrary"), vmem_limit_bytes=V7X_VMEM_LIMIT),
        name=f"merge_l{layer}",
    )(xs, mod, mod, mod, na_o, ml_o[0], ml_o[1], ml, gla_o[0], gla_o[1], gla, da_o, norm_g, gmat,
      w_gates, w_branch, w_out, ln_g, ln_b)


def _rope_tables():
    t = jnp.arange(SEQ)
    n_f = DA_DQK // 4
    inv = ROPE_BASE ** (-jnp.arange(n_f, dtype=F32) / n_f)

    def cs(pos):
        ang = pos.astype(F32)[:, None] * inv
        return jnp.cos(ang), jnp.sin(ang)

    (cr, sr), (cc, sc) = cs(t // GRID_W), cs(t % GRID_W)
    cos32 = jnp.concatenate([cr, cr, cc, cc], axis=-1)
    sin32 = jnp.concatenate([-sr, sr, -sc, sc], axis=-1)
    reps = BRANCH_W // DA_DQK
    cos = jnp.concatenate([jnp.tile(cos32, (1, reps)), jnp.ones((CTX_LEN, BRANCH_W), F32)], axis=0)
    sin = jnp.concatenate([jnp.tile(sin32, (1, reps)), jnp.zeros((CTX_LEN, BRANCH_W), F32)], axis=0)
    return cos, sin


def _swap_perm():
    n_f = DA_DQK // 4
    base = np.concatenate([np.arange(n_f, 2 * n_f), np.arange(0, n_f),
                           np.arange(3 * n_f, 4 * n_f), np.arange(2 * n_f, 3 * n_f)])
    return np.concatenate([g * DA_DQK + base for g in range(BRANCH_W // DA_DQK)])


def _prep_mix_weights(w):
    na, ml, ml_if = w[:, 0:768], w[:, 768:1792], w[:, 1792:1808]
    gla, gla_a = w[:, 1808:2576], w[:, 2576:2608]
    dq, dk, dv = w[:, 2608:2864], w[:, 2864:3120], w[:, 3120:3376]
    gates = w[:, 3376:]
    perm = _swap_perm()
    pad = jnp.zeros((D_MODEL, 128 - ml_if.shape[1] - gla_a.shape[1]), w.dtype)
    w_all = jnp.concatenate([na, ml, gla, ml_if, gla_a, pad, dq, dq[:, perm], dk, dk[:, perm], dv], axis=1)
    return w_all.astype(BF16), gates.astype(BF16)


def _col_scale():
    cs = np.ones((1, C_SMALL), np.float32)
    cs[:, C_NA:C_NA + 256] = 64 ** -0.5 * LOG2E
    cs[:, C_ML:C_ML + 256] = 64 ** -0.5
    cs[:, C_GLA:C_GLA + 128] = GLA_DK ** -0.5
    return jnp.asarray(cs)


def _group_mean_matrix():
    head = np.arange(BRANCH_W) // (BRANCH_W // N_HEADS)
    return jnp.asarray((head[:, None] == head[None, :]) / (BRANCH_W // N_HEADS), dtype=BF16)


def _da_group_sum_matrix():
    grp = np.arange(BRANCH_W) // DA_DQK
    return jnp.asarray(grp[:, None] == grp[None, :], dtype=BF16)


def kernel(x, c, ctx, c_ctx, w_ada, b_ada, ln_g, ln_b, ffn_w_in, ffn_w_out, w_mix_in, na_rpb, ml_gate_b,
           ml_norm_g, gla_w_a2, gla_b_a, gla_norm_g, da_lambda, da_norm_g, w_branch, w_out):
    n_batch = x.shape[0]
    depth = w_ada.shape[0]
    alpha = (2 * depth) ** 0.25
    mod_rows = -(-(n_batch + 1) // 8) * 8
    cc = jnp.concatenate([c, c_ctx[None, :], jnp.zeros((mod_rows - n_batch - 1, D_MODEL), F32)], axis=0)
    mod = _ada(cc, w_ada, b_ada)

    xs = x
    w_in_bf = ffn_w_in.astype(BF16)
    w_out_bf = ffn_w_out.astype(BF16)
    w_branch_bf = w_branch.astype(BF16)
    w_o_bf = w_out.astype(BF16)
    ln_g3 = ln_g.reshape(depth * 3, 1, D_MODEL)
    ln_b3 = ln_b.reshape(depth * 3, 1, D_MODEL)
    cos_tab, sin_tab = _rope_tables()
    colscale = _col_scale()
    gmat = _group_mean_matrix()
    da_gsum = _da_group_sum_matrix()

    for l in range(depth):
        ctx_out = l < depth - 1
        lambda_init = 0.8 - 0.6 * math.exp(-0.3 * l)
        w_all, w_gates = _prep_mix_weights(w_mix_in[l])
        bias = _na_bias_tables(na_rpb[l])
        gate_b = jnp.concatenate([ml_gate_b[l], jnp.zeros((128 - ml_gate_b.shape[1],), F32)])[None, :]
        w2pad = jnp.zeros((2, 128, 128), F32)
        for d in range(2):
            lo = SMALL_A_OFF + d * GLA_RANK
            w2pad = w2pad.at[d, lo:lo + GLA_RANK, :].set(gla_w_a2[l, d])
        ba = gla_b_a[l][:, None, :]
        norm_g = jnp.concatenate([ml_norm_g[l][None], gla_norm_g[l][None], da_norm_g[l][None],
                                  jnp.zeros((5, BRANCH_W), F32)], axis=0)

        xs = _ffn(xs, mod, w_in_bf, w_out_bf, ln_g3, ln_b3, layer=l, sub=0, mod0=0,
                  n_rows=NTOK, tm=TM_ALL, alpha=alpha, ctx=ctx if l == 0 else None)
        na, na_vt, ml, gla, small, da, da_vt = _inproj(xs, mod, w_all, colscale, cos_tab, sin_tab,
                                                       layer=l, tm=TM_ALL)
        na_o = _neighbourhood(na, na_vt, bias, layer=l, with_ctx=ctx_out)
        da_o = _diff_attention(da, da_vt, da_lambda, da_gsum, layer=l, lambda_init=lambda_init,
                               with_ctx=ctx_out)
        ml_o = _mlstm(ml, small, gate_b, layer=l)
        gla_o = _gla(gla, small, w2pad, ba, layer=l)
        n_rows, tm = (NTOK, TM_ALL) if ctx_out else (SEQ, TM_LAT)
        xs = _merge(xs, mod, na_o, ml_o, ml, gla_o, gla, da_o, norm_g, gmat, w_gates, w_branch_bf, w_o_bf,
                    ln_g3, ln_b3, layer=l, n_rows=n_rows, tm=tm, alpha=alpha, lambda_init=lambda_init)
        xs = _ffn(xs, mod, w_in_bf, w_out_bf, ln_g3, ln_b3, layer=l, sub=1, mod0=6,
                  n_rows=n_rows, tm=tm, alpha=alpha)
    return xs
```

```python
import functools
import math

import jax
import jax.numpy as jnp
import numpy as np
from jax import lax
from jax.experimental import pallas as pl
from jax.experimental.pallas import tpu as pltpu

F32 = jnp.float32
BF16 = jnp.bfloat16

D_MODEL = 1024
SEQ = 2048
CTX_LEN = 256
NTOK = SEQ + CTX_LEN
GRID_W = 64
GRID_ROWS = SEQ // GRID_W
N_MOD = 9
D_FF = 2816
FFN_HALF = 0.5
N_BRANCH = 4
BRANCH_W = 256
N_HEADS = 4
NA_KH = 8
NA_KW = 16
ML_F_OFF = 4
GLA_DK = 32
GLA_RANK = 16
GLA_TAU = 16.0
DA_DQK = 32
ROPE_BASE = 10000.0
LN_EPS = 1e-5
HEAD_EPS = 1e-6
NEG = -1e30
LOG2E = math.log2(math.e)
DA_SAFE_LOG2 = 100.0
DA_BOUND_SLACK = 1.1

V7X_VMEM_LIMIT = 56 * 1024 * 1024
FF_CHUNK = 256
BLK = 256
TM_ALL = 3 * BLK
TM_LAT = 2 * BLK
N_BLK = NTOK // BLK
N_LAT_BLK = SEQ // BLK
NA_BAND_ROWS = 12
NA_BAND = NA_BAND_ROWS * GRID_W
NA_QROWS = BLK // GRID_W

C_NA = 0
C_ML = 768
C_GLA = 1792
C_SMALL = 2560
C_DA = 2688
N_PROJ = 3968
SMALL_A_OFF = 16


def _dot(a, b):
    return jnp.dot(a, b, preferred_element_type=F32)


def _dot_nt(a, b):
    return lax.dot_general(a, b, (((1,), (1,)), ((), ())), preferred_element_type=F32)


def _dot_tn(a, b):
    return lax.dot_general(a, b, (((0,), (0,)), ((), ())), preferred_element_type=F32)


def _split_f32(x, n):
    parts = []
    r = x
    for i in range(n):
        p = r.astype(BF16).astype(F32)
        parts.append(p)
        if i + 1 < n:
            r = r - p
    return parts


def _split(x, n):
    return [p.astype(BF16) for p in _split_f32(x, n)]


def _dot_split(a, b, na, nb, dot=_dot):
    pa = _split(a, na) if na > 1 else [a.astype(BF16)]
    pb = _split(b, nb) if nb > 1 else [b.astype(BF16)]
    acc = None
    for i, x in enumerate(pa):
        for j, y in enumerate(pb):
            if i + j < max(na, nb):
                t = dot(x, y)
                acc = t if acc is None else acc + t
    return acc


def _layer_norm(y, g, b):
    mu = jnp.mean(y, axis=-1, keepdims=True)
    yc = y - mu
    var = jnp.mean(yc * yc, axis=-1, keepdims=True)
    return yc * lax.rsqrt(var + LN_EPS) * g + b


def _mod_rows(ref, b, n_batch, is_ctx):
    lat = ref[0, pl.ds(b, 1), :]
    ctx = ref[0, n_batch:n_batch + 1, :]
    return jnp.where(is_ctx, ctx, lat)


def _is_ctx_rows(t, tm):
    return (t * tm + lax.broadcasted_iota(jnp.int32, (tm, 1), 0)) >= SEQ


def _head_mask(width, head, n=1):
    lane = lax.broadcasted_iota(jnp.int32, (n, width), 1)
    per = width // N_HEADS
    return (lane >= head * per) & (lane < (head + 1) * per)


def _stack_heads(x):
    w = x.shape[-1]
    return jnp.concatenate([jnp.where(_head_mask(w, h), x, jnp.zeros_like(x)) for h in range(N_HEADS)], axis=0)


def _batch_group(n_batch, want=4):
    return next(g for g in (4, 2, 1) if g <= want and n_batch % g == 0)


def _unstack_heads(x, m):
    w = x.shape[-1]
    out = jnp.zeros((m, w), x.dtype)
    for h in range(N_HEADS):
        out = jnp.where(_head_mask(w, h), x[h * m:(h + 1) * m], out)
    return out


def _ada_kernel(c_ref, w_ref, b_ref, o_ref):
    c = c_ref[...]
    s = c * jax.nn.sigmoid(c)
    o_ref[0] = _dot_split(s, w_ref[0], 2, 2) + b_ref[0]


def _ada(cc, w_ada, b_ada):
    depth = w_ada.shape[0]
    rows = cc.shape[0]
    return pl.pallas_call(
        _ada_kernel,
        grid=(depth, N_MOD),
        in_specs=[
            pl.BlockSpec((rows, D_MODEL), lambda l, k: (0, 0)),
            pl.BlockSpec((1, D_MODEL, D_MODEL), lambda l, k: (l, 0, k)),
            pl.BlockSpec((1, 1, D_MODEL), lambda l, k: (l, 0, k)),
        ],
        out_specs=pl.BlockSpec((1, rows, D_MODEL), lambda l, k: (l, 0, k)),
        out_shape=jax.ShapeDtypeStruct((depth, rows, N_MOD * D_MODEL), F32),
        compiler_params=pltpu.CompilerParams(
            dimension_semantics=("arbitrary", "arbitrary"), vmem_limit_bytes=V7X_VMEM_LIMIT),
        name="ada_mod",
    )(cc, w_ada, b_ada.reshape(depth, 1, N_MOD * D_MODEL))


def _ffn_kernel(*refs, n_batch, tm, alpha, n_x):
    x_refs = refs[:n_x]
    sh_ref, sc_ref, gt_ref, w_in_ref, w_out_ref, lng_ref, lnb_ref, o_ref, g_scr = refs[n_x:]
    b = pl.program_id(0)
    t = pl.program_id(1)
    is_ctx = _is_ctx_rows(t, tm)
    if n_x == 1:
        x = x_refs[0][0]
    else:
        parts = [r[0] for r in x_refs[:-1]]
        parts[-1] = jnp.where(t == pl.num_programs(1) - 1, x_refs[-1][0], parts[-1])
        x = jnp.concatenate(parts, axis=0)
    h = (x * (1.0 + _mod_rows(sc_ref, b, n_batch, is_ctx)) + _mod_rows(sh_ref, b, n_batch, is_ctx)).astype(BF16)
    for j in range(D_FF // FF_CHUNK):
        lo = j * FF_CHUNK
        a = _dot(h, w_in_ref[:, lo:lo + FF_CHUNK])
        v = _dot(h, w_in_ref[:, D_FF + lo:D_FF + lo + FF_CHUNK])
        g_scr[:, lo:lo + FF_CHUNK] = (a * jax.nn.sigmoid(a) * v).astype(BF16)
    y = _dot(g_scr[...], w_out_ref[...])
    y = alpha * x + FFN_HALF * _mod_rows(gt_ref, b, n_batch, is_ctx) * y
    o_ref[0] = _layer_norm(y, lng_ref[0], lnb_ref[0])


def _const_spec(shape, index_map):
    return pl.BlockSpec(shape, index_map, pipeline_mode=pl.Buffered(1))


def _mod_spec(rows, layer, k):
    return pl.BlockSpec((1, rows, D_MODEL), lambda b, t: (layer, 0, k))


def _ffn(xs, mod, w_in, w_out, ln_g, ln_b, *, layer, sub, mod0, n_rows, tm, alpha, ctx=None):
    n_batch = xs.shape[0]
    rows = mod.shape[1]
    ln_idx = layer * 3 + (0 if sub == 0 else 2)
    if ctx is None:
        x_args = [xs]
        x_specs = [pl.BlockSpec((1, tm, D_MODEL), lambda b, t: (b, t, 0))]
    else:
        per = tm // BLK
        x_args = [xs] * per + [ctx]
        x_specs = [pl.BlockSpec((1, BLK, D_MODEL), lambda b, t, j=j: (b, jnp.minimum(t * per + j, N_LAT_BLK - 1), 0))
                   for j in range(per)]
        x_specs.append(pl.BlockSpec((1, CTX_LEN, D_MODEL), lambda b, t: (b, 0, 0)))
    kern = functools.partial(_ffn_kernel, n_batch=n_batch, tm=tm, alpha=alpha, n_x=len(x_args))
    return pl.pallas_call(
        kern,
        grid=(n_batch, n_rows // tm),
        in_specs=x_specs + [
            _mod_spec(rows, layer, mod0), _mod_spec(rows, layer, mod0 + 1), _mod_spec(rows, layer, mod0 + 2),
            _const_spec((None, None, D_MODEL, 2 * D_FF), lambda b, t: (layer, sub, 0, 0)),
            _const_spec((None, None, D_FF, D_MODEL), lambda b, t: (layer, sub, 0, 0)),
            pl.BlockSpec((1, 1, D_MODEL), lambda b, t: (ln_idx, 0, 0)),
            pl.BlockSpec((1, 1, D_MODEL), lambda b, t: (ln_idx, 0, 0)),
        ],
        out_specs=pl.BlockSpec((1, tm, D_MODEL), lambda b, t: (b, t, 0)),
        out_shape=jax.ShapeDtypeStruct((n_batch, n_rows, D_MODEL), F32),
        scratch_shapes=[pltpu.VMEM((tm, D_FF), BF16)],
        compiler_params=pltpu.CompilerParams(
            dimension_semantics=("arbitrary", "arbitrary"), vmem_limit_bytes=V7X_VMEM_LIMIT),
        name=f"ffn_l{layer}_s{sub}",
    )(*x_args, mod, mod, mod, w_in, w_out, ln_g, ln_b)


def _inproj_kernel(x_ref, sh_ref, sc_ref, w_ref, cs_ref, cos_ref, sin_ref,
                   na_ref, nvt_ref, ml_ref, gla_ref, sm_ref, da_ref, dvt_ref, *, n_batch, tm):
    b = pl.program_id(0)
    t = pl.program_id(1)
    is_ctx = _is_ctx_rows(t, tm)
    x = x_ref[0]
    h = (x * (1.0 + _mod_rows(sc_ref, b, n_batch, is_ctx)) + _mod_rows(sh_ref, b, n_batch, is_ctx)).astype(BF16)
    p_na = _dot(h, w_ref[:, C_NA:C_ML]) * cs_ref[:, C_NA:C_ML]
    na_ref[0] = p_na[:, 0:2 * BRANCH_W].astype(BF16)
    nvt_ref[0] = p_na[:, 2 * BRANCH_W:3 * BRANCH_W].T.astype(BF16)
    ml_ref[0] = _dot(h, w_ref[:, C_ML:C_GLA]) * cs_ref[:, C_ML:C_GLA]
    gla_ref[0] = _dot(h, w_ref[:, C_GLA:C_SMALL]) * cs_ref[:, C_GLA:C_SMALL]
    sm_ref[0] = _dot(h, w_ref[:, C_SMALL:C_DA])
    pd = _dot(h, w_ref[:, C_DA:N_PROJ])
    cos = cos_ref[...]
    sin = sin_ref[...]
    w = BRANCH_W
    da_ref[0, :, 0:w] = ((pd[:, 0:w] * cos + pd[:, w:2 * w] * sin) * (DA_DQK ** -0.5 * LOG2E)).astype(BF16)
    da_ref[0, :, w:2 * w] = (pd[:, 2 * w:3 * w] * cos + pd[:, 3 * w:4 * w] * sin).astype(BF16)
    dvt_ref[0] = pd[:, 4 * w:5 * w].T.astype(BF16)


def _inproj(xs, mod, w_all, colscale, cos_tab, sin_tab, *, layer, tm):
    n_batch = xs.shape[0]
    rows = mod.shape[1]
    kern = functools.partial(_inproj_kernel, n_batch=n_batch, tm=tm)

    def out(width, dtype):
        return (pl.BlockSpec((1, tm, width), lambda b, t: (b, t, 0)),
                jax.ShapeDtypeStruct((n_batch, NTOK, width), dtype))

    transposed = (pl.BlockSpec((1, BRANCH_W, tm), lambda b, t: (b, 0, t)),
                  jax.ShapeDtypeStruct((n_batch, BRANCH_W, NTOK), BF16))
    outs = [out(512, BF16), transposed, out(1024, F32), out(768, F32), out(128, F32), out(512, BF16), transposed]
    return pl.pallas_call(
        kern,
        grid=(n_batch, NTOK // tm),
        in_specs=[
            pl.BlockSpec((1, tm, D_MODEL), lambda b, t: (b, t, 0)),
            _mod_spec(rows, layer, 3), _mod_spec(rows, layer, 4),
            _const_spec((D_MODEL, N_PROJ), lambda b, t: (0, 0)),
            pl.BlockSpec((1, C_SMALL), lambda b, t: (0, 0)),
            pl.BlockSpec((tm, BRANCH_W), lambda b, t: (t, 0)),
            pl.BlockSpec((tm, BRANCH_W), lambda b, t: (t, 0)),
        ],
        out_specs=[o[0] for o in outs],
        out_shape=[o[1] for o in outs],
        compiler_params=pltpu.CompilerParams(
            dimension_semantics=("arbitrary", "arbitrary"), vmem_limit_bytes=V7X_VMEM_LIMIT),
        name=f"inproj_l{layer}",
    )(xs, mod, mod, w_all, colscale, cos_tab, sin_tab)


def _softmax_pv_t(parts, vals_t):
    dv = BRANCH_W // N_HEADS
    m = None
    for s in parts:
        mi = jnp.max(s, axis=0, keepdims=True)
        m = mi if m is None else jnp.maximum(m, mi)
    ps = [jnp.exp2(s - m) for s in parts]
    l = sum(jnp.sum(p, axis=0, keepdims=True) for p in ps)
    r = 1.0 / l
    outs = []
    for h in range(N_HEADS):
        qs_h = slice(h * BLK, (h + 1) * BLK)
        o = sum(_dot(vt[h * dv:(h + 1) * dv, :], p[:, qs_h].astype(BF16)) for p, vt in zip(ps, vals_t))
        outs.append(o * r[:, qs_h])
    return jnp.concatenate(outs, axis=0).T


def _na_kernel(q_ref, k_ref, vt_ref, bias_ref, o_ref, *, with_ctx, group):
    i = pl.program_id(0)

    @pl.when(i < N_LAT_BLK)
    def _():
        start = pl.multiple_of(jnp.clip(i - 1, 0, N_LAT_BLK - NA_BAND // BLK) * BLK, BLK)
        bias = bias_ref[...]
        for e in range(group):
            qs = _stack_heads(q_ref[e])
            s_loc = _dot_nt(k_ref[e, pl.ds(start, NA_BAND), :], qs) + bias
            s_ctx = _dot_nt(k_ref[e, SEQ:NTOK, :], qs)
            vts = [vt_ref[e, :, pl.ds(start, NA_BAND)], vt_ref[e, :, SEQ:NTOK]]
            o_ref[e] = _softmax_pv_t([s_loc, s_ctx], vts).astype(BF16)

    if with_ctx:
        @pl.when(i == N_LAT_BLK)
        def _():
            for e in range(group):
                qs = _stack_heads(q_ref[e])
                s_ctx = _dot_nt(k_ref[e, SEQ:NTOK, :], qs)
                o_ref[e] = _softmax_pv_t([s_ctx], [vt_ref[e, :, SEQ:NTOK]]).astype(BF16)


def _na_pattern(i):
    return jnp.minimum(i, 1) + (i >= N_LAT_BLK - 2).astype(jnp.int32) + (i >= N_LAT_BLK - 1).astype(jnp.int32)


def _neighbourhood(na, na_vt, bias, *, layer, with_ctx):
    n_batch = na.shape[0]
    n_blk = N_BLK if with_ctx else N_LAT_BLK
    group = _batch_group(n_batch)
    return pl.pallas_call(
        functools.partial(_na_kernel, with_ctx=with_ctx, group=group),
        grid=(n_blk, n_batch // group),
        in_specs=[
            pl.BlockSpec((group, BLK, BRANCH_W), lambda i, b: (b, i, 0)),
            pl.BlockSpec((group, NTOK, BRANCH_W), lambda i, b: (b, 0, 1)),
            pl.BlockSpec((group, BRANCH_W, NTOK), lambda i, b: (b, 0, 0)),
            pl.BlockSpec((None, NA_BAND, N_HEADS * BLK),
                         lambda i, b: (_na_pattern(jnp.minimum(i, N_LAT_BLK - 1)), 0, 0)),
        ],
        out_specs=pl.BlockSpec((group, BLK, BRANCH_W), lambda i, b: (b, i, 0)),
        out_shape=jax.ShapeDtypeStruct((n_batch, n_blk * BLK, BRANCH_W), BF16),
        compiler_params=pltpu.CompilerParams(
            dimension_semantics=("arbitrary", "arbitrary"), vmem_limit_bytes=V7X_VMEM_LIMIT),
        name=f"na_l{layer}",
    )(na, na, na_vt, bias)


def _na_bias_tables(rpb):
    blocks = np.array([0, 1, N_LAT_BLK - 2, N_LAT_BLK - 1])
    qr = blocks[:, None] * NA_QROWS + np.arange(NA_QROWS)[None, :]
    r0 = np.clip(qr - NA_KH // 2, 0, GRID_ROWS - NA_KH)
    band0 = np.clip(blocks - 1, 0, N_LAT_BLK - NA_BAND // BLK) * NA_QROWS
    kr = band0[:, None] + np.arange(NA_BAND_ROWS)[None, :]
    row_ok = (kr[:, None, :] >= r0[:, :, None]) & (kr[:, None, :] < r0[:, :, None] + NA_KH)
    dr = np.clip(kr[:, None, :] - qr[:, :, None] + NA_KH - 1, 0, 2 * NA_KH - 2)
    col = np.arange(GRID_W)
    c0 = np.clip(col - NA_KW // 2, 0, GRID_W - NA_KW)
    col_ok = (col[None, :] >= c0[:, None]) & (col[None, :] < c0[:, None] + NA_KW)
    dc = np.clip(col[None, :] - col[:, None] + NA_KW - 1, 0, 2 * NA_KW - 2)
    ok = row_ok[:, :, None, :, None] & col_ok[None, None, :, None, :]
    oh_c = jnp.asarray(dc[:, :, None] == np.arange(2 * NA_KW - 1), dtype=F32)
    oh_r = jnp.asarray(dr[..., None] == np.arange(2 * NA_KH - 1), dtype=F32)
    by_col = jnp.einsum("hrc,xyc->hrxy", rpb, oh_c, precision=lax.Precision.HIGHEST)
    vals = jnp.einsum("pqkr,hrxy->pkyhqx", oh_r, by_col, precision=lax.Precision.HIGHEST)
    ok_t = ok.transpose(0, 3, 4, 1, 2)[:, :, :, None, :, :]
    tab = jnp.where(ok_t, vals * LOG2E, NEG)
    return tab.reshape(4, NA_BAND, N_HEADS * BLK)


def _da_kernel(q_ref, k_ref, vt_ref, qall_ref, lam_ref, g_ref, o_ref, safe_scr, *, lambda_init, with_ctx,
               group):
    i = pl.program_id(1)
    lp = lam_ref[...]
    lam = (jnp.exp(jnp.sum(lp[0:1] * lp[1:2], axis=-1, keepdims=True))
           - jnp.exp(jnp.sum(lp[2:3] * lp[3:4], axis=-1, keepdims=True)) + lambda_init)
    lane = lax.broadcasted_iota(jnp.int32, (1, BRANCH_W), 1)

    @pl.when(i == 0)
    def _():
        def max_sq_norm(ref, e):
            f = ref[e].astype(F32)
            return jnp.max(_dot((f * f).astype(BF16), g_ref[...]), axis=0, keepdims=True)

        bound2 = functools.reduce(
            jnp.maximum, [jnp.max(max_sq_norm(qall_ref, e) * max_sq_norm(k_ref, e)) for e in range(group)])
        safe_scr[0] = (bound2 * DA_BOUND_SLACK <= DA_SAFE_LOG2 ** 2).astype(jnp.int32)

    safe = safe_scr[0] == 1

    def attend(q, k, vt, stabilise):
        dv = BRANCH_W // N_HEADS
        outs = []
        for h in range(N_HEADS):
            lo = h * 2 * DA_DQK
            q1 = jnp.where((lane >= lo) & (lane < lo + DA_DQK), q, jnp.zeros_like(q))
            q2 = jnp.where((lane >= lo + DA_DQK) & (lane < lo + 2 * DA_DQK), q, jnp.zeros_like(q))
            st = _dot_nt(k, jnp.concatenate([q1, q2], axis=0))
            if stabilise:
                st = st - jnp.max(st, axis=0, keepdims=True)
            pt = jnp.exp2(st)
            r = 1.0 / jnp.sum(pt, axis=0, keepdims=True)
            ot = _dot(vt[h * dv:(h + 1) * dv, :], pt.astype(BF16))
            outs.append(ot[:, :BLK] * r[:, :BLK] - ot[:, BLK:] * (lam * r[:, BLK:]))
        return jnp.concatenate(outs, axis=0).T

    def both(k_fn, vt_fn):
        for stabilise in (False, True):
            @pl.when(safe != stabilise)
            def _():
                for e in range(group):
                    o_ref[e] = attend(q_ref[e], k_fn(e), vt_fn(e), stabilise)

    @pl.when(i < N_LAT_BLK)
    def _():
        both(lambda e: k_ref[e], lambda e: vt_ref[e])

    if with_ctx:
        @pl.when(i == N_LAT_BLK)
        def _():
            both(lambda e: k_ref[e, SEQ:NTOK, :], lambda e: vt_ref[e, :, SEQ:NTOK])


def _diff_attention(da, da_vt, lam_p, gsum, *, layer, lambda_init, with_ctx):
    n_batch = da.shape[0]
    n_blk = N_BLK if with_ctx else N_LAT_BLK
    group = _batch_group(n_batch, want=2)
    return pl.pallas_call(
        functools.partial(_da_kernel, lambda_init=lambda_init, with_ctx=with_ctx, group=group),
        grid=(n_batch // group, n_blk),
        in_specs=[
            pl.BlockSpec((group, BLK, BRANCH_W), lambda b, i: (b, i, 0)),
            pl.BlockSpec((group, NTOK, BRANCH_W), lambda b, i: (b, 0, 1)),
            pl.BlockSpec((group, BRANCH_W, NTOK), lambda b, i: (b, 0, 0)),
            pl.BlockSpec((group, NTOK, BRANCH_W), lambda b, i: (b, 0, 0)),
            pl.BlockSpec((None, 4, DA_DQK), lambda b, i: (layer, 0, 0)),
            pl.BlockSpec((BRANCH_W, BRANCH_W), lambda b, i: (0, 0)),
        ],
        out_specs=pl.BlockSpec((group, BLK, BRANCH_W), lambda b, i: (b, i, 0)),
        out_shape=jax.ShapeDtypeStruct((n_batch, n_blk * BLK, BRANCH_W), F32),
        scratch_shapes=[pltpu.SMEM((1,), jnp.int32)],
        compiler_params=pltpu.CompilerParams(
            dimension_semantics=("arbitrary", "arbitrary"), vmem_limit_bytes=V7X_VMEM_LIMIT),
        name=f"da_l{layer}",
    )(da, da, da_vt, da, lam_p, gsum)


def _fwd_chunk(g):
    return jnp.where(g == 0, N_LAT_BLK, g - 1)


def _bwd_chunk(g):
    return jnp.where(g == 0, N_LAT_BLK, N_LAT_BLK - g)


def _tri_mask(reverse):
    r = lax.broadcasted_iota(jnp.int32, (BLK, BLK), 0)
    c = lax.broadcasted_iota(jnp.int32, (BLK, BLK), 1)
    return (c >= r) if reverse else (c <= r)


def _cumsum_rows(x, mask):
    tri = jnp.where(mask, 1.0, 0.0).astype(BF16)
    return _dot_split(tri, x, 1, 3)


def _cummax_rows(x, reverse):
    n = x.shape[0]
    row = lax.broadcasted_iota(jnp.int32, (n, 1), 0)
    sh = 1
    while sh < n:
        if reverse:
            moved, ok = pltpu.roll(x, n - sh, axis=0), row < n - sh
        else:
            moved, ok = pltpu.roll(x, sh, axis=0), row >= sh
        x = jnp.maximum(x, jnp.where(ok, moved, NEG))
        sh *= 2
    return x


def _mlstm_chunk(e, d, q_ref, k_ref, v_ref, s_ref, gb_ref, ex_ref, o_ref, c_scr, n_scr, m_scr):
    reverse = d == 1
    mask = _tri_mask(reverse)
    end = 0 if reverse else BLK - 1
    st = 2 * e + d
    q = q_ref[e]
    k = k_ref[e]
    v = v_ref[e]
    qb = q.astype(BF16)
    kb = k.astype(BF16)
    vb = v.astype(BF16)
    il = 2 * N_HEADS * d
    lane = lax.broadcasted_iota(jnp.int32, (1, 128), 1)
    sel = (lane >= il) & (lane < il + N_HEADS)
    is_f = (lane >= il + ML_F_OFF) & (lane < il + ML_F_OFF + N_HEADS)
    gs = s_ref[e] + gb_ref[...]
    gl = jnp.where(is_f, jax.nn.log_sigmoid(gs), gs)
    bc = _cumsum_rows(gl, mask)
    b_al = pltpu.roll(bc, 128 - ML_F_OFF, axis=1)
    b_al = jnp.where(sel, b_al, 0.0)
    i_g = jnp.where(sel, gl, 0.0)
    u = i_g - b_al
    m_prev = m_scr[st, 0:1, :]
    c = jnp.maximum(_cummax_rows(u, reverse), m_prev)
    c3 = _split_f32(c * LOG2E, 3)
    u3 = _split_f32(u * LOG2E, 3)
    pc = c3[0] + pltpu.roll(c3[1], 16, axis=1) + pltpu.roll(c3[2], 32, axis=1)
    pu = pltpu.roll(u3[0], 48, axis=1) + pltpu.roll(u3[1], 64, axis=1) + pltpu.roll(u3[2], 80, axis=1)
    y = jnp.where(lane < 48, 1.0, pu).astype(BF16)
    xs = []
    for h in range(N_HEADS):
        l = il + h
        neg_c = (lane == l) | (lane == l + 16) | (lane == l + 32)
        ones = (lane == l + 48) | (lane == l + 64) | (lane == l + 80)
        xs.append(jnp.where(neg_c, -pc, jnp.where(ones, 1.0, 0.0)))
    e_all = _dot_nt(jnp.concatenate(xs, axis=0).astype(BF16), y)
    mask4 = jnp.concatenate([mask] * N_HEADS, axis=0)
    w_all = jnp.exp2(jnp.where(mask4, e_all, NEG))
    qk = _dot_nt(_stack_heads(qb), kb) * w_all
    num = _unstack_heads(_dot(qk.astype(BF16), vb), BLK)
    wi_p = jnp.exp(jnp.where(sel, m_prev - c, 0.0))
    em_p = jnp.exp(-(b_al + c))
    den = jnp.zeros((BLK, BRANCH_W), F32)
    w_inter = jnp.zeros((BLK, BRANCH_W), F32)
    exp_neg_m = jnp.zeros((BLK, BRANCH_W), F32)
    for h in range(N_HEADS):
        hm = _head_mask(BRANCH_W, h)
        den = jnp.where(hm, jnp.sum(qk[h * BLK:(h + 1) * BLK], axis=-1, keepdims=True), den)
        w_inter = jnp.where(hm, wi_p[:, il + h:il + h + 1], w_inter)
        exp_neg_m = jnp.where(hm, em_p[:, il + h:il + h + 1], exp_neg_m)
    ex = ex_ref[d]
    q_c = _dot(qb, c_scr[st].astype(BF16))
    q_n = _dot(qb, n_scr[st].astype(BF16))
    o_ref[e] = (num + w_inter * q_c) / jnp.maximum(jnp.abs(den + w_inter * q_n), exp_neg_m)
    b_end = b_al[end:end + 1, :]
    k_log = b_end - b_al + i_g
    m_new = jnp.maximum(b_end + m_prev, jnp.max(k_log, axis=0, keepdims=True))
    wk = jnp.where(sel, jnp.exp(k_log - m_new), 0.0)
    decay = jnp.where(sel, jnp.exp(b_end + m_prev - m_new), 0.0)
    kw = (k * _dot_split(wk, ex, 2, 1)).astype(BF16)
    decay_l = _dot_split(jnp.broadcast_to(decay, (8, 128)), ex, 3, 1)[0:1]
    r_head = lax.broadcasted_iota(jnp.int32, (BRANCH_W, BRANCH_W), 0) // (BRANCH_W // N_HEADS)
    c_head = lax.broadcasted_iota(jnp.int32, (BRANCH_W, BRANCH_W), 1) // (BRANCH_W // N_HEADS)
    diag = r_head == c_head
    c_scr[st] = decay_l * c_scr[st] + jnp.where(diag, _dot_tn(kw, vb), 0.0)
    n_sum = _dot_tn(kw, jnp.ones((BLK, BRANCH_W), BF16))
    n_scr[st] = decay_l * n_scr[st] + jnp.where(diag, n_sum, 0.0)
    m_scr[st, 0:1, :] = jnp.where(sel, m_new, 0.0)


def _mlstm_kernel(qf, kf, vf, sf, qb, kb, vb, sb, gb_ref, ex_ref, o_ref_f, o_ref_b, c_scr, n_scr, m_scr, *, group):
    @pl.when(pl.program_id(1) == 0)
    def _():
        c_scr[...] = jnp.zeros_like(c_scr)
        n_scr[...] = jnp.zeros_like(n_scr)
        m_scr[...] = jnp.zeros_like(m_scr)

    for e in range(group):
        _mlstm_chunk(e, 0, qf, kf, vf, sf, gb_ref, ex_ref, o_ref_f, c_scr, n_scr, m_scr)
        _mlstm_chunk(e, 1, qb, kb, vb, sb, gb_ref, ex_ref, o_ref_b, c_scr, n_scr, m_scr)


def _scan_specs(group, width_blocks, chunk_fn):
    return [pl.BlockSpec((group, BLK, BRANCH_W), lambda b, g, j=j: (b, chunk_fn(g), j)) for j in width_blocks]


def _mlstm_expand_matrix():
    ex = np.zeros((2, 128, BRANCH_W), np.float32)
    per = BRANCH_W // N_HEADS
    for d in range(2):
        for h in range(N_HEADS):
            ex[d, 2 * N_HEADS * d + h, h * per:(h + 1) * per] = 1.0
    return jnp.asarray(ex, dtype=BF16)


def _mlstm(ml, small, gate_b, *, layer):
    n_batch = ml.shape[0]
    group = _batch_group(n_batch, want=2)
    small_spec = lambda fn: pl.BlockSpec((group, BLK, 128), lambda b, g: (b, fn(g), 0))
    out_spec = lambda fn: pl.BlockSpec((group, BLK, BRANCH_W), lambda b, g: (b, fn(g), 0))
    return pl.pallas_call(
        functools.partial(_mlstm_kernel, group=group),
        grid=(n_batch // group, N_BLK),
        in_specs=(_scan_specs(group, (0, 1, 2), _fwd_chunk) + [small_spec(_fwd_chunk)]
                  + _scan_specs(group, (0, 1, 2), _bwd_chunk) + [small_spec(_bwd_chunk)]
                  + [pl.BlockSpec((1, 128), lambda b, g: (0, 0)),
                     pl.BlockSpec((2, 128, BRANCH_W), lambda b, g: (0, 0, 0))]),
        out_specs=[out_spec(_fwd_chunk), out_spec(_bwd_chunk)],
        out_shape=[jax.ShapeDtypeStruct((n_batch, NTOK, BRANCH_W), F32)] * 2,
        scratch_shapes=[pltpu.VMEM((2 * group, BRANCH_W, BRANCH_W), F32),
                        pltpu.VMEM((2 * group, BRANCH_W, BRANCH_W), F32),
                        pltpu.VMEM((2 * group, 8, 128), F32)],
        compiler_params=pltpu.CompilerParams(
            dimension_semantics=("arbitrary", "arbitrary"), vmem_limit_bytes=V7X_VMEM_LIMIT),
        name=f"mlstm_l{layer}",
    )(ml, ml, ml, small, ml, ml, ml, small, gate_b, _mlstm_expand_matrix())


def _gla_chunk(e, d, qk_ref, v_ref, s_ref, w2_ref, ba_ref, o_ref, st_scr):
    reverse = d == 1
    mask = _tri_mask(reverse)
    end = 0 if reverse else BLK - 1
    half = BLK // 2
    slot = 2 * e + d
    q = qk_ref[e, :, 0:128]
    k = qk_ref[e, :, 128:256]
    vb = v_ref[e].astype(BF16)
    x = _dot_split(s_ref[e], w2_ref[d], 2, 2) + ba_ref[d]
    la = jax.nn.log_sigmoid(x) * (1.0 / GLA_TAU)
    bc = _cumsum_rows(la, mask)
    ref_row = bc[half:half + 1, :]
    qe = (q * jnp.exp(bc - ref_row)).astype(BF16)
    ke = (k * jnp.exp(ref_row - bc)).astype(BF16)
    att = _dot_nt(_stack_heads(qe), ke)
    mask4 = jnp.concatenate([mask] * N_HEADS, axis=0)
    o_all = _dot(jnp.where(mask4, att, 0.0).astype(BF16), vb)
    st = st_scr[slot]
    inter = _dot_nt((q * jnp.exp(bc)).astype(BF16), st.astype(BF16))
    o_ref[e] = _unstack_heads(o_all, BLK) + inter
    b_end = bc[end:end + 1, :]
    kend = (k * jnp.exp(b_end - bc)).astype(BF16)
    r_head = lax.broadcasted_iota(jnp.int32, (BRANCH_W, 128), 0) // (BRANCH_W // N_HEADS)
    c_head = lax.broadcasted_iota(jnp.int32, (BRANCH_W, 128), 1) // GLA_DK
    st_scr[slot] = jnp.exp(b_end) * st + jnp.where(r_head == c_head, _dot_tn(vb, kend), 0.0)


def _gla_kernel(qkf, vf, sf, qkb, vb, sb, w2_ref, ba_ref, o_ref_f, o_ref_b, st_scr, *, group):
    @pl.when(pl.program_id(1) == 0)
    def _():
        st_scr[...] = jnp.zeros_like(st_scr)

    for e in range(group):
        _gla_chunk(e, 0, qkf, vf, sf, w2_ref, ba_ref, o_ref_f, st_scr)
        _gla_chunk(e, 1, qkb, vb, sb, w2_ref, ba_ref, o_ref_b, st_scr)


def _gla(gla, small, w2pad, ba, *, layer):
    n_batch = gla.shape[0]
    group = _batch_group(n_batch)
    small_spec = lambda fn: pl.BlockSpec((group, BLK, 128), lambda b, g: (b, fn(g), 0))
    out_spec = lambda fn: pl.BlockSpec((group, BLK, BRANCH_W), lambda b, g: (b, fn(g), 0))
    return pl.pallas_call(
        functools.partial(_gla_kernel, group=group),
        grid=(n_batch // group, N_BLK),
        in_specs=(_scan_specs(group, (0, 1), _fwd_chunk) + [small_spec(_fwd_chunk)]
                  + _scan_specs(group, (0, 1), _bwd_chunk) + [small_spec(_bwd_chunk)]
                  + [pl.BlockSpec((2, 128, 128), lambda b, g: (0, 0, 0)),
                     pl.BlockSpec((2, 1, 128), lambda b, g: (0, 0, 0))]),
        out_specs=[out_spec(_fwd_chunk), out_spec(_bwd_chunk)],
        out_shape=[jax.ShapeDtypeStruct((n_batch, NTOK, BRANCH_W), F32)] * 2,
        scratch_shapes=[pltpu.VMEM((2 * group, BRANCH_W, 128), F32)],
        compiler_params=pltpu.CompilerParams(
            dimension_semantics=("arbitrary", "arbitrary"), vmem_limit_bytes=V7X_VMEM_LIMIT),
        name=f"gla_l{layer}",
    )(gla, gla, small, gla, gla, small, w2pad, ba)


def _group_mean(x, gmat):
    return _dot_split(x, gmat, 2, 1)


def _merge_kernel(x_ref, sh_ref, sc_ref, gt_ref, na_ref, mlf_ref, mlb_ref, mlo_ref, glf_ref, glb_ref, glg_ref,
                  da_ref, ng_ref, gm_ref, wg_ref, wb_ref, wo_ref, lng_ref, lnb_ref, o_ref,
                  *, n_batch, tm, alpha, lambda_init):
    b = pl.program_id(0)
    t = pl.program_id(1)
    is_ctx = _is_ctx_rows(t, tm)
    x = x_ref[0]
    h = (x * (1.0 + _mod_rows(sc_ref, b, n_batch, is_ctx)) + _mod_rows(sh_ref, b, n_batch, is_ctx)).astype(BF16)
    gmat = gm_ref[...]

    def rms_norm(y, gain):
        return y * lax.rsqrt(_group_mean(y * y, gmat) + HEAD_EPS) * gain

    h_ml = mlf_ref[0] + mlb_ref[0]
    y_ml = jax.nn.sigmoid(mlo_ref[0]) * rms_norm(h_ml - _group_mean(h_ml, gmat), ng_ref[0:1, :])
    g_gl = glg_ref[0]
    y_gla = g_gl * jax.nn.sigmoid(g_gl) * rms_norm(glf_ref[0] + glb_ref[0], ng_ref[1:2, :])
    y_da = (1.0 - lambda_init) * rms_norm(da_ref[0], ng_ref[2:3, :])
    ys = [na_ref[0], y_ml.astype(BF16), y_gla.astype(BF16), y_da.astype(BF16)]
    acc = None
    for j in range(N_BRANCH):
        gate = jax.nn.sigmoid(_dot(h, wg_ref[:, j * D_MODEL:(j + 1) * D_MODEL]))
        term = gate * _dot(ys[j], wb_ref[j])
        acc = term if acc is None else acc + term
    y = _dot(acc.astype(BF16), wo_ref[...])
    y = alpha * x + _mod_rows(gt_ref, b, n_batch, is_ctx) * y
    o_ref[0] = _layer_norm(y, lng_ref[0], lnb_ref[0])


def _merge(xs, mod, na_o, ml_o, ml, gla_o, gla, da_o, norm_g, gmat, w_gates, w_branch, w_out, ln_g, ln_b,
           *, layer, n_rows, tm, alpha, lambda_init):
    n_batch = xs.shape[0]
    rows = mod.shape[1]
    ln_idx = layer * 3 + 1
    kern = functools.partial(_merge_kernel, n_batch=n_batch, tm=tm, alpha=alpha, lambda_init=lambda_init)
    tok = lambda j: pl.BlockSpec((1, tm, BRANCH_W), lambda b, t: (b, t, j))
    return pl.pallas_call(
        kern,
        grid=(n_batch, n_rows // tm),
        in_specs=[
            pl.BlockSpec((1, tm, D_MODEL), lambda b, t: (b, t, 0)),
            _mod_spec(rows, layer, 3), _mod_spec(rows, layer, 4), _mod_spec(rows, layer, 5),
            tok(0), tok(0), tok(0), tok(3), tok(0), tok(0), tok(2), tok(0),
            pl.BlockSpec((8, BRANCH_W), lambda b, t: (0, 0)),
            pl.BlockSpec((BRANCH_W, BRANCH_W), lambda b, t: (0, 0)),
            _const_spec((D_MODEL, N_BRANCH * D_MODEL), lambda b, t: (0, 0)),
            _const_spec((None, N_BRANCH, BRANCH_W, D_MODEL), lambda b, t: (layer, 0, 0, 0)),
            _const_spec((None, D_MODEL, D_MODEL), lambda b, t: (layer, 0, 0)),
            pl.BlockSpec((1, 1, D_MODEL), lambda b, t: (ln_idx, 0, 0)),
            pl.BlockSpec((1, 1, D_MODEL), lambda b, t: (ln_idx, 0, 0)),
        ],
        out_specs=pl.BlockSpec((1, tm, D_MODEL), lambda b, t: (b, t, 0)),
        out_shape=jax.ShapeDtypeStruct((n_batch, n_rows, D_MODEL), F32),
        compiler_params=pltpu.CompilerParams(
            dimension_semantics=("arbitrary", "arbitrary"), vmem_limit_bytes=V7X_VMEM_LIMIT),
        name=f"merge_l{layer}",
    )(xs, mod, mod, mod, na_o, ml_o[0], ml_o[1], ml, gla_o[0], gla_o[1], gla, da_o, norm_g, gmat,
      w_gates, w_branch, w_out, ln_g, ln_b)


def _rope_tables():
    t = jnp.arange(SEQ)
    n_f = DA_DQK // 4
    inv = ROPE_BASE ** (-jnp.arange(n_f, dtype=F32) / n_f)

    def cs(pos):
        ang = pos.astype(F32)[:, None] * inv
        return jnp.cos(ang), jnp.sin(ang)

    (cr, sr), (cc, sc) = cs(t // GRID_W), cs(t % GRID_W)
    cos32 = jnp.concatenate([cr, cr, cc, cc], axis=-1)
    sin32 = jnp.concatenate([-sr, sr, -sc, sc], axis=-1)
    reps = BRANCH_W // DA_DQK
    cos = jnp.concatenate([jnp.tile(cos32, (1, reps)), jnp.ones((CTX_LEN, BRANCH_W), F32)], axis=0)
    sin = jnp.concatenate([jnp.tile(sin32, (1, reps)), jnp.zeros((CTX_LEN, BRANCH_W), F32)], axis=0)
    return cos, sin


def _swap_perm():
    n_f = DA_DQK // 4
    base = np.concatenate([np.arange(n_f, 2 * n_f), np.arange(0, n_f),
                           np.arange(3 * n_f, 4 * n_f), np.arange(2 * n_f, 3 * n_f)])
    return np.concatenate([g * DA_DQK + base for g in range(BRANCH_W // DA_DQK)])


def _prep_mix_weights(w):
    na, ml, ml_if = w[:, 0:768], w[:, 768:1792], w[:, 1792:1808]
    gla, gla_a = w[:, 1808:2576], w[:, 2576:2608]
    dq, dk, dv = w[:, 2608:2864], w[:, 2864:3120], w[:, 3120:3376]
    gates = w[:, 3376:]
    perm = _swap_perm()
    pad = jnp.zeros((D_MODEL, 128 - ml_if.shape[1] - gla_a.shape[1]), w.dtype)
    w_all = jnp.concatenate([na, ml, gla, ml_if, gla_a, pad, dq, dq[:, perm], dk, dk[:, perm], dv], axis=1)
    return w_all.astype(BF16), gates.astype(BF16)


def _col_scale():
    cs = np.ones((1, C_SMALL), np.float32)
    cs[:, C_NA:C_NA + 256] = 64 ** -0.5 * LOG2E
    cs[:, C_ML:C_ML + 256] = 64 ** -0.5
    cs[:, C_GLA:C_GLA + 128] = GLA_DK ** -0.5
    return jnp.asarray(cs)


def _group_mean_matrix():
    head = np.arange(BRANCH_W) // (BRANCH_W // N_HEADS)
    return jnp.asarray((head[:, None] == head[None, :]) / (BRANCH_W // N_HEADS), dtype=BF16)


def _da_group_sum_matrix():
    grp = np.arange(BRANCH_W) // DA_DQK
    return jnp.asarray(grp[:, None] == grp[None, :], dtype=BF16)


def kernel(x, c, ctx, c_ctx, w_ada, b_ada, ln_g, ln_b, ffn_w_in, ffn_w_out, w_mix_in, na_rpb, ml_gate_b,
           ml_norm_g, gla_w_a2, gla_b_a, gla_norm_g, da_lambda, da_norm_g, w_branch, w_out):
    n_batch = x.shape[0]
    depth = w_ada.shape[0]
    alpha = (2 * depth) ** 0.25
    mod_rows = -(-(n_batch + 1) // 8) * 8
    cc = jnp.concatenate([c, c_ctx[None, :], jnp.zeros((mod_rows - n_batch - 1, D_MODEL), F32)], axis=0)
    mod = _ada(cc, w_ada, b_ada)

    xs = x
    w_in_bf = ffn_w_in.astype(BF16)
    w_out_bf = ffn_w_out.astype(BF16)
    w_branch_bf = w_branch.astype(BF16)
    w_o_bf = w_out.astype(BF16)
    ln_g3 = ln_g.reshape(depth * 3, 1, D_MODEL)
    ln_b3 = ln_b.reshape(depth * 3, 1, D_MODEL)
    cos_tab, sin_tab = _rope_tables()
    colscale = _col_scale()
    gmat = _group_mean_matrix()
    da_gsum = _da_group_sum_matrix()

    for l in range(depth):
        ctx_out = l < depth - 1
        lambda_init = 0.8 - 0.6 * math.exp(-0.3 * l)
        w_all, w_gates = _prep_mix_weights(w_mix_in[l])
        bias = _na_bias_tables(na_rpb[l])
        gate_b = jnp.concatenate([ml_gate_b[l], jnp.zeros((128 - ml_gate_b.shape[1],), F32)])[None, :]
        w2pad = jnp.zeros((2, 128, 128), F32)
        for d in range(2):
            lo = SMALL_A_OFF + d * GLA_RANK
            w2pad = w2pad.at[d, lo:lo + GLA_RANK, :].set(gla_w_a2[l, d])
        ba = gla_b_a[l][:, None, :]
        norm_g = jnp.concatenate([ml_norm_g[l][None], gla_norm_g[l][None], da_norm_g[l][None],
                                  jnp.zeros((5, BRANCH_W), F32)], axis=0)

        xs = _ffn(xs, mod, w_in_bf, w_out_bf, ln_g3, ln_b3, layer=l, sub=0, mod0=0,
                  n_rows=NTOK, tm=TM_ALL, alpha=alpha, ctx=ctx if l == 0 else None)
        na, na_vt, ml, gla, small, da, da_vt = _inproj(xs, mod, w_all, colscale, cos_tab, sin_tab,
                                                       layer=l, tm=TM_ALL)
        na_o = _neighbourhood(na, na_vt, bias, layer=l, with_ctx=ctx_out)
        da_o = _diff_attention(da, da_vt, da_lambda, da_gsum, layer=l, lambda_init=lambda_init,
                               with_ctx=ctx_out)
        ml_o = _mlstm(ml, small, gate_b, layer=l)
        gla_o = _gla(gla, small, w2pad, ba, layer=l)
        n_rows, tm = (NTOK, TM_ALL) if ctx_out else (SEQ, TM_LAT)
        xs = _merge(xs, mod, na_o, ml_o, ml, gla_o, gla, da_o, norm_g, gmat, w_gates, w_branch_bf, w_o_bf,
                    ln_g3, ln_b3, layer=l, n_rows=n_rows, tm=tm, alpha=alpha, lambda_init=lambda_init)
        xs = _ffn(xs, mod, w_in_bf, w_out_bf, ln_g3, ln_b3, layer=l, sub=1, mod0=6,
                  n_rows=n_rows, tm=tm, alpha=alpha)
    return xs
```
